```python
import jax, jax.numpy as jnp
from jax import lax
import numpy as np

D_MODEL = 2048
BATCH = 4
SEQ = 2048
DEPTH = 4
DEC_BATCH = 8
DEC_SEQ = 4
PAST_LEN = 16384
PAGE_SIZE = 128

HEAD_DIM = 128
MIX_WIDTH = D_MODEL
MEM_HEADS = 4
MEM_WIDTH = MEM_HEADS * HEAD_DIM
TOK_WIDTH = MIX_WIDTH - MEM_WIDTH
SB_HEADS = TOK_WIDTH // HEAD_DIM
POOL_WINDOWS = (2, 4, 8, 16)
N_POOL_GROUPS = len(POOL_WINDOWS)
POOL_GROUP = TOK_WIDTH // N_POOL_GROUPS
POOL_BUF = max(POOL_WINDOWS) - 1
MEM_LEN = 256
D_FF = ((8 * D_MODEL // 3 + 255) // 256) * 256
N_MIXERS = 2
N_POOL_LAYERS = (DEPTH + 1) // 2
N_SB_LAYERS = DEPTH // 2
SB_BLOCK = 128
SB_BIAS_INIT = -6.0
RMS_EPS = 1e-6

kernel_name = 'hybrid_pool_stickbreak_memory_decoder_step'


def rmsnorm(x, g):
    x32 = x.astype(jnp.float32)
    y = x32 * lax.rsqrt(jnp.mean(x32 * x32, axis=-1, keepdims=True) + RMS_EPS)
    return (y * g.astype(jnp.float32)).astype(x.dtype)


def ffn_half(x, g, w_gate_up, w_down):
    h = rmsnorm(x, g[0])
    gate, up = jnp.split(h @ w_gate_up, 2, axis=-1)
    y = (jax.nn.silu(gate) * up) @ w_down
    return x + 0.5 * rmsnorm(y, g[1])


def memory_kv(mem, g, w_kv):
    b, m, _ = mem.shape
    k, v = jnp.split(rmsnorm(mem, g) @ w_kv, 2, axis=-1)
    return k.reshape(b, m, MEM_HEADS, HEAD_DIM), v.reshape(b, m, MEM_HEADS, HEAD_DIM)


def memory_attend(qm, mk, mv):
    b, t, _ = qm.shape
    q = qm.reshape(b, t, MEM_HEADS, HEAD_DIM).astype(jnp.float32)
    s = jnp.einsum('bthd,bmhd->bhtm', q, mk.astype(jnp.float32)) * HEAD_DIM ** -0.5
    p = jax.nn.softmax(s, axis=-1)
    o = jnp.einsum('bhtm,bmhd->bthd', p, mv.astype(jnp.float32))
    return o.reshape(b, t, MEM_WIDTH).astype(qm.dtype)


def pool_mix(u_ext, n_prefix, group_maps, scale):
    b, length, c = u_ext.shape
    t = length - n_prefix
    u32 = u_ext.astype(jnp.float32)
    csum = jnp.concatenate([jnp.zeros((b, 1, c), jnp.float32), jnp.cumsum(u32, axis=1)], axis=1)
    end = n_prefix + 1 + jnp.arange(t, dtype=jnp.int32)
    c_end = csum[:, n_prefix + 1:]
    u_self = u32[:, n_prefix:]
    diffs = []
    for g, w in enumerate(POOL_WINDOWS):
        sl = slice(g * POOL_GROUP, (g + 1) * POOL_GROUP)
        start = jnp.maximum(end - w, 0)
        count = (end - start).astype(jnp.float32)
        window_sum = c_end[..., sl] - jnp.take(csum[..., sl], start, axis=1)
        diffs.append(window_sum / count[None, :, None] - u_self[..., sl])
    d = jnp.stack(diffs, axis=2)
    mixed = jnp.einsum('btgc,gce->btge', d, group_maps.astype(jnp.float32))
    return (mixed.reshape(b, t, c) * scale.astype(jnp.float32)).astype(u_ext.dtype)


def stick_breaking(q, k, v, bias, q_pos, k_pos):
    z = jnp.einsum('bqhd,bkhd->bhqk', q.astype(jnp.float32), k.astype(jnp.float32)) * HEAD_DIM ** -0.5
    z = z + bias.astype(jnp.float32)[None, :, None, None]
    visible = k_pos[None, :] < q_pos[:, None]
    log_not_break = jnp.where(visible, -jax.nn.softplus(z), 0.0)
    log_survive = lax.cumsum(log_not_break, axis=3, reverse=True) - log_not_break
    weight = jnp.where(visible, jnp.exp(jax.nn.log_sigmoid(z) + log_survive), 0.0)
    out = jnp.einsum('bhqk,bkhd->bqhd', weight, v.astype(jnp.float32))
    return out.astype(v.dtype)


def stick_breaking_prompt(q, k, v, bias):
    b, t, h, dh = q.shape
    nb = t // SB_BLOCK
    k_pos = jnp.arange(t, dtype=jnp.int32)
    q_blocks = q.reshape(b, nb, SB_BLOCK, h, dh).transpose(1, 0, 2, 3, 4)
    starts = jnp.arange(nb, dtype=jnp.int32) * SB_BLOCK

    def one_block(args):
        q_blk, start = args
        return stick_breaking(q_blk, k, v, bias, start + jnp.arange(SB_BLOCK, dtype=jnp.int32), k_pos)

    out = lax.map(one_block, (q_blocks, starts))
    return out.transpose(1, 0, 2, 3, 4).reshape(b, t, h, dh)


def pool_sublayer(x, prefix, g, w_in, group_maps, scale, w_out, mk, mv):
    h = rmsnorm(x, g[0])
    proj = h @ w_in
    u, qm = proj[..., :TOK_WIDTH], proj[..., TOK_WIDTH:]
    if prefix is None:
        n_prefix, u_ext = 0, u
    else:
        n_prefix = prefix.shape[1]
        u_ext = jnp.concatenate([prefix.astype(u.dtype), u], axis=1)
    mixed = pool_mix(u_ext, n_prefix, group_maps, scale)
    om = memory_attend(qm, mk, mv)
    y = jnp.concatenate([mixed, om], axis=-1) @ w_out
    return x + rmsnorm(y, g[1]), u_ext[:, -POOL_BUF:]


def sb_sublayer(x, past_k, past_v, g, w_in, bias, w_out, mk, mv):
    b, t, _ = x.shape
    h = rmsnorm(x, g[0])
    proj = h @ w_in
    q = proj[..., :TOK_WIDTH].reshape(b, t, SB_HEADS, HEAD_DIM)
    k = proj[..., TOK_WIDTH:2 * TOK_WIDTH].reshape(b, t, SB_HEADS, HEAD_DIM)
    v = proj[..., 2 * TOK_WIDTH:3 * TOK_WIDTH].reshape(b, t, SB_HEADS, HEAD_DIM)
    qm = proj[..., 3 * TOK_WIDTH:]
    if past_k is None:
        o = stick_breaking_prompt(q, k, v, bias)
    else:
        past_len = past_k.shape[1]
        k_all = jnp.concatenate([past_k.astype(k.dtype), k], axis=1)
        v_all = jnp.concatenate([past_v.astype(v.dtype), v], axis=1)
        q_pos = past_len + jnp.arange(t, dtype=jnp.int32)
        k_pos = jnp.arange(past_len + t, dtype=jnp.int32)
        o = stick_breaking(q, k_all, v_all, bias, q_pos, k_pos)
    om = memory_attend(qm, mk, mv)
    y = jnp.concatenate([o.reshape(b, t, TOK_WIDTH), om], axis=-1) @ w_out
    return x + rmsnorm(y, g[1]), k, v


def setup_inputs(seed: int = 0) -> dict:
    key = jax.random.key(seed)
    keys = iter(jax.random.split(key, 40))

    def nrm(shape, scale=1.0):
        return jax.random.normal(next(keys), shape, jnp.float32) * scale

    def gain(shape):
        return 1.0 + 0.05 * nrm(shape)

    n_pages = PAST_LEN // PAGE_SIZE
    n_phys = (DEC_BATCH * n_pages * 5 + 3) // 4
    perm = jax.random.permutation(next(keys), n_phys)
    page_table = perm[:DEC_BATCH * n_pages].reshape(DEC_BATCH, n_pages).astype(jnp.int32)
    d_in = D_MODEL ** -0.5
    return {
        'x_prompt': nrm((BATCH, SEQ, D_MODEL)),
        'x_sample': nrm((DEC_BATCH, DEC_SEQ, D_MODEL)),
        'state_pool': nrm((N_POOL_LAYERS, DEC_BATCH, POOL_BUF, TOK_WIDTH)),
        'cache_sb_k': nrm((N_SB_LAYERS, n_phys, PAGE_SIZE, SB_HEADS, HEAD_DIM)),
        'cache_sb_v': nrm((N_SB_LAYERS, n_phys, PAGE_SIZE, SB_HEADS, HEAD_DIM)),
        'cache_mem_k': nrm((DEPTH, DEC_BATCH, MEM_LEN, MEM_HEADS, HEAD_DIM)),
        'cache_mem_v': nrm((DEPTH, DEC_BATCH, MEM_LEN, MEM_HEADS, HEAD_DIM)),
        'page_table': page_table,
        'mem_prompt': nrm((BATCH, MEM_LEN, D_MODEL)),
        'norm_ffn1': gain((DEPTH, 2, D_MODEL)),
        'ffn1_w_gate_up': nrm((DEPTH, D_MODEL, 2 * D_FF), d_in),
        'ffn1_w_down': nrm((DEPTH, D_FF, D_MODEL), D_FF ** -0.5),
        'norm_mix': gain((DEPTH, 2, D_MODEL)),
        'norm_mem': gain((DEPTH, D_MODEL)),
        'mem_w_kv': nrm((DEPTH, D_MODEL, 2 * MEM_WIDTH), d_in),
        'pool_w_in': nrm((N_POOL_LAYERS, D_MODEL, MIX_WIDTH), d_in),
        'pool_group_maps': nrm((N_POOL_LAYERS, N_POOL_GROUPS, POOL_GROUP, POOL_GROUP), POOL_GROUP ** -0.5),
        'pool_scale': gain((N_POOL_LAYERS, TOK_WIDTH)),
        'pool_w_out': nrm((N_POOL_LAYERS, MIX_WIDTH, D_MODEL), MIX_WIDTH ** -0.5),
        'sb_w_in': nrm((N_SB_LAYERS, D_MODEL, 3 * TOK_WIDTH + MEM_WIDTH), d_in),
        'sb_logit_bias': SB_BIAS_INIT + 0.5 * nrm((N_SB_LAYERS, SB_HEADS)),
        'sb_w_out': nrm((N_SB_LAYERS, MIX_WIDTH, D_MODEL), MIX_WIDTH ** -0.5),
        'norm_ffn2': gain((DEPTH, 2, D_MODEL)),
        'ffn2_w_gate_up': nrm((DEPTH, D_MODEL, 2 * D_FF), d_in),
        'ffn2_w_down': nrm((DEPTH, D_FF, D_MODEL), D_FF ** -0.5),
    }


def reference(x_prompt, x_sample, state_pool, cache_sb_k, cache_sb_v, cache_mem_k, cache_mem_v,
              page_table, mem_prompt, norm_ffn1, ffn1_w_gate_up, ffn1_w_down, norm_mix, norm_mem,
              mem_w_kv, pool_w_in, pool_group_maps, pool_scale, pool_w_out, sb_w_in, sb_logit_bias,
              sb_w_out, norm_ffn2, ffn2_w_gate_up, ffn2_w_down):
    n_pages = page_table.shape[1]
    past_len = n_pages * PAGE_SIZE
    dec_batch = x_sample.shape[0]
    xp, xs = x_prompt, x_sample
    pool_p, pool_s, kp, vp, ks, vs, mkp, mvp = [], [], [], [], [], [], [], []
    for i in range(DEPTH):
        li = i // N_MIXERS
        xp = ffn_half(xp, norm_ffn1[i], ffn1_w_gate_up[i], ffn1_w_down[i])
        xs = ffn_half(xs, norm_ffn1[i], ffn1_w_gate_up[i], ffn1_w_down[i])
        mk_p, mv_p = memory_kv(mem_prompt, norm_mem[i], mem_w_kv[i])
        mkp.append(mk_p)
        mvp.append(mv_p)
        mk_s, mv_s = cache_mem_k[i], cache_mem_v[i]
        if i % N_MIXERS == 0:
            xp, buf_p = pool_sublayer(xp, None, norm_mix[i], pool_w_in[li], pool_group_maps[li],
                                      pool_scale[li], pool_w_out[li], mk_p, mv_p)
            xs, buf_s = pool_sublayer(xs, state_pool[li], norm_mix[i], pool_w_in[li], pool_group_maps[li],
                                      pool_scale[li], pool_w_out[li], mk_s, mv_s)
            pool_p.append(buf_p)
            pool_s.append(buf_s)
        else:
            past_k = cache_sb_k[li][page_table].reshape(dec_batch, past_len, SB_HEADS, HEAD_DIM)
            past_v = cache_sb_v[li][page_table].reshape(dec_batch, past_len, SB_HEADS, HEAD_DIM)
            xp, k_p, v_p = sb_sublayer(xp, None, None, norm_mix[i], sb_w_in[li], sb_logit_bias[li],
                                       sb_w_out[li], mk_p, mv_p)
            xs, k_s, v_s = sb_sublayer(xs, past_k, past_v, norm_mix[i], sb_w_in[li], sb_logit_bias[li],
                                       sb_w_out[li], mk_s, mv_s)
            kp.append(k_p)
            vp.append(v_p)
            ks.append(k_s)
            vs.append(v_s)
        xp = ffn_half(xp, norm_ffn2[i], ffn2_w_gate_up[i], ffn2_w_down[i])
        xs = ffn_half(xs, norm_ffn2[i], ffn2_w_gate_up[i], ffn2_w_down[i])
    return (xp, xs, jnp.stack(pool_p), jnp.stack(kp), jnp.stack(vp), jnp.stack(mkp), jnp.stack(mvp),
            jnp.stack(pool_s), jnp.stack(ks), jnp.stack(vs))
```

```python
import functools

import jax
import jax.numpy as jnp
from jax import lax
from jax.experimental import pallas as pl
from jax.experimental.pallas import tpu as pltpu

F32 = jnp.float32
BF16 = jnp.bfloat16

HEAD_DIM = 128
MEM_HEADS = 4
MEM_WIDTH = MEM_HEADS * HEAD_DIM
POOL_WINDOWS = (2, 4, 8, 16)
POOL_BUF = max(POOL_WINDOWS) - 1
HALO = 16
N_MIXERS = 2
SB_BLOCK = 128
RMS_EPS = 1e-6
SAMPLE_ROWS = 8
MIB = 1024 * 1024
VMEM_CAP = 60 * MIB

_NT = (((1,), (1,)), ((), ()))


def _params(semantics, vmem_bytes):
    return pltpu.CompilerParams(dimension_semantics=semantics,
                                vmem_limit_bytes=int(min(vmem_bytes, VMEM_CAP)))


def _rms(x, g):
    ms = jnp.mean(x * x, axis=-1, keepdims=True)
    return x * lax.rsqrt(ms + RMS_EPS) * g


def _softplus_parts(z):
    t = jnp.log1p(jnp.exp(-jnp.abs(z)))
    return jnp.maximum(z, 0.0) + t, jnp.minimum(z, 0.0) - t


def _suffix_matrix():
    r = lax.broadcasted_iota(jnp.int32, (SB_BLOCK, 2 * SB_BLOCK), 0)
    c = lax.broadcasted_iota(jnp.int32, (SB_BLOCK, 2 * SB_BLOCK), 1)
    return jnp.where(c >= SB_BLOCK, 1.0, jnp.where(r > c, 1.0, 0.0)).astype(BF16)


def _suffix_sums(lnb, tmat):
    hi = lnb.astype(BF16)
    lo = (lnb - hi.astype(F32)).astype(BF16)
    return (jnp.dot(hi, tmat, preferred_element_type=F32) +
            jnp.dot(lo, tmat, preferred_element_type=F32))


def _ffn_kernel(x_ref, g_ref, wg_ref, wu_ref, wd_ref, o_ref, h_ref, acc_ref):
    j = pl.program_id(1)

    @pl.when(j == 0)
    def _():
        h_ref[...] = _rms(x_ref[...], g_ref[0:1, :]).astype(BF16)

    h = h_ref[...]
    gate = jnp.dot(h, wg_ref[...], preferred_element_type=F32)
    up = jnp.dot(h, wu_ref[...], preferred_element_type=F32)
    a = (gate * jax.nn.sigmoid(gate) * up).astype(BF16)
    part = jnp.dot(a, wd_ref[...], preferred_element_type=F32)

    @pl.when(j == 0)
    def _():
        acc_ref[...] = part

    @pl.when(j > 0)
    def _():
        acc_ref[...] += part

    @pl.when(j == pl.num_programs(1) - 1)
    def _():
        o_ref[...] = x_ref[...] + 0.5 * _rms(acc_ref[...], g_ref[1:2, :])


def _ffn_half(x, g, w_gu, w_d, *, tm, tf):
    m, d = x.shape
    f = w_d.shape[0]
    nf = f // tf
    vmem = 2 * (2 * tm * d * 4) + tm * d * (2 + 4) + 2 * 3 * d * tf * 2 + 5 * tm * tf * 4 + 4 * MIB
    return pl.pallas_call(
        _ffn_kernel,
        grid=(m // tm, nf),
        in_specs=[pl.BlockSpec((tm, d), lambda i, j: (i, 0)),
                  pl.BlockSpec((2, d), lambda i, j: (0, 0)),
                  pl.BlockSpec((d, tf), lambda i, j: (0, j)),
                  pl.BlockSpec((d, tf), lambda i, j: (0, j + nf)),
                  pl.BlockSpec((tf, d), lambda i, j: (j, 0))],
        out_specs=pl.BlockSpec((tm, d), lambda i, j: (i, 0)),
        out_shape=jax.ShapeDtypeStruct((m, d), F32),
        scratch_shapes=[pltpu.VMEM((tm, d), BF16), pltpu.VMEM((tm, d), F32)],
        compiler_params=_params(("parallel", "arbitrary"), vmem),
        name="ffn_half",
    )(x, g, w_gu, w_gu, w_d)


def _norm_proj_kernel(x_ref, g_ref, w_ref, o_ref, h_ref):
    @pl.when(pl.program_id(1) == 0)
    def _():
        h_ref[...] = _rms(x_ref[...], g_ref[...]).astype(BF16)

    o_ref[...] = jnp.dot(h_ref[...], w_ref[...], preferred_element_type=F32).astype(o_ref.dtype)


def _norm_proj(x, g, w, col0, ncols, out_dtype, *, tm, tn):
    m, d = x.shape
    assert col0 % tn == 0 and ncols % tn == 0 and m % tm == 0
    cb = col0 // tn
    vmem = 2 * tm * d * 4 + tm * d * 2 + 2 * d * tn * 2 + 3 * tm * tn * 4 + 4 * MIB
    return pl.pallas_call(
        _norm_proj_kernel,
        grid=(m // tm, ncols // tn),
        in_specs=[pl.BlockSpec((tm, d), lambda i, j: (i, 0)),
                  pl.BlockSpec((1, d), lambda i, j: (0, 0)),
                  pl.BlockSpec((d, tn), lambda i, j: (0, j + cb))],
        out_specs=pl.BlockSpec((tm, tn), lambda i, j: (i, j)),
        out_shape=jax.ShapeDtypeStruct((m, ncols), out_dtype),
        scratch_shapes=[pltpu.VMEM((tm, d), BF16)],
        compiler_params=_params(("parallel", "arbitrary"), vmem),
        name="norm_proj",
    )(x, g, w)


def _sb_block(z, vis, tmat, carry):
    sp, lsig = _softplus_parts(z)
    lnb = -sp if vis is None else jnp.where(vis, -sp, 0.0)
    sums = _suffix_sums(lnb, tmat)
    w = jnp.exp(lsig + sums[:, :SB_BLOCK] + carry)
    if vis is not None:
        w = jnp.where(vis, w, 0.0)
    return w, carry + sums[:, SB_BLOCK:]


def _sb_prompt_kernel(bias_ref, q_ref, k_ref, v_ref, o_ref, kb_ref, vb_ref, acc_ref, car_ref):
    hd = pl.program_id(1)
    i = pl.program_id(2)

    @pl.when(i == 0)
    def _():
        kb_ref[...] = k_ref[0].astype(BF16)
        vb_ref[...] = v_ref[0].astype(BF16)

    q = q_ref[0]
    bias = bias_ref[hd]
    scale = HEAD_DIM ** -0.5
    tmat = _suffix_matrix()

    def logits(j):
        kj = kb_ref[pl.ds(pl.multiple_of(j * SB_BLOCK, SB_BLOCK), SB_BLOCK), :]
        return lax.dot_general(q, kj, _NT, preferred_element_type=F32) * scale + bias

    def values(j):
        return vb_ref[pl.ds(pl.multiple_of(j * SB_BLOCK, SB_BLOCK), SB_BLOCK), :]

    r = lax.broadcasted_iota(jnp.int32, (SB_BLOCK, SB_BLOCK), 0)
    c = lax.broadcasted_iota(jnp.int32, (SB_BLOCK, SB_BLOCK), 1)
    w, carry = _sb_block(logits(i), c < r, tmat, jnp.zeros((SB_BLOCK, SB_BLOCK), F32))
    acc_ref[...] = jnp.dot(w.astype(BF16), values(i), preferred_element_type=F32)
    car_ref[...] = carry

    def body(jj, _):
        j = i - 1 - jj
        w, carry = _sb_block(logits(j), None, tmat, car_ref[...])
        acc_ref[...] += jnp.dot(w.astype(BF16), values(j), preferred_element_type=F32)
        car_ref[...] = carry
        return 0

    lax.fori_loop(0, i, body, 0)
    o_ref[0] = acc_ref[...].astype(o_ref.dtype)


def _sb_prompt(q, k, v, bias):
    b, t, w = q.shape
    nh = w // HEAD_DIM
    blk = SB_BLOCK
    return pl.pallas_call(
        _sb_prompt_kernel,
        grid=(b, nh, t // blk),
        in_specs=[pl.BlockSpec(memory_space=pltpu.SMEM),
                  pl.BlockSpec((1, blk, HEAD_DIM), lambda bb, h, i: (bb, i, h)),
                  pl.BlockSpec((1, t, HEAD_DIM), lambda bb, h, i: (bb, 0, h)),
                  pl.BlockSpec((1, t, HEAD_DIM), lambda bb, h, i: (bb, 0, h))],
        out_specs=pl.BlockSpec((1, blk, HEAD_DIM), lambda bb, h, i: (bb, i, h)),
        out_shape=jax.ShapeDtypeStruct((b, t, w), BF16),
        scratch_shapes=[pltpu.VMEM((t, HEAD_DIM), BF16), pltpu.VMEM((t, HEAD_DIM), BF16),
                        pltpu.VMEM((blk, HEAD_DIM), F32), pltpu.VMEM((blk, blk), F32)],
        compiler_params=_params(("parallel", "parallel", "arbitrary"), 24 * MIB),
        name="sb_prompt",
    )(bias, q, k, v)


def _sb_decode_kernel(pt_ref, q_ref, knew_ref, vnew_ref, bias_ref, *refs, pages_per_step, n_heads):
    del pt_ref
    k_refs = refs[:pages_per_step]
    v_refs = refs[pages_per_step:2 * pages_per_step]
    o_ref, acc_ref, car_ref = refs[2 * pages_per_step:]
    p = pl.program_id(1)
    rows = n_heads * SAMPLE_ROWS
    scale = HEAD_DIM ** -0.5
    tmat = _suffix_matrix()
    bias = bias_ref[...]
    q = q_ref[0]

    def block(k_ref, v_ref, vis):
        zs = []
        for h in range(n_heads):
            qh = q[h * SAMPLE_ROWS:(h + 1) * SAMPLE_ROWS].astype(BF16)
            kh = k_ref[:, h * HEAD_DIM:(h + 1) * HEAD_DIM].astype(BF16)
            zs.append(lax.dot_general(qh, kh, _NT, preferred_element_type=F32))
        z = jnp.concatenate(zs, axis=0) * scale + bias
        w, carry = _sb_block(z, vis, tmat, car_ref[...])
        outs = []
        for h in range(n_heads):
            wh = w[h * SAMPLE_ROWS:(h + 1) * SAMPLE_ROWS].astype(BF16)
            vh = v_ref[:, h * HEAD_DIM:(h + 1) * HEAD_DIM].astype(BF16)
            outs.append(jnp.dot(wh, vh, preferred_element_type=F32))
        acc_ref[...] += jnp.concatenate(outs, axis=0)
        car_ref[...] = carry

    @pl.when(p == 0)
    def _():
        acc_ref[...] = jnp.zeros_like(acc_ref)
        car_ref[...] = jnp.zeros_like(car_ref)
        t = lax.broadcasted_iota(jnp.int32, (rows, SB_BLOCK), 0) % SAMPLE_ROWS
        s = lax.broadcasted_iota(jnp.int32, (rows, SB_BLOCK), 1)
        block(knew_ref.at[0], vnew_ref.at[0], s < t)

    for c in range(pages_per_step):
        block(k_refs[c], v_refs[c], None)

    @pl.when(p == pl.num_programs(1) - 1)
    def _():
        o_ref[0] = acc_ref[...]


def _sb_decode(q_rows, k_new, v_new, bias_rows, cache_k, cache_v, page_flat, layer, *, pages_per_step):
    bsz, rows, _ = q_rows.shape
    n_heads = rows // SAMPLE_ROWS
    width = n_heads * HEAD_DIM
    page = cache_k.shape[2]
    n_pages = page_flat.shape[0] // bsz
    assert page == SB_BLOCK and n_pages % pages_per_step == 0
    steps = n_pages // pages_per_step

    def page_map(c):
        return lambda b, p, pt: (layer, pt[b * n_pages + n_pages - 1 - (p * pages_per_step + c)], 0, 0)

    page_specs = [pl.BlockSpec((None, None, page, width), page_map(c)) for c in range(pages_per_step)]
    grid_spec = pltpu.PrefetchScalarGridSpec(
        num_scalar_prefetch=1,
        grid=(bsz, steps),
        in_specs=[pl.BlockSpec((1, rows, HEAD_DIM), lambda b, p, pt: (b, 0, 0)),
                  pl.BlockSpec((1, page, width), lambda b, p, pt: (b, 0, 0)),
                  pl.BlockSpec((1, page, width), lambda b, p, pt: (b, 0, 0)),
                  pl.BlockSpec((rows, SB_BLOCK), lambda b, p, pt: (0, 0))] + page_specs + page_specs,
        out_specs=pl.BlockSpec((1, rows, HEAD_DIM), lambda b, p, pt: (b, 0, 0)),
        scratch_shapes=[pltpu.VMEM((rows, HEAD_DIM), F32), pltpu.VMEM((rows, SB_BLOCK), F32)],
    )
    vmem = 2 * 2 * (pages_per_step + 1) * page * width * 4 + 8 * MIB
    return pl.pallas_call(
        functools.partial(_sb_decode_kernel, pages_per_step=pages_per_step, n_heads=n_heads),
        grid_spec=grid_spec,
        out_shape=jax.ShapeDtypeStruct((bsz, rows, HEAD_DIM), F32),
        compiler_params=_params(("parallel", "arbitrary"), vmem),
        name="sb_decode",
    )(page_flat, q_rows, k_new, v_new, bias_rows, *([cache_k] * pages_per_step), *([cache_v] * pages_per_step))


def _mixer_kernel(*refs, pool, n_prefix, halo_valid_from):
    if pool:
        (u_ref, halo_ref, qm_ref, mk_ref, mv_ref, x_ref, g_ref, wo_ref, gm_ref, sc_ref,
         o_ref, cat_ref, ubuf_ref) = refs
    else:
        tok_ref, qm_ref, mk_ref, mv_ref, x_ref, g_ref, wo_ref, o_ref, cat_ref = refs
    t = pl.program_id(1)
    tm = x_ref.shape[1]
    tok_width = cat_ref.shape[1] - MEM_WIDTH

    if pool:
        group = tok_width // len(POOL_WINDOWS)
        halo = halo_ref[0]
        ubuf_ref[0:HALO, :] = jnp.where(t >= halo_valid_from, halo, 0.0)
        ubuf_ref[HALO:HALO + tm, :] = u_ref[0]
        pos = t * tm + lax.broadcasted_iota(jnp.int32, (tm, 1), 0) + (n_prefix + 1)
        for gi, win in enumerate(POOL_WINDOWS):
            cols = slice(gi * group, (gi + 1) * group)
            own = ubuf_ref[HALO:HALO + tm, cols]
            tot = own
            for back in range(1, win):
                tot = tot + ubuf_ref[HALO - back:HALO - back + tm, cols]
            count = jnp.minimum(pos, win).astype(F32)
            diff = tot / count - own
            mixed = jnp.dot(diff.astype(BF16), gm_ref[gi], preferred_element_type=F32) * sc_ref[:, cols]
            cat_ref[:, cols] = mixed.astype(BF16)
    else:
        cat_ref[:, 0:tok_width] = tok_ref[0].astype(BF16)

    scale = HEAD_DIM ** -0.5
    for h in range(MEM_HEADS):
        cols = slice(h * HEAD_DIM, (h + 1) * HEAD_DIM)
        qh = qm_ref[0, :, cols].astype(BF16)
        kh = mk_ref[0, :, cols].astype(BF16)
        vh = mv_ref[0, :, cols].astype(BF16)
        s = lax.dot_general(qh, kh, _NT, preferred_element_type=F32) * scale
        e = jnp.exp(s - jnp.max(s, axis=-1, keepdims=True))
        prob = e / jnp.sum(e, axis=-1, keepdims=True)
        oh = jnp.dot(prob.astype(BF16), vh, preferred_element_type=F32)
        cat_ref[:, tok_width + h * HEAD_DIM:tok_width + (h + 1) * HEAD_DIM] = oh.astype(BF16)

    y = jnp.dot(cat_ref[...], wo_ref[...], preferred_element_type=F32)
    o_ref[0] = x_ref[0] + _rms(y, g_ref[1:2, :])


def _mixer_out(x, g, w_out, qm_src, qm_block, mk, mv, *, tm, tok=None, pool_src=None, halo_src=None,
               halo_valid_from=0, n_prefix=0, group_maps=None, pool_scale=None):
    b, t, d = x.shape
    tok_width = d - MEM_WIDTH
    mem_len = mk.shape[1]
    pool = tok is None
    qm_spec = pl.BlockSpec((1, tm, MEM_WIDTH), lambda bb, i: (bb, i, qm_block))
    mem_spec = pl.BlockSpec((1, mem_len, MEM_WIDTH), lambda bb, i: (bb, 0, 0))
    x_spec = pl.BlockSpec((1, tm, d), lambda bb, i: (bb, i, 0))
    g_spec = pl.BlockSpec((2, d), lambda bb, i: (0, 0))
    wo_spec = pl.BlockSpec((d, d), lambda bb, i: (0, 0))
    scratch = [pltpu.VMEM((tm, d), BF16)]
    if pool:
        halo_blocks = tm // HALO
        if halo_src is pool_src:
            halo_map = lambda bb, i: (bb, jnp.maximum(i * halo_blocks - 1, 0), 0)
        else:
            halo_map = lambda bb, i: (bb, 0, 0)
        ng, gw, _ = group_maps.shape
        in_specs = [pl.BlockSpec((1, tm, tok_width), lambda bb, i: (bb, i, 0)),
                    pl.BlockSpec((1, HALO, tok_width), halo_map),
                    qm_spec, mem_spec, mem_spec, x_spec, g_spec, wo_spec,
                    pl.BlockSpec((ng, gw, gw), lambda bb, i: (0, 0, 0)),
                    pl.BlockSpec((1, tok_width), lambda bb, i: (0, 0))]
        args = (pool_src, halo_src, qm_src, mk, mv, x, g, w_out, group_maps, pool_scale)
        scratch.append(pltpu.VMEM((HALO + tm, tok_width), F32))
    else:
        in_specs = [pl.BlockSpec((1, tm, tok_width), lambda bb, i: (bb, i, 0)),
                    qm_spec, mem_spec, mem_spec, x_spec, g_spec, wo_spec]
        args = (tok, qm_src, mk, mv, x, g, w_out)
    vmem = (2 * (tm * tok_width * 4 + tm * MEM_WIDTH * 4 + 2 * tm * d * 4 + 2 * mem_len * MEM_WIDTH * 4 + d * d * 2)
            + (HALO + tm) * tok_width * 4 + tm * d * 2 + 3 * tm * d * 4 + 6 * MIB)
    return pl.pallas_call(
        functools.partial(_mixer_kernel, pool=pool, n_prefix=n_prefix, halo_valid_from=halo_valid_from),
        grid=(b, t // tm),
        in_specs=in_specs,
        out_specs=x_spec,
        out_shape=jax.ShapeDtypeStruct((b, t, d), F32),
        scratch_shapes=scratch,
        compiler_params=_params(("parallel", "arbitrary"), vmem),
        name="mixer_pool" if pool else "mixer_sb",
    )(*args)


def kernel(x_prompt, x_sample, state_pool, cache_sb_k, cache_sb_v, cache_mem_k, cache_mem_v, page_table, mem_prompt, norm_ffn1, ffn1_w_gate_up, ffn1_w_down, norm_mix, norm_mem, mem_w_kv, pool_w_in, pool_group_maps, pool_scale, pool_w_out, sb_w_in, sb_logit_bias, sb_w_out, norm_ffn2, ffn2_w_gate_up, ffn2_w_down):
    b, t, d = x_prompt.shape
    bs, ts, _ = x_sample.shape
    depth = norm_ffn1.shape[0]
    mem_len = mem_prompt.shape[1]
    tok_width = d - MEM_WIDTH
    n_heads = tok_width // HEAD_DIM
    n_phys, page = cache_sb_k.shape[1], cache_sb_k.shape[2]
    assert ts <= SAMPLE_ROWS and page == SB_BLOCK

    tm_ffn, tf_ffn = 512, 512
    tm_proj, tn_proj = 1024, 512
    tm_mix = 256
    ms = bs * SAMPLE_ROWS

    xp = x_prompt.reshape(b * t, d)
    xs = jnp.pad(x_sample, ((0, 0), (0, SAMPLE_ROWS - ts), (0, 0))).reshape(ms, d)
    mem2 = mem_prompt.reshape(b * mem_len, d)
    cache_k = cache_sb_k.reshape(cache_sb_k.shape[0], n_phys, page, tok_width)
    cache_v = cache_sb_v.reshape(cache_sb_v.shape[0], n_phys, page, tok_width)
    page_flat = page_table.reshape(-1).astype(jnp.int32)

    pool_p, pool_s, kp, vp, ks, vs, mkp, mvp = [], [], [], [], [], [], [], []
    for i in range(depth):
        li = i // N_MIXERS
        w_gu, w_d = ffn1_w_gate_up[i].astype(BF16), ffn1_w_down[i].astype(BF16)
        xp = _ffn_half(xp, norm_ffn1[i], w_gu, w_d, tm=tm_ffn, tf=tf_ffn)
        xs = _ffn_half(xs, norm_ffn1[i], w_gu, w_d, tm=ms, tf=tf_ffn)

        w_kv = mem_w_kv[i].astype(BF16)
        g_mem = norm_mem[i][None]
        mk_p = _norm_proj(mem2, g_mem, w_kv, 0, MEM_WIDTH, F32, tm=tm_proj, tn=tn_proj)
        mv_p = _norm_proj(mem2, g_mem, w_kv, MEM_WIDTH, MEM_WIDTH, F32, tm=tm_proj, tn=tn_proj)
        mk_p = mk_p.reshape(b, mem_len, MEM_WIDTH)
        mv_p = mv_p.reshape(b, mem_len, MEM_WIDTH)
        mkp.append(mk_p.reshape(b, mem_len, MEM_HEADS, HEAD_DIM))
        mvp.append(mv_p.reshape(b, mem_len, MEM_HEADS, HEAD_DIM))
        mk_s = cache_mem_k[i].reshape(bs, mem_len, MEM_WIDTH)
        mv_s = cache_mem_v[i].reshape(bs, mem_len, MEM_WIDTH)

        g_mix = norm_mix[i]
        g_in = g_mix[0:1]
        xp3 = xp.reshape(b, t, d)
        xs3 = xs.reshape(bs, SAMPLE_ROWS, d)
        if i % N_MIXERS == 0:
            w_in, w_out = pool_w_in[li].astype(BF16), pool_w_out[li].astype(BF16)
            gmaps, scale = pool_group_maps[li].astype(BF16), pool_scale[li][None]
            proj_p = _norm_proj(xp, g_in, w_in, 0, d, F32, tm=tm_proj, tn=tn_proj).reshape(b, t, d)
            proj_s = _norm_proj(xs, g_in, w_in, 0, d, F32, tm=ms, tn=tn_proj).reshape(bs, SAMPLE_ROWS, d)
            qm_block = tok_width // MEM_WIDTH
            xp3 = _mixer_out(xp3, g_mix, w_out, proj_p, qm_block, mk_p, mv_p, tm=tm_mix, pool_src=proj_p,
                             halo_src=proj_p, halo_valid_from=1, n_prefix=0, group_maps=gmaps, pool_scale=scale)
            prefix = state_pool[li]
            n_prefix = prefix.shape[1]
            halo_s = jnp.pad(prefix, ((0, 0), (HALO - n_prefix, 0), (0, 0)))
            xs3 = _mixer_out(xs3, g_mix, w_out, proj_s, qm_block, mk_s, mv_s, tm=SAMPLE_ROWS, pool_src=proj_s,
                             halo_src=halo_s, halo_valid_from=0, n_prefix=n_prefix, group_maps=gmaps,
                             pool_scale=scale)
            pool_p.append(proj_p[:, t - POOL_BUF:, :tok_width])
            u_ext = jnp.concatenate([prefix, proj_s[:, :ts, :tok_width]], axis=1)
            pool_s.append(u_ext[:, u_ext.shape[1] - POOL_BUF:])
        else:
            w_in, w_out = sb_w_in[li].astype(BF16), sb_w_out[li].astype(BF16)
            bias = sb_logit_bias[li].astype(F32)
            q_p = _norm_proj(xp, g_in, w_in, 0, tok_width, BF16, tm=tm_proj, tn=tn_proj)
            k_p = _norm_proj(xp, g_in, w_in, tok_width, tok_width, F32, tm=tm_proj, tn=tn_proj)
            v_p = _norm_proj(xp, g_in, w_in, 2 * tok_width, tok_width, F32, tm=tm_proj, tn=tn_proj)
            qm_p = _norm_proj(xp, g_in, w_in, 3 * tok_width, MEM_WIDTH, F32, tm=tm_proj, tn=tn_proj)
            k_p3 = k_p.reshape(b, t, tok_width)
            v_p3 = v_p.reshape(b, t, tok_width)
            o_p = _sb_prompt(q_p.reshape(b, t, tok_width), k_p3, v_p3, bias)
            xp3 = _mixer_out(xp3, g_mix, w_out, qm_p.reshape(b, t, MEM_WIDTH), 0, mk_p, mv_p, tm=tm_mix, tok=o_p)
            kp.append(k_p3.reshape(b, t, n_heads, HEAD_DIM))
            vp.append(v_p3.reshape(b, t, n_heads, HEAD_DIM))

            q_s = _norm_proj(xs, g_in, w_in, 0, tok_width, F32, tm=ms, tn=tn_proj)
            k_s = _norm_proj(xs, g_in, w_in, tok_width, tok_width, F32, tm=ms, tn=tn_proj)
            v_s = _norm_proj(xs, g_in, w_in, 2 * tok_width, tok_width, F32, tm=ms, tn=tn_proj)
            qm_s = _norm_proj(xs, g_in, w_in, 3 * tok_width, MEM_WIDTH, F32, tm=ms, tn=tn_proj)
            k_s3 = k_s.reshape(bs, SAMPLE_ROWS, tok_width)
            v_s3 = v_s.reshape(bs, SAMPLE_ROWS, tok_width)
            q_rows = q_s.reshape(bs, SAMPLE_ROWS, n_heads, HEAD_DIM).transpose(0, 2, 1, 3)
            q_rows = q_rows.reshape(bs, n_heads * SAMPLE_ROWS, HEAD_DIM)
            pad_rows = ((0, 0), (0, page - SAMPLE_ROWS), (0, 0))
            bias_rows = jnp.broadcast_to(jnp.repeat(bias, SAMPLE_ROWS)[:, None], (n_heads * SAMPLE_ROWS, SB_BLOCK))
            o_rows = _sb_decode(q_rows, jnp.pad(k_s3, pad_rows), jnp.pad(v_s3, pad_rows), bias_rows,
                                cache_k, cache_v, page_flat, li, pages_per_step=4)
            o_s = o_rows.reshape(bs, n_heads, SAMPLE_ROWS, HEAD_DIM).transpose(0, 2, 1, 3)
            o_s = o_s.reshape(bs, SAMPLE_ROWS, tok_width)
            xs3 = _mixer_out(xs3, g_mix, w_out, qm_s.reshape(bs, SAMPLE_ROWS, MEM_WIDTH), 0, mk_s, mv_s,
                             tm=SAMPLE_ROWS, tok=o_s)
            ks.append(k_s3[:, :ts].reshape(bs, ts, n_heads, HEAD_DIM))
            vs.append(v_s3[:, :ts].reshape(bs, ts, n_heads, HEAD_DIM))
        xp = xp3.reshape(b * t, d)
        xs = xs3.reshape(ms, d)

        w_gu, w_d = ffn2_w_gate_up[i].astype(BF16), ffn2_w_down[i].astype(BF16)
        xp = _ffn_half(xp, norm_ffn2[i], w_gu, w_d, tm=tm_ffn, tf=tf_ffn)
        xs = _ffn_half(xs, norm_ffn2[i], w_gu, w_d, tm=ms, tf=tf_ffn)

    y_p = xp.reshape(b, t, d)
    y_s = xs.reshape(bs, SAMPLE_ROWS, d)[:, :ts]
    return (y_p, y_s, jnp.stack(pool_p), jnp.stack(kp), jnp.stack(vp), jnp.stack(mkp), jnp.stack(mvp),
            jnp.stack(pool_s), jnp.stack(ks), jnp.stack(vs))
```

```python
import functools

import jax
import jax.numpy as jnp
from jax import lax
from jax.experimental import pallas as pl
from jax.experimental.pallas import tpu as pltpu

F32 = jnp.float32
BF16 = jnp.bfloat16

HEAD_DIM = 128
MEM_HEADS = 4
MEM_WIDTH = MEM_HEADS * HEAD_DIM
POOL_WINDOWS = (2, 4, 8, 16)
POOL_BUF = max(POOL_WINDOWS) - 1
HALO = 16
N_MIXERS = 2
SB_BLOCK = 128
SB_TILE = 256
RMS_EPS = 1e-6
SAMPLE_ROWS = 8
LOG2E = 1.4426950408889634
MIB = 1024 * 1024
VMEM_CAP = 60 * MIB

_NT = (((1,), (1,)), ((), ()))


def _params(semantics, vmem_bytes):
    return pltpu.CompilerParams(dimension_semantics=semantics,
                                vmem_limit_bytes=int(min(vmem_bytes, VMEM_CAP)))


def _rms(x, g):
    ms = jnp.mean(x * x, axis=-1, keepdims=True)
    return x * lax.rsqrt(ms + RMS_EPS) * g


def _suffix_matrix(n):
    r = lax.broadcasted_iota(jnp.int32, (n, n), 0)
    c = lax.broadcasted_iota(jnp.int32, (n, n), 1)
    return jnp.where(r > c, 1.0, 0.0).astype(BF16)


def _sb_weights(z, vis, umat, carry):
    e = jnp.exp2(jnp.abs(z) * (-LOG2E))
    sp = jnp.maximum(z, 0.0) + jnp.log(1.0 + e)
    if vis is not None:
        sp = jnp.where(vis, sp, 0.0)
    later = jnp.dot(sp.astype(BF16), umat, preferred_element_type=F32)
    w = jnp.exp(z - sp - later - carry)
    if vis is not None:
        w = jnp.where(vis, w, 0.0)
    return w, carry + jnp.sum(sp, axis=-1, keepdims=True)


def _ffn_kernel(x_ref, g_ref, wg_ref, wu_ref, wd_ref, o_ref, h_ref, acc_ref, *, acc_cols):
    j = pl.program_id(1)

    @pl.when(j == 0)
    def _():
        h_ref[...] = _rms(x_ref[...], g_ref[0:1, :]).astype(BF16)
        acc_ref[...] = jnp.zeros_like(acc_ref)

    h = h_ref[...]
    gate = jnp.dot(h, wg_ref[...], preferred_element_type=F32)
    up = jnp.dot(h, wu_ref[...], preferred_element_type=F32)
    a = (gate * jax.nn.sigmoid(gate) * up).astype(BF16)
    for c0 in range(0, acc_ref.shape[1], acc_cols):
        cols = slice(c0, c0 + acc_cols)
        acc_ref[:, cols] += jnp.dot(a, wd_ref[:, cols], preferred_element_type=F32)

    @pl.when(j == pl.num_programs(1) - 1)
    def _():
        o_ref[...] = x_ref[...] + 0.5 * _rms(acc_ref[...], g_ref[1:2, :])


def _ffn_half(x, g, w_gu, w_d, layer, *, tm, tf):
    m, d = x.shape
    f = w_d.shape[1]
    nf = f // tf
    vmem = 2 * (2 * tm * d * 4) + tm * d * (2 + 4) + 2 * 3 * d * tf * 2 + 5 * tm * tf * 4 + 4 * MIB
    return pl.pallas_call(
        functools.partial(_ffn_kernel, acc_cols=min(d, 512)),
        grid=(m // tm, nf),
        in_specs=[pl.BlockSpec((tm, d), lambda i, j: (i, 0)),
                  pl.BlockSpec((2, d), lambda i, j: (0, 0)),
                  pl.BlockSpec((None, d, tf), lambda i, j: (layer, 0, j)),
                  pl.BlockSpec((None, d, tf), lambda i, j: (layer, 0, j + nf)),
                  pl.BlockSpec((None, tf, d), lambda i, j: (layer, j, 0))],
        out_specs=pl.BlockSpec((tm, d), lambda i, j: (i, 0)),
        out_shape=jax.ShapeDtypeStruct((m, d), F32),
        scratch_shapes=[pltpu.VMEM((tm, d), BF16), pltpu.VMEM((tm, d), F32)],
        compiler_params=_params(("parallel", "arbitrary"), vmem),
        name="ffn_half",
    )(x, g, w_gu, w_gu, w_d)


def _norm_proj_kernel(x_ref, g_ref, w_ref, o_ref, h_ref):
    @pl.when(pl.program_id(1) == 0)
    def _():
        h_ref[...] = _rms(x_ref[...], g_ref[...]).astype(BF16)

    o_ref[...] = jnp.dot(h_ref[...], w_ref[...], preferred_element_type=F32).astype(o_ref.dtype)


def _norm_proj(x, g, w, layer, col0, ncols, out_dtype, *, tm, tn):
    m, d = x.shape
    assert col0 % tn == 0 and ncols % tn == 0 and m % tm == 0
    cb = col0 // tn
    vmem = 2 * tm * d * 4 + tm * d * 2 + 2 * d * tn * 2 + 3 * tm * tn * 4 + 4 * MIB
    return pl.pallas_call(
        _norm_proj_kernel,
        grid=(m // tm, ncols // tn),
        in_specs=[pl.BlockSpec((tm, d), lambda i, j: (i, 0)),
                  pl.BlockSpec((1, d), lambda i, j: (0, 0)),
                  pl.BlockSpec((None, d, tn), lambda i, j: (layer, 0, j + cb))],
        out_specs=pl.BlockSpec((tm, tn), lambda i, j: (i, j)),
        out_shape=jax.ShapeDtypeStruct((m, ncols), out_dtype),
        scratch_shapes=[pltpu.VMEM((tm, d), BF16)],
        compiler_params=_params(("parallel", "arbitrary"), vmem),
        name="norm_proj",
    )(x, g, w)


def _sb_prompt_kernel(bias_ref, q_ref, k_ref, v_ref, o_ref, kb_ref, vb_ref, u_ref):
    tile = SB_TILE
    kb_ref[...] = k_ref[0].astype(BF16)
    vb_ref[...] = v_ref[0].astype(BF16)
    u_ref[...] = _suffix_matrix(tile)
    bias = bias_ref[pl.program_id(1)]
    scale = HEAD_DIM ** -0.5
    r = lax.broadcasted_iota(jnp.int32, (tile, tile), 0)
    c = lax.broadcasted_iota(jnp.int32, (tile, tile), 1)
    diag_vis = c < r
    for i in range(q_ref.shape[1] // tile):
        q = q_ref[0, i * tile:(i + 1) * tile, :]
        acc = jnp.zeros((tile, HEAD_DIM), F32)
        carry = jnp.zeros((tile, 1), F32)
        for j in range(i, -1, -1):
            rows = slice(j * tile, (j + 1) * tile)
            z = lax.dot_general(q, kb_ref[rows, :], _NT, preferred_element_type=F32) * scale + bias
            w, carry = _sb_weights(z, diag_vis if j == i else None, u_ref[...], carry)
            acc = acc + jnp.dot(w.astype(BF16), vb_ref[rows, :], preferred_element_type=F32)
        o_ref[0, i * tile:(i + 1) * tile, :] = acc.astype(o_ref.dtype)


def _sb_prompt(q, k, v, bias):
    b, t, w = q.shape
    nh = w // HEAD_DIM
    assert t % SB_TILE == 0
    head_spec = pl.BlockSpec((1, t, HEAD_DIM), lambda bb, h: (bb, 0, h))
    return pl.pallas_call(
        _sb_prompt_kernel,
        grid=(b, nh),
        in_specs=[pl.BlockSpec(memory_space=pltpu.SMEM), head_spec, head_spec, head_spec],
        out_specs=head_spec,
        out_shape=jax.ShapeDtypeStruct((b, t, w), BF16),
        scratch_shapes=[pltpu.VMEM((t, HEAD_DIM), BF16), pltpu.VMEM((t, HEAD_DIM), BF16),
                        pltpu.VMEM((SB_TILE, SB_TILE), BF16)],
        compiler_params=_params(("parallel", "parallel"), 32 * MIB),
        name="sb_prompt",
    )(bias, q, k, v)


def _sb_decode_kernel(pt_ref, q_ref, knew_ref, vnew_ref, bias_ref, *refs, pages_per_step, n_heads):
    del pt_ref
    k_refs = refs[:pages_per_step]
    v_refs = refs[pages_per_step:2 * pages_per_step]
    o_ref, acc_ref, car_ref = refs[2 * pages_per_step:]
    p = pl.program_id(1)
    rows = n_heads * SAMPLE_ROWS
    scale = HEAD_DIM ** -0.5
    umat = _suffix_matrix(SB_BLOCK)
    bias = bias_ref[...]
    q = q_ref[0]
    qh = [q[h * SAMPLE_ROWS:(h + 1) * SAMPLE_ROWS].astype(BF16) for h in range(n_heads)]

    def block(keys, values, vis):
        zs = [lax.dot_general(qh[h], keys(h), _NT, preferred_element_type=F32) for h in range(n_heads)]
        z = jnp.concatenate(zs, axis=0) * scale + bias
        w, carry = _sb_weights(z, vis, umat, car_ref[...])
        outs = [jnp.dot(w[h * SAMPLE_ROWS:(h + 1) * SAMPLE_ROWS].astype(BF16), values(h),
                        preferred_element_type=F32) for h in range(n_heads)]
        acc_ref[...] += jnp.concatenate(outs, axis=0)
        car_ref[...] = carry

    @pl.when(p == 0)
    def _():
        acc_ref[...] = jnp.zeros_like(acc_ref)
        car_ref[...] = jnp.zeros_like(car_ref)
        t = lax.broadcasted_iota(jnp.int32, (rows, SB_BLOCK), 0) % SAMPLE_ROWS
        s = lax.broadcasted_iota(jnp.int32, (rows, SB_BLOCK), 1)
        zeros = jnp.zeros((SB_BLOCK - SAMPLE_ROWS, HEAD_DIM), F32)

        def new_block(ref, h):
            head = ref[0, :, h * HEAD_DIM:(h + 1) * HEAD_DIM]
            return jnp.concatenate([head, zeros], axis=0).astype(BF16)

        block(functools.partial(new_block, knew_ref), functools.partial(new_block, vnew_ref), s < t)

    for c in range(pages_per_step):
        kb = pltpu.einshape("phd->hpd", k_refs[c][...].astype(BF16))
        vb = pltpu.einshape("phd->hpd", v_refs[c][...].astype(BF16))
        block(lambda h, kb=kb: kb[h], lambda h, vb=vb: vb[h], None)

    @pl.when(p == pl.num_programs(1) - 1)
    def _():
        o_ref[0] = acc_ref[...]


def _sb_decode(q_rows, k_new, v_new, bias_rows, cache_k, cache_v, page_flat, layer, *, pages_per_step):
    bsz, rows, _ = q_rows.shape
    n_heads = rows // SAMPLE_ROWS
    width = n_heads * HEAD_DIM
    page = cache_k.shape[2]
    n_pages = page_flat.shape[0] // bsz
    assert page == SB_BLOCK and n_pages % pages_per_step == 0
    steps = n_pages // pages_per_step
    sublane_pad = -(-n_heads // 8) * 8

    def page_map(c):
        return lambda b, p, pt: (layer, pt[b * n_pages + n_pages - 1 - (p * pages_per_step + c)], 0, 0, 0)

    def page_specs():
        return [pl.BlockSpec((None, None, page, n_heads, HEAD_DIM), page_map(c)) for c in range(pages_per_step)]

    seq_map = lambda b, p, pt: (b, 0, 0)
    grid_spec = pltpu.PrefetchScalarGridSpec(
        num_scalar_prefetch=1,
        grid=(bsz, steps),
        in_specs=[pl.BlockSpec((1, rows, HEAD_DIM), seq_map),
                  pl.BlockSpec((1, SAMPLE_ROWS, width), seq_map),
                  pl.BlockSpec((1, SAMPLE_ROWS, width), seq_map),
                  pl.BlockSpec((rows, 1), lambda b, p, pt: (0, 0))] + page_specs() + page_specs(),
        out_specs=pl.BlockSpec((1, rows, HEAD_DIM), seq_map),
        scratch_shapes=[pltpu.VMEM((rows, HEAD_DIM), F32), pltpu.VMEM((rows, 1), F32)],
    )
    vmem = 2 * 2 * pages_per_step * page * sublane_pad * HEAD_DIM * 4 + 12 * MIB
    return pl.pallas_call(
        functools.partial(_sb_decode_kernel, pages_per_step=pages_per_step, n_heads=n_heads),
        grid_spec=grid_spec,
        out_shape=jax.ShapeDtypeStruct((bsz, rows, HEAD_DIM), F32),
        compiler_params=_params(("parallel", "arbitrary"), vmem),
        name="sb_decode",
    )(page_flat, q_rows, k_new, v_new, bias_rows, *([cache_k] * pages_per_step), *([cache_v] * pages_per_step))


def _mixer_kernel(*refs, pool, n_prefix, halo_valid_from, mem_by_head):
    if pool:
        (u_ref, halo_ref, qm_ref, mk_ref, mv_ref, x_ref, g_ref, wo_ref, gm_ref, sc_ref,
         o_ref, cat_ref, ubuf_ref) = refs
    else:
        tok_ref, qm_ref, mk_ref, mv_ref, x_ref, g_ref, wo_ref, o_ref, cat_ref = refs
    t = pl.program_id(1)
    tm = x_ref.shape[1]
    tok_width = cat_ref.shape[1] - MEM_WIDTH

    if pool:
        group = tok_width // len(POOL_WINDOWS)
        halo = halo_ref[0]
        ubuf_ref[0:HALO, :] = jnp.where(t >= halo_valid_from, halo, 0.0)
        ubuf_ref[HALO:HALO + tm, :] = u_ref[0]
        pos = t * tm + lax.broadcasted_iota(jnp.int32, (tm, 1), 0) + (n_prefix + 1)
        for gi, win in enumerate(POOL_WINDOWS):
            cols = slice(gi * group, (gi + 1) * group)
            own = ubuf_ref[HALO:HALO + tm, cols]
            tot = own
            for back in range(1, win):
                tot = tot + ubuf_ref[HALO - back:HALO - back + tm, cols]
            count = jnp.minimum(pos, win).astype(F32)
            diff = tot / count - own
            mixed = jnp.dot(diff.astype(BF16), gm_ref[gi], preferred_element_type=F32) * sc_ref[:, cols]
            cat_ref[:, cols] = mixed.astype(BF16)
    else:
        cat_ref[:, 0:tok_width] = tok_ref[0].astype(BF16)

    scale = HEAD_DIM ** -0.5
    if mem_by_head:
        mk_heads = pltpu.einshape("mhd->hmd", mk_ref[0].astype(BF16))
        mv_heads = pltpu.einshape("mhd->hmd", mv_ref[0].astype(BF16))
    for h in range(MEM_HEADS):
        cols = slice(h * HEAD_DIM, (h + 1) * HEAD_DIM)
        qh = qm_ref[0, :, cols].astype(BF16)
        if mem_by_head:
            kh, vh = mk_heads[h], mv_heads[h]
        else:
            kh, vh = mk_ref[0, :, cols].astype(BF16), mv_ref[0, :, cols].astype(BF16)
        s = lax.dot_general(qh, kh, _NT, preferred_element_type=F32) * scale
        e = jnp.exp(s - jnp.max(s, axis=-1, keepdims=True))
        prob = e / jnp.sum(e, axis=-1, keepdims=True)
        oh = jnp.dot(prob.astype(BF16), vh, preferred_element_type=F32)
        cat_ref[:, tok_width + h * HEAD_DIM:tok_width + (h + 1) * HEAD_DIM] = oh.astype(BF16)

    y = jnp.dot(cat_ref[...], wo_ref[...], preferred_element_type=F32)
    o_ref[0] = x_ref[0] + _rms(y, g_ref[1:2, :])


def _mixer_out(x, g, w_out, layer, qm_src, qm_block, mk, mv, *, tm, tok=None, pool_src=None, halo_src=None,
               halo_valid_from=0, n_prefix=0, group_maps=None, pool_scale=None):
    b, t, d = x.shape
    tok_width = d - MEM_WIDTH
    mem_len = mk.shape[1]
    pool = tok is None
    mem_by_head = mk.ndim == 4
    qm_spec = pl.BlockSpec((1, tm, MEM_WIDTH), lambda bb, i: (bb, i, qm_block))
    if mem_by_head:
        mem_spec = pl.BlockSpec((1, mem_len, MEM_HEADS, HEAD_DIM), lambda bb, i: (bb, 0, 0, 0))
    else:
        mem_spec = pl.BlockSpec((1, mem_len, MEM_WIDTH), lambda bb, i: (bb, 0, 0))
    x_spec = pl.BlockSpec((1, tm, d), lambda bb, i: (bb, i, 0))
    g_spec = pl.BlockSpec((2, d), lambda bb, i: (0, 0))
    wo_spec = pl.BlockSpec((None, d, d), lambda bb, i: (layer, 0, 0))
    scratch = [pltpu.VMEM((tm, d), BF16)]
    if pool:
        halo_blocks = tm // HALO
        if halo_src is pool_src:
            halo_map = lambda bb, i: (bb, jnp.maximum(i * halo_blocks - 1, 0), 0)
        else:
            halo_map = lambda bb, i: (bb, 0, 0)
        _, ng, gw, _ = group_maps.shape
        in_specs = [pl.BlockSpec((1, tm, tok_width), lambda bb, i: (bb, i, 0)),
                    pl.BlockSpec((1, HALO, tok_width), halo_map),
                    qm_spec, mem_spec, mem_spec, x_spec, g_spec, wo_spec,
                    pl.BlockSpec((None, ng, gw, gw), lambda bb, i: (layer, 0, 0, 0)),
                    pl.BlockSpec((1, tok_width), lambda bb, i: (0, 0))]
        args = (pool_src, halo_src, qm_src, mk, mv, x, g, w_out, group_maps, pool_scale)
        scratch.append(pltpu.VMEM((HALO + tm, tok_width), F32))
    else:
        in_specs = [pl.BlockSpec((1, tm, tok_width), lambda bb, i: (bb, i, 0)),
                    qm_spec, mem_spec, mem_spec, x_spec, g_spec, wo_spec]
        args = (tok, qm_src, mk, mv, x, g, w_out)
    vmem = (2 * (tm * tok_width * 4 + tm * MEM_WIDTH * 4 + 2 * tm * d * 4 + 4 * mem_len * MEM_WIDTH * 4 + d * d * 2)
            + (HALO + tm) * tok_width * 4 + tm * d * 2 + 3 * tm * d * 4 + 6 * MIB)
    return pl.pallas_call(
        functools.partial(_mixer_kernel, pool=pool, n_prefix=n_prefix, halo_valid_from=halo_valid_from,
                          mem_by_head=mem_by_head),
        grid=(b, t // tm),
        in_specs=in_specs,
        out_specs=x_spec,
        out_shape=jax.ShapeDtypeStruct((b, t, d), F32),
        scratch_shapes=scratch,
        compiler_params=_params(("parallel", "arbitrary"), vmem),
        name="mixer_pool" if pool else "mixer_sb",
    )(*args)


def kernel(x_prompt, x_sample, state_pool, cache_sb_k, cache_sb_v, cache_mem_k, cache_mem_v, page_table, mem_prompt, norm_ffn1, ffn1_w_gate_up, ffn1_w_down, norm_mix, norm_mem, mem_w_kv, pool_w_in, pool_group_maps, pool_scale, pool_w_out, sb_w_in, sb_logit_bias, sb_w_out, norm_ffn2, ffn2_w_gate_up, ffn2_w_down):
    b, t, d = x_prompt.shape
    bs, ts, _ = x_sample.shape
    depth = norm_ffn1.shape[0]
    mem_len = mem_prompt.shape[1]
    tok_width = d - MEM_WIDTH
    n_heads = tok_width // HEAD_DIM
    assert ts <= SAMPLE_ROWS

    tm_ffn, tf_ffn = 512, 512
    tm_proj, tn_proj = 1024, 512
    tm_mix = 256
    ms = bs * SAMPLE_ROWS

    w1_gu, w1_d = ffn1_w_gate_up.astype(BF16), ffn1_w_down.astype(BF16)
    w2_gu, w2_d = ffn2_w_gate_up.astype(BF16), ffn2_w_down.astype(BF16)
    w_kv = mem_w_kv.astype(BF16)
    wp_in, wp_out, gmaps = pool_w_in.astype(BF16), pool_w_out.astype(BF16), pool_group_maps.astype(BF16)
    ws_in, ws_out = sb_w_in.astype(BF16), sb_w_out.astype(BF16)

    xp = x_prompt.reshape(b * t, d)
    xs = jnp.pad(x_sample, ((0, 0), (0, SAMPLE_ROWS - ts), (0, 0))).reshape(ms, d)
    mem2 = mem_prompt.reshape(b * mem_len, d)
    page_flat = page_table.reshape(-1).astype(jnp.int32)

    pool_p, pool_s, kp, vp, ks, vs, mkp, mvp = [], [], [], [], [], [], [], []
    for i in range(depth):
        li = i // N_MIXERS
        xp = _ffn_half(xp, norm_ffn1[i], w1_gu, w1_d, i, tm=tm_ffn, tf=tf_ffn)
        xs = _ffn_half(xs, norm_ffn1[i], w1_gu, w1_d, i, tm=ms, tf=tf_ffn)

        g_mem = norm_mem[i][None]
        mk_p = _norm_proj(mem2, g_mem, w_kv, i, 0, MEM_WIDTH, F32, tm=tm_proj, tn=tn_proj)
        mv_p = _norm_proj(mem2, g_mem, w_kv, i, MEM_WIDTH, MEM_WIDTH, F32, tm=tm_proj, tn=tn_proj)
        mk_p = mk_p.reshape(b, mem_len, MEM_WIDTH)
        mv_p = mv_p.reshape(b, mem_len, MEM_WIDTH)
        mkp.append(mk_p.reshape(b, mem_len, MEM_HEADS, HEAD_DIM))
        mvp.append(mv_p.reshape(b, mem_len, MEM_HEADS, HEAD_DIM))
        mk_s, mv_s = cache_mem_k[i], cache_mem_v[i]

        g_mix = norm_mix[i]
        g_in = g_mix[0:1]
        xp3 = xp.reshape(b, t, d)
        xs3 = xs.reshape(bs, SAMPLE_ROWS, d)
        if i % N_MIXERS == 0:
            scale = pool_scale[li][None]
            proj_p = _norm_proj(xp, g_in, wp_in, li, 0, d, F32, tm=tm_proj, tn=tn_proj).reshape(b, t, d)
            proj_s = _norm_proj(xs, g_in, wp_in, li, 0, d, F32, tm=ms, tn=tn_proj).reshape(bs, SAMPLE_ROWS, d)
            qm_block = tok_width // MEM_WIDTH
            xp3 = _mixer_out(xp3, g_mix, wp_out, li, proj_p, qm_block, mk_p, mv_p, tm=tm_mix, pool_src=proj_p,
                             halo_src=proj_p, halo_valid_from=1, n_prefix=0, group_maps=gmaps, pool_scale=scale)
            prefix = state_pool[li]
            n_prefix = prefix.shape[1]
            halo_s = jnp.pad(prefix, ((0, 0), (HALO - n_prefix, 0), (0, 0)))
            xs3 = _mixer_out(xs3, g_mix, wp_out, li, proj_s, qm_block, mk_s, mv_s, tm=SAMPLE_ROWS, pool_src=proj_s,
                             halo_src=halo_s, halo_valid_from=0, n_prefix=n_prefix, group_maps=gmaps,
                             pool_scale=scale)
            pool_p.append(proj_p[:, t - POOL_BUF:, :tok_width])
            u_ext = jnp.concatenate([prefix, proj_s[:, :ts, :tok_width]], axis=1)
            pool_s.append(u_ext[:, u_ext.shape[1] - POOL_BUF:])
        else:
            bias = sb_logit_bias[li].astype(F32)
            q_p = _norm_proj(xp, g_in, ws_in, li, 0, tok_width, BF16, tm=tm_proj, tn=tn_proj)
            k_p = _norm_proj(xp, g_in, ws_in, li, tok_width, tok_width, F32, tm=tm_proj, tn=tn_proj)
            v_p = _norm_proj(xp, g_in, ws_in, li, 2 * tok_width, tok_width, F32, tm=tm_proj, tn=tn_proj)
            qm_p = _norm_proj(xp, g_in, ws_in, li, 3 * tok_width, MEM_WIDTH, F32, tm=tm_proj, tn=tn_proj)
            k_p3 = k_p.reshape(b, t, tok_width)
            v_p3 = v_p.reshape(b, t, tok_width)
            o_p = _sb_prompt(q_p.reshape(b, t, tok_width), k_p3, v_p3, bias)
            xp3 = _mixer_out(xp3, g_mix, ws_out, li, qm_p.reshape(b, t, MEM_WIDTH), 0, mk_p, mv_p, tm=tm_mix,
                             tok=o_p)
            kp.append(k_p3.reshape(b, t, n_heads, HEAD_DIM))
            vp.append(v_p3.reshape(b, t, n_heads, HEAD_DIM))

            q_s = _norm_proj(xs, g_in, ws_in, li, 0, tok_width, F32, tm=ms, tn=tn_proj)
            k_s = _norm_proj(xs, g_in, ws_in, li, tok_width, tok_width, F32, tm=ms, tn=tn_proj)
            v_s = _norm_proj(xs, g_in, ws_in, li, 2 * tok_width, tok_width, F32, tm=ms, tn=tn_proj)
            qm_s = _norm_proj(xs, g_in, ws_in, li, 3 * tok_width, MEM_WIDTH, F32, tm=ms, tn=tn_proj)
            k_s3 = k_s.reshape(bs, SAMPLE_ROWS, tok_width)
            v_s3 = v_s.reshape(bs, SAMPLE_ROWS, tok_width)
            q_rows = q_s.reshape(bs, SAMPLE_ROWS, n_heads, HEAD_DIM).transpose(0, 2, 1, 3)
            q_rows = q_rows.reshape(bs, n_heads * SAMPLE_ROWS, HEAD_DIM)
            bias_rows = jnp.repeat(bias, SAMPLE_ROWS)[:, None]
            o_rows = _sb_decode(q_rows, k_s3, v_s3, bias_rows, cache_sb_k, cache_sb_v, page_flat, li,
                                pages_per_step=8)
            o_s = o_rows.reshape(bs, n_heads, SAMPLE_ROWS, HEAD_DIM).transpose(0, 2, 1, 3)
            o_s = o_s.reshape(bs, SAMPLE_ROWS, tok_width)
            xs3 = _mixer_out(xs3, g_mix, ws_out, li, qm_s.reshape(bs, SAMPLE_ROWS, MEM_WIDTH), 0, mk_s, mv_s,
                             tm=SAMPLE_ROWS, tok=o_s)
            ks.append(k_s3[:, :ts].reshape(bs, ts, n_heads, HEAD_DIM))
            vs.append(v_s3[:, :ts].reshape(bs, ts, n_heads, HEAD_DIM))
        xp = xp3.reshape(b * t, d)
        xs = xs3.reshape(ms, d)

        xp = _ffn_half(xp, norm_ffn2[i], w2_gu, w2_d, i, tm=tm_ffn, tf=tf_ffn)
        xs = _ffn_half(xs, norm_ffn2[i], w2_gu, w2_d, i, tm=ms, tf=tf_ffn)

    y_p = xp.reshape(b, t, d)
    y_s = xs.reshape(bs, SAMPLE_ROWS, d)[:, :ts]
    return (y_p, y_s, jnp.stack(pool_p), jnp.stack(kp), jnp.stack(vp), jnp.stack(mkp), jnp.stack(mvp),
            jnp.stack(pool_s), jnp.stack(ks), jnp.stack(vs))
```

```python
import functools

import jax
import jax.numpy as jnp
from jax import lax
from jax.experimental import pallas as pl
from jax.experimental.pallas import tpu as pltpu

F32 = jnp.float32
BF16 = jnp.bfloat16

HEAD_DIM = 128
MEM_HEADS = 4
MEM_WIDTH = MEM_HEADS * HEAD_DIM
POOL_WINDOWS = (2, 4, 8, 16)
POOL_BUF = max(POOL_WINDOWS) - 1
HALO = 16
N_MIXERS = 2
SB_BLOCK = 128
SB_TILE = 256
RMS_EPS = 1e-6
SAMPLE_ROWS = 8
LOG2E = 1.4426950408889634
MIB = 1024 * 1024
VMEM_CAP = 60 * MIB

_NT = (((1,), (1,)), ((), ()))


def _params(semantics, vmem_bytes):
    return pltpu.CompilerParams(dimension_semantics=semantics,
                                vmem_limit_bytes=int(min(vmem_bytes, VMEM_CAP)))


def _rms(x, g):
    ms = jnp.mean(x * x, axis=-1, keepdims=True)
    return x * lax.rsqrt(ms + RMS_EPS) * g


def _suffix_matrix(n):
    r = lax.broadcasted_iota(jnp.int32, (n, n), 0)
    c = lax.broadcasted_iota(jnp.int32, (n, n), 1)
    return jnp.where(r > c, 1.0, 0.0).astype(BF16)


def _sb_weights(z, vis, umat, carry):
    e = jnp.exp2(jnp.abs(z) * (-LOG2E))
    sp = jnp.maximum(z, 0.0) + jnp.log(1.0 + e)
    if vis is not None:
        sp = jnp.where(vis, sp, 0.0)
    later = jnp.dot(sp.astype(BF16), umat, preferred_element_type=F32)
    w = jnp.exp(z - sp - later - carry)
    if vis is not None:
        w = jnp.where(vis, w, 0.0)
    return w, carry + jnp.sum(sp, axis=-1, keepdims=True)


def _ffn_kernel(x_ref, g_ref, wg_ref, wu_ref, wd_ref, o_ref, h_ref, acc_ref, *, acc_cols):
    j = pl.program_id(1)

    @pl.when(j == 0)
    def _():
        h_ref[...] = _rms(x_ref[...], g_ref[0:1, :]).astype(BF16)
        acc_ref[...] = jnp.zeros_like(acc_ref)

    h = h_ref[...]
    gate = jnp.dot(h, wg_ref[...], preferred_element_type=F32)
    up = jnp.dot(h, wu_ref[...], preferred_element_type=F32)
    a = (gate * jax.nn.sigmoid(gate) * up).astype(BF16)
    for c0 in range(0, acc_ref.shape[1], acc_cols):
        cols = slice(c0, c0 + acc_cols)
        acc_ref[:, cols] += jnp.dot(a, wd_ref[:, cols], preferred_element_type=F32)

    @pl.when(j == pl.num_programs(1) - 1)
    def _():
        o_ref[...] = x_ref[...] + 0.5 * _rms(acc_ref[...], g_ref[1:2, :])


def _ffn_half(x, g, w_gu, w_d, layer, *, tm, tf):
    m, d = x.shape
    f = w_d.shape[1]
    nf = f // tf
    vmem = 2 * (2 * tm * d * 4) + tm * d * (2 + 4) + 2 * 3 * d * tf * 2 + 5 * tm * tf * 4 + 4 * MIB
    return pl.pallas_call(
        functools.partial(_ffn_kernel, acc_cols=min(d, 512)),
        grid=(m // tm, nf),
        in_specs=[pl.BlockSpec((tm, d), lambda i, j: (i, 0)),
                  pl.BlockSpec((2, d), lambda i, j: (0, 0)),
                  pl.BlockSpec((None, d, tf), lambda i, j: (layer, 0, j)),
                  pl.BlockSpec((None, d, tf), lambda i, j: (layer, 0, j + nf)),
                  pl.BlockSpec((None, tf, d), lambda i, j: (layer, j, 0))],
        out_specs=pl.BlockSpec((tm, d), lambda i, j: (i, 0)),
        out_shape=jax.ShapeDtypeStruct((m, d), F32),
        scratch_shapes=[pltpu.VMEM((tm, d), BF16), pltpu.VMEM((tm, d), F32)],
        compiler_params=_params(("parallel", "arbitrary"), vmem),
        name="ffn_half",
    )(x, g, w_gu, w_gu, w_d)


def _norm_proj_kernel(x_ref, g_ref, w_ref, o_ref, h_ref, *, by_head):
    @pl.when(pl.program_id(1) == 0)
    def _():
        h_ref[...] = _rms(x_ref[...], g_ref[...]).astype(BF16)

    res = jnp.dot(h_ref[...], w_ref[...], preferred_element_type=F32).astype(o_ref.dtype)
    if by_head:
        for hh in range(o_ref.shape[1]):
            o_ref[0, hh] = res[:, hh * HEAD_DIM:(hh + 1) * HEAD_DIM]
    else:
        o_ref[...] = res


def _norm_proj(x, g, w, layer, col0, ncols, out_dtype, *, tm, tn, seq_len=None):
    m, d = x.shape
    assert col0 % tn == 0 and ncols % tn == 0 and m % tm == 0
    cb = col0 // tn
    vmem = 2 * tm * d * 4 + tm * d * 2 + 2 * d * tn * 2 + 3 * tm * tn * 4 + 4 * MIB
    if seq_len is None:
        out_spec = pl.BlockSpec((tm, tn), lambda i, j: (i, j))
        out_shape = jax.ShapeDtypeStruct((m, ncols), out_dtype)
    else:
        assert seq_len % tm == 0 and tn % HEAD_DIM == 0
        tiles = seq_len // tm
        out_spec = pl.BlockSpec((1, tn // HEAD_DIM, tm, HEAD_DIM), lambda i, j: (i // tiles, j, i % tiles, 0))
        out_shape = jax.ShapeDtypeStruct((m // seq_len, ncols // HEAD_DIM, seq_len, HEAD_DIM), out_dtype)
    return pl.pallas_call(
        functools.partial(_norm_proj_kernel, by_head=seq_len is not None),
        grid=(m // tm, ncols // tn),
        in_specs=[pl.BlockSpec((tm, d), lambda i, j: (i, 0)),
                  pl.BlockSpec((1, d), lambda i, j: (0, 0)),
                  pl.BlockSpec((None, d, tn), lambda i, j: (layer, 0, j + cb))],
        out_specs=out_spec,
        out_shape=out_shape,
        scratch_shapes=[pltpu.VMEM((tm, d), BF16)],
        compiler_params=_params(("parallel", "arbitrary"), vmem),
        name="norm_proj",
    )(x, g, w)


def _sb_prompt_kernel(bias_ref, q_ref, k_ref, v_ref, o_ref, kb_ref, vb_ref, u_ref):
    tile = SB_TILE
    kb_ref[...] = k_ref[0, 0].astype(BF16)
    vb_ref[...] = v_ref[0, 0].astype(BF16)
    u_ref[...] = _suffix_matrix(tile)
    bias = bias_ref[pl.program_id(1)]
    scale = HEAD_DIM ** -0.5
    r = lax.broadcasted_iota(jnp.int32, (tile, tile), 0)
    c = lax.broadcasted_iota(jnp.int32, (tile, tile), 1)
    diag_vis = c < r
    for i in range(q_ref.shape[2] // tile):
        q = q_ref[0, 0, i * tile:(i + 1) * tile, :]
        acc = jnp.zeros((tile, HEAD_DIM), F32)
        carry = jnp.zeros((tile, 1), F32)
        for j in range(i, -1, -1):
            rows = slice(j * tile, (j + 1) * tile)
            z = lax.dot_general(q, kb_ref[rows, :], _NT, preferred_element_type=F32) * scale + bias
            w, carry = _sb_weights(z, diag_vis if j == i else None, u_ref[...], carry)
            acc = acc + jnp.dot(w.astype(BF16), vb_ref[rows, :], preferred_element_type=F32)
        o_ref[0, i * tile:(i + 1) * tile, :] = acc.astype(o_ref.dtype)


def _sb_prompt(q, k, v, bias):
    b, nh, t, _ = q.shape
    assert t % SB_TILE == 0
    head_spec = pl.BlockSpec((1, 1, t, HEAD_DIM), lambda bb, h: (bb, h, 0, 0))
    return pl.pallas_call(
        _sb_prompt_kernel,
        grid=(b, nh),
        in_specs=[pl.BlockSpec(memory_space=pltpu.SMEM), head_spec, head_spec, head_spec],
        out_specs=pl.BlockSpec((1, t, HEAD_DIM), lambda bb, h: (bb, 0, h)),
        out_shape=jax.ShapeDtypeStruct((b, t, nh * HEAD_DIM), BF16),
        scratch_shapes=[pltpu.VMEM((t, HEAD_DIM), BF16), pltpu.VMEM((t, HEAD_DIM), BF16),
                        pltpu.VMEM((SB_TILE, SB_TILE), BF16)],
        compiler_params=_params(("parallel", "parallel"), 32 * MIB),
        name="sb_prompt",
    )(bias, q, k, v)


def _sb_decode_kernel(pt_ref, q_ref, knew_ref, vnew_ref, bias_ref, *refs, pages_per_step, n_heads):
    del pt_ref
    k_refs = refs[:pages_per_step]
    v_refs = refs[pages_per_step:2 * pages_per_step]
    o_ref, acc_ref, car_ref = refs[2 * pages_per_step:]
    p = pl.program_id(1)
    rows = n_heads * SAMPLE_ROWS
    scale = HEAD_DIM ** -0.5
    umat = _suffix_matrix(SB_BLOCK)
    bias = bias_ref[...]
    q = q_ref[0]
    qh = [q[h * SAMPLE_ROWS:(h + 1) * SAMPLE_ROWS].astype(BF16) for h in range(n_heads)]

    def block(keys, values, vis):
        zs = [lax.dot_general(qh[h], keys(h), _NT, preferred_element_type=F32) for h in range(n_heads)]
        z = jnp.concatenate(zs, axis=0) * scale + bias
        w, carry = _sb_weights(z, vis, umat, car_ref[...])
        outs = [jnp.dot(w[h * SAMPLE_ROWS:(h + 1) * SAMPLE_ROWS].astype(BF16), values(h),
                        preferred_element_type=F32) for h in range(n_heads)]
        acc_ref[...] += jnp.concatenate(outs, axis=0)
        car_ref[...] = carry

    @pl.when(p == 0)
    def _():
        acc_ref[...] = jnp.zeros_like(acc_ref)
        car_ref[...] = jnp.zeros_like(car_ref)
        t = lax.broadcasted_iota(jnp.int32, (rows, SB_BLOCK), 0) % SAMPLE_ROWS
        s = lax.broadcasted_iota(jnp.int32, (rows, SB_BLOCK), 1)
        zeros = jnp.zeros((SB_BLOCK - SAMPLE_ROWS, HEAD_DIM), F32)

        def new_block(ref, h):
            head = ref[0, :, h * HEAD_DIM:(h + 1) * HEAD_DIM]
            return jnp.concatenate([head, zeros], axis=0).astype(BF16)

        block(functools.partial(new_block, knew_ref), functools.partial(new_block, vnew_ref), s < t)

    for c in range(pages_per_step):
        block(lambda h, r=k_refs[c]: r[h].astype(BF16), lambda h, r=v_refs[c]: r[h].astype(BF16), None)

    @pl.when(p == pl.num_programs(1) - 1)
    def _():
        o_ref[0] = acc_ref[...]


def _sb_decode(q_rows, k_new, v_new, bias_rows, cache_k, cache_v, page_flat, layer, *, pages_per_step):
    bsz, rows, _ = q_rows.shape
    n_heads = rows // SAMPLE_ROWS
    width = n_heads * HEAD_DIM
    page = cache_k.shape[3]
    n_pages = page_flat.shape[0] // bsz
    assert page == SB_BLOCK and n_pages % pages_per_step == 0
    steps = n_pages // pages_per_step

    def page_map(c):
        return lambda b, p, pt: (layer, pt[b * n_pages + n_pages - 1 - (p * pages_per_step + c)], 0, 0, 0)

    def page_specs():
        return [pl.BlockSpec((None, None, n_heads, page, HEAD_DIM), page_map(c)) for c in range(pages_per_step)]

    seq_map = lambda b, p, pt: (b, 0, 0)
    grid_spec = pltpu.PrefetchScalarGridSpec(
        num_scalar_prefetch=1,
        grid=(bsz, steps),
        in_specs=[pl.BlockSpec((1, rows, HEAD_DIM), seq_map),
                  pl.BlockSpec((1, SAMPLE_ROWS, width), seq_map),
                  pl.BlockSpec((1, SAMPLE_ROWS, width), seq_map),
                  pl.BlockSpec((rows, 1), lambda b, p, pt: (0, 0))] + page_specs() + page_specs(),
        out_specs=pl.BlockSpec((1, rows, HEAD_DIM), seq_map),
        scratch_shapes=[pltpu.VMEM((rows, HEAD_DIM), F32), pltpu.VMEM((rows, 1), F32)],
    )
    vmem = 2 * 2 * pages_per_step * page * width * 4 + 12 * MIB
    return pl.pallas_call(
        functools.partial(_sb_decode_kernel, pages_per_step=pages_per_step, n_heads=n_heads),
        grid_spec=grid_spec,
        out_shape=jax.ShapeDtypeStruct((bsz, rows, HEAD_DIM), F32),
        compiler_params=_params(("parallel", "arbitrary"), vmem),
        name="sb_decode",
    )(page_flat, q_rows, k_new, v_new, bias_rows, *([cache_k] * pages_per_step), *([cache_v] * pages_per_step))


def _mixer_kernel(*refs, pool, n_prefix, halo_valid_from, mem_by_head):
    if pool:
        (u_ref, halo_ref, qm_ref, mk_ref, mv_ref, x_ref, g_ref, wo_ref, gm_ref, sc_ref,
         o_ref, cat_ref, ubuf_ref) = refs
    else:
        tok_ref, qm_ref, mk_ref, mv_ref, x_ref, g_ref, wo_ref, o_ref, cat_ref = refs
    t = pl.program_id(1)
    tm = x_ref.shape[1]
    tok_width = cat_ref.shape[1] - MEM_WIDTH

    if pool:
        group = tok_width // len(POOL_WINDOWS)
        halo = halo_ref[0]
        ubuf_ref[0:HALO, :] = jnp.where(t >= halo_valid_from, halo, 0.0)
        ubuf_ref[HALO:HALO + tm, :] = u_ref[0]
        pos = t * tm + lax.broadcasted_iota(jnp.int32, (tm, 1), 0) + (n_prefix + 1)
        for gi, win in enumerate(POOL_WINDOWS):
            cols = slice(gi * group, (gi + 1) * group)
            own = ubuf_ref[HALO:HALO + tm, cols]
            tot = own
            for back in range(1, win):
                tot = tot + ubuf_ref[HALO - back:HALO - back + tm, cols]
            count = jnp.minimum(pos, win).astype(F32)
            diff = tot / count - own
            mixed = jnp.dot(diff.astype(BF16), gm_ref[gi], preferred_element_type=F32) * sc_ref[:, cols]
            cat_ref[:, cols] = mixed.astype(BF16)
    else:
        cat_ref[:, 0:tok_width] = tok_ref[0].astype(BF16)

    scale = HEAD_DIM ** -0.5
    if mem_by_head:
        mk_heads = pltpu.einshape("mhd->hmd", mk_ref[0].astype(BF16))
        mv_heads = pltpu.einshape("mhd->hmd", mv_ref[0].astype(BF16))
    for h in range(MEM_HEADS):
        cols = slice(h * HEAD_DIM, (h + 1) * HEAD_DIM)
        qh = qm_ref[0, :, cols].astype(BF16)
        if mem_by_head:
            kh, vh = mk_heads[h], mv_heads[h]
        else:
            kh, vh = mk_ref[0, :, cols].astype(BF16), mv_ref[0, :, cols].astype(BF16)
        s = lax.dot_general(qh, kh, _NT, preferred_element_type=F32) * scale
        e = jnp.exp(s - jnp.max(s, axis=-1, keepdims=True))
        prob = e / jnp.sum(e, axis=-1, keepdims=True)
        oh = jnp.dot(prob.astype(BF16), vh, preferred_element_type=F32)
        cat_ref[:, tok_width + h * HEAD_DIM:tok_width + (h + 1) * HEAD_DIM] = oh.astype(BF16)

    y = jnp.dot(cat_ref[...], wo_ref[...], preferred_element_type=F32)
    o_ref[0] = x_ref[0] + _rms(y, g_ref[1:2, :])


def _mixer_out(x, g, w_out, layer, qm_src, qm_block, mk, mv, *, tm, tok=None, pool_src=None, halo_src=None,
               halo_valid_from=0, n_prefix=0, group_maps=None, pool_scale=None):
    b, t, d = x.shape
    tok_width = d - MEM_WIDTH
    mem_len = mk.shape[1]
    pool = tok is None
    mem_by_head = mk.ndim == 4
    qm_spec = pl.BlockSpec((1, tm, MEM_WIDTH), lambda bb, i: (bb, i, qm_block))
    if mem_by_head:
        mem_spec = pl.BlockSpec((1, mem_len, MEM_HEADS, HEAD_DIM), lambda bb, i: (bb, 0, 0, 0))
    else:
        mem_spec = pl.BlockSpec((1, mem_len, MEM_WIDTH), lambda bb, i: (bb, 0, 0))
    x_spec = pl.BlockSpec((1, tm, d), lambda bb, i: (bb, i, 0))
    g_spec = pl.BlockSpec((2, d), lambda bb, i: (0, 0))
    wo_spec = pl.BlockSpec((None, d, d), lambda bb, i: (layer, 0, 0))
    scratch = [pltpu.VMEM((tm, d), BF16)]
    if pool:
        halo_blocks = tm // HALO
        if halo_src is pool_src:
            halo_map = lambda bb, i: (bb, jnp.maximum(i * halo_blocks - 1, 0), 0)
        else:
            halo_map = lambda bb, i: (bb, 0, 0)
        _, ng, gw, _ = group_maps.shape
        in_specs = [pl.BlockSpec((1, tm, tok_width), lambda bb, i: (bb, i, 0)),
                    pl.BlockSpec((1, HALO, tok_width), halo_map),
                    qm_spec, mem_spec, mem_spec, x_spec, g_spec, wo_spec,
                    pl.BlockSpec((None, ng, gw, gw), lambda bb, i: (layer, 0, 0, 0)),
                    pl.BlockSpec((1, tok_width), lambda bb, i: (0, 0))]
        args = (pool_src, halo_src, qm_src, mk, mv, x, g, w_out, group_maps, pool_scale)
        scratch.append(pltpu.VMEM((HALO + tm, tok_width), F32))
    else:
        in_specs = [pl.BlockSpec((1, tm, tok_width), lambda bb, i: (bb, i, 0)),
                    qm_spec, mem_spec, mem_spec, x_spec, g_spec, wo_spec]
        args = (tok, qm_src, mk, mv, x, g, w_out)
    vmem = (2 * (tm * tok_width * 4 + tm * MEM_WIDTH * 4 + 2 * tm * d * 4 + 4 * mem_len * MEM_WIDTH * 4 + d * d * 2)
            + (HALO + tm) * tok_width * 4 + tm * d * 2 + 3 * tm * d * 4 + 6 * MIB)
    return pl.pallas_call(
        functools.partial(_mixer_kernel, pool=pool, n_prefix=n_prefix, halo_valid_from=halo_valid_from,
                          mem_by_head=mem_by_head),
        grid=(b, t // tm),
        in_specs=in_specs,
        out_specs=x_spec,
        out_shape=jax.ShapeDtypeStruct((b, t, d), F32),
        scratch_shapes=scratch,
        compiler_params=_params(("parallel", "arbitrary"), vmem),
        name="mixer_pool" if pool else "mixer_sb",
    )(*args)


def kernel(x_prompt, x_sample, state_pool, cache_sb_k, cache_sb_v, cache_mem_k, cache_mem_v, page_table, mem_prompt, norm_ffn1, ffn1_w_gate_up, ffn1_w_down, norm_mix, norm_mem, mem_w_kv, pool_w_in, pool_group_maps, pool_scale, pool_w_out, sb_w_in, sb_logit_bias, sb_w_out, norm_ffn2, ffn2_w_gate_up, ffn2_w_down):
    b, t, d = x_prompt.shape
    bs, ts, _ = x_sample.shape
    depth = norm_ffn1.shape[0]
    mem_len = mem_prompt.shape[1]
    tok_width = d - MEM_WIDTH
    n_heads = tok_width // HEAD_DIM
    assert ts <= SAMPLE_ROWS

    tm_ffn, tf_ffn = 512, 512
    tm_proj, tn_proj = 1024, 512
    tm_mix = 256
    ms = bs * SAMPLE_ROWS

    w1_gu, w1_d = ffn1_w_gate_up.astype(BF16), ffn1_w_down.astype(BF16)
    w2_gu, w2_d = ffn2_w_gate_up.astype(BF16), ffn2_w_down.astype(BF16)
    w_kv = mem_w_kv.astype(BF16)
    wp_in, wp_out, gmaps = pool_w_in.astype(BF16), pool_w_out.astype(BF16), pool_group_maps.astype(BF16)
    ws_in, ws_out = sb_w_in.astype(BF16), sb_w_out.astype(BF16)

    xp = x_prompt.reshape(b * t, d)
    xs = jnp.pad(x_sample, ((0, 0), (0, SAMPLE_ROWS - ts), (0, 0))).reshape(ms, d)
    mem2 = mem_prompt.reshape(b * mem_len, d)
    page_flat = page_table.reshape(-1).astype(jnp.int32)
    cache_k = jnp.transpose(cache_sb_k, (0, 1, 3, 2, 4))
    cache_v = jnp.transpose(cache_sb_v, (0, 1, 3, 2, 4))

    pool_p, pool_s, kp, vp, ks, vs, mkp, mvp = [], [], [], [], [], [], [], []
    for i in range(depth):
        li = i // N_MIXERS
        xp = _ffn_half(xp, norm_ffn1[i], w1_gu, w1_d, i, tm=tm_ffn, tf=tf_ffn)
        xs = _ffn_half(xs, norm_ffn1[i], w1_gu, w1_d, i, tm=ms, tf=tf_ffn)

        g_mem = norm_mem[i][None]
        mk_p = _norm_proj(mem2, g_mem, w_kv, i, 0, MEM_WIDTH, F32, tm=tm_proj, tn=tn_proj)
        mv_p = _norm_proj(mem2, g_mem, w_kv, i, MEM_WIDTH, MEM_WIDTH, F32, tm=tm_proj, tn=tn_proj)
        mk_p = mk_p.reshape(b, mem_len, MEM_WIDTH)
        mv_p = mv_p.reshape(b, mem_len, MEM_WIDTH)
        mkp.append(mk_p.reshape(b, mem_len, MEM_HEADS, HEAD_DIM))
        mvp.append(mv_p.reshape(b, mem_len, MEM_HEADS, HEAD_DIM))
        mk_s, mv_s = cache_mem_k[i], cache_mem_v[i]

        g_mix = norm_mix[i]
        g_in = g_mix[0:1]
        xp3 = xp.reshape(b, t, d)
        xs3 = xs.reshape(bs, SAMPLE_ROWS, d)
        if i % N_MIXERS == 0:
            scale = pool_scale[li][None]
            proj_p = _norm_proj(xp, g_in, wp_in, li, 0, d, F32, tm=tm_proj, tn=tn_proj).reshape(b, t, d)
            proj_s = _norm_proj(xs, g_in, wp_in, li, 0, d, F32, tm=ms, tn=tn_proj).reshape(bs, SAMPLE_ROWS, d)
            qm_block = tok_width // MEM_WIDTH
            xp3 = _mixer_out(xp3, g_mix, wp_out, li, proj_p, qm_block, mk_p, mv_p, tm=tm_mix, pool_src=proj_p,
                             halo_src=proj_p, halo_valid_from=1, n_prefix=0, group_maps=gmaps, pool_scale=scale)
            prefix = state_pool[li]
            n_prefix = prefix.shape[1]
            halo_s = jnp.pad(prefix, ((0, 0), (HALO - n_prefix, 0), (0, 0)))
            xs3 = _mixer_out(xs3, g_mix, wp_out, li, proj_s, qm_block, mk_s, mv_s, tm=SAMPLE_ROWS, pool_src=proj_s,
                             halo_src=halo_s, halo_valid_from=0, n_prefix=n_prefix, group_maps=gmaps,
                             pool_scale=scale)
            pool_p.append(proj_p[:, t - POOL_BUF:, :tok_width])
            u_ext = jnp.concatenate([prefix, proj_s[:, :ts, :tok_width]], axis=1)
            pool_s.append(u_ext[:, u_ext.shape[1] - POOL_BUF:])
        else:
            bias = sb_logit_bias[li].astype(F32)
            q_p = _norm_proj(xp, g_in, ws_in, li, 0, tok_width, BF16, tm=tm_proj, tn=tn_proj, seq_len=t)
            k_p = _norm_proj(xp, g_in, ws_in, li, tok_width, tok_width, F32, tm=tm_proj, tn=tn_proj, seq_len=t)
            v_p = _norm_proj(xp, g_in, ws_in, li, 2 * tok_width, tok_width, F32, tm=tm_proj, tn=tn_proj, seq_len=t)
            qm_p = _norm_proj(xp, g_in, ws_in, li, 3 * tok_width, MEM_WIDTH, F32, tm=tm_proj, tn=tn_proj)
            o_p = _sb_prompt(q_p, k_p, v_p, bias)
            xp3 = _mixer_out(xp3, g_mix, ws_out, li, qm_p.reshape(b, t, MEM_WIDTH), 0, mk_p, mv_p, tm=tm_mix,
                             tok=o_p)
            kp.append(jnp.transpose(k_p, (0, 2, 1, 3)))
            vp.append(jnp.transpose(v_p, (0, 2, 1, 3)))

            q_s = _norm_proj(xs, g_in, ws_in, li, 0, tok_width, F32, tm=ms, tn=tn_proj)
            k_s = _norm_proj(xs, g_in, ws_in, li, tok_width, tok_width, F32, tm=ms, tn=tn_proj)
            v_s = _norm_proj(xs, g_in, ws_in, li, 2 * tok_width, tok_width, F32, tm=ms, tn=tn_proj)
            qm_s = _norm_proj(xs, g_in, ws_in, li, 3 * tok_width, MEM_WIDTH, F32, tm=ms, tn=tn_proj)
            k_s3 = k_s.reshape(bs, SAMPLE_ROWS, tok_width)
            v_s3 = v_s.reshape(bs, SAMPLE_ROWS, tok_width)
            q_rows = q_s.reshape(bs, SAMPLE_ROWS, n_heads, HEAD_DIM).transpose(0, 2, 1, 3)
            q_rows = q_rows.reshape(bs, n_heads * SAMPLE_ROWS, HEAD_DIM)
            bias_rows = jnp.repeat(bias, SAMPLE_ROWS)[:, None]
            o_rows = _sb_decode(q_rows, k_s3, v_s3, bias_rows, cache_k, cache_v, page_flat, li,
                                pages_per_step=8)
            o_s = o_rows.reshape(bs, n_heads, SAMPLE_ROWS, HEAD_DIM).transpose(0, 2, 1, 3)
            o_s = o_s.reshape(bs, SAMPLE_ROWS, tok_width)
            xs3 = _mixer_out(xs3, g_mix, ws_out, li, qm_s.reshape(bs, SAMPLE_ROWS, MEM_WIDTH), 0, mk_s, mv_s,
                             tm=SAMPLE_ROWS, tok=o_s)
            ks.append(k_s3[:, :ts].reshape(bs, ts, n_heads, HEAD_DIM))
            vs.append(v_s3[:, :ts].reshape(bs, ts, n_heads, HEAD_DIM))
        xp = xp3.reshape(b * t, d)
        xs = xs3.reshape(ms, d)

        xp = _ffn_half(xp, norm_ffn2[i], w2_gu, w2_d, i, tm=tm_ffn, tf=tf_ffn)
        xs = _ffn_half(xs, norm_ffn2[i], w2_gu, w2_d, i, tm=ms, tf=tf_ffn)

    y_p = xp.reshape(b, t, d)
    y_s = xs.reshape(bs, SAMPLE_ROWS, d)[:, :ts]
    return (y_p, y_s, jnp.stack(pool_p), jnp.stack(kp), jnp.stack(vp), jnp.stack(mkp), jnp.stack(mvp),
            jnp.stack(pool_s), jnp.stack(ks), jnp.stack(vs))
```

```python
import functools

import jax
import jax.numpy as jnp
from jax import lax
from jax.experimental import pallas as pl
from jax.experimental.pallas import tpu as pltpu

F32 = jnp.float32
BF16 = jnp.bfloat16

HEAD_DIM = 128
MEM_HEADS = 4
MEM_WIDTH = MEM_HEADS * HEAD_DIM
POOL_WINDOWS = (2, 4, 8, 16)
POOL_BUF = max(POOL_WINDOWS) - 1
HALO = 16
N_MIXERS = 2
SB_BLOCK = 128
SB_QTILE = 512
SB_KCHUNK = 256
RMS_EPS = 1e-6
SAMPLE_ROWS = 8
LOG2E = 1.4426950408889634
MIB = 1024 * 1024
VMEM_CAP = 62 * MIB

_NT = (((1,), (1,)), ((), ()))


def _params(semantics, vmem_bytes):
    return pltpu.CompilerParams(dimension_semantics=semantics,
                                vmem_limit_bytes=int(min(vmem_bytes, VMEM_CAP)))


def _rms(x, g):
    ms = jnp.mean(x * x, axis=-1, keepdims=True)
    return x * lax.rsqrt(ms + RMS_EPS) * g


def _suffix_matrix(n):
    r = lax.broadcasted_iota(jnp.int32, (n, n), 0)
    c = lax.broadcasted_iota(jnp.int32, (n, n), 1)
    return jnp.where(r > c, 1.0, 0.0).astype(BF16)


def _sb_weights(z2, vis, umat, carry):
    neg_abs = lax.bitcast_convert_type(lax.bitcast_convert_type(z2, jnp.uint32) | jnp.uint32(0x80000000), F32)
    sp = jnp.maximum(z2, 0.0) + jnp.log2(1.0 + jnp.exp2(neg_abs))
    if vis is not None:
        sp = jnp.where(vis, sp, 0.0)
    later = jnp.dot(sp.astype(BF16), umat, preferred_element_type=F32)
    w = jnp.exp2(z2 - sp - later - carry)
    if vis is not None:
        w = jnp.where(vis, w, 0.0)
    return w, carry + jnp.sum(sp, axis=-1, keepdims=True)


def _ffn_kernel(x_ref, g_ref, wg_ref, wu_ref, wd_ref, o_ref, h_ref, *, acc_cols):
    j = pl.program_id(1)

    @pl.when(j == 0)
    def _():
        h_ref[...] = _rms(x_ref[...], g_ref[0:1, :]).astype(BF16)
        o_ref[...] = jnp.zeros_like(o_ref)

    h = h_ref[...]
    gate = jnp.dot(h, wg_ref[...], preferred_element_type=F32)
    up = jnp.dot(h, wu_ref[...], preferred_element_type=F32)
    a = (gate * jax.nn.sigmoid(gate) * up).astype(BF16)
    for c0 in range(0, o_ref.shape[1], acc_cols):
        cols = slice(c0, c0 + acc_cols)
        o_ref[:, cols] += jnp.dot(a, wd_ref[:, cols], preferred_element_type=F32)

    @pl.when(j == pl.num_programs(1) - 1)
    def _():
        o_ref[...] = x_ref[...] + 0.5 * _rms(o_ref[...], g_ref[1:2, :])


def _ffn_half(x, g, w_gu, w_d, layer, *, tm, tf):
    m, d = x.shape
    f = w_d.shape[1]
    nf = f // tf
    vmem = 2 * (2 * tm * d * 4) + tm * d * 2 + 2 * 3 * d * tf * 2 + 6 * tm * tf * 4 + 2 * MIB
    return pl.pallas_call(
        functools.partial(_ffn_kernel, acc_cols=min(d, 512)),
        grid=(m // tm, nf),
        in_specs=[pl.BlockSpec((tm, d), lambda i, j: (i, 0)),
                  pl.BlockSpec((2, d), lambda i, j: (0, 0)),
                  pl.BlockSpec((None, d, tf), lambda i, j: (layer, 0, j)),
                  pl.BlockSpec((None, d, tf), lambda i, j: (layer, 0, j + nf)),
                  pl.BlockSpec((None, tf, d), lambda i, j: (layer, j, 0))],
        out_specs=pl.BlockSpec((tm, d), lambda i, j: (i, 0)),
        out_shape=jax.ShapeDtypeStruct((m, d), F32),
        scratch_shapes=[pltpu.VMEM((tm, d), BF16)],
        compiler_params=_params(("parallel", "arbitrary"), vmem),
        name="ffn_half",
    )(x, g, w_gu, w_gu, w_d)


def _norm_proj_kernel(x_ref, g_ref, w_ref, *refs, bounds, by_head):
    o_refs, h_ref = refs[:-1], refs[-1]
    j = pl.program_id(1)

    @pl.when(j == 0)
    def _():
        h_ref[...] = _rms(x_ref[...], g_ref[...]).astype(BF16)

    res = jnp.dot(h_ref[...], w_ref[...], preferred_element_type=F32)
    for o_ref, (lo, hi), heads in zip(o_refs, bounds, by_head):
        @pl.when(jnp.logical_and(j >= lo, j < hi))
        def _(o_ref=o_ref, heads=heads):
            out = res.astype(o_ref.dtype)
            if heads:
                for hh in range(o_ref.shape[1]):
                    o_ref[0, hh] = out[:, hh * HEAD_DIM:(hh + 1) * HEAD_DIM]
            else:
                o_ref[...] = out


def _norm_proj(x, g, w, layer, col0, segments, *, tm, tn, seq_len=None):
    m, d = x.shape
    assert col0 % tn == 0 and m % tm == 0 and all(n % tn == 0 for n, _, _ in segments)
    cb = col0 // tn
    bounds, lo = [], 0
    for n, _, _ in segments:
        bounds.append((lo, lo + n // tn))
        lo += n // tn
    out_specs, out_shapes = [], []
    for (n, dtype, heads), (a, b) in zip(segments, bounds):
        local = lambda j, a=a, b=b: jnp.clip(j - a, 0, b - a - 1)
        if heads:
            assert seq_len % tm == 0 and tn % HEAD_DIM == 0
            tiles = seq_len // tm
            out_specs.append(pl.BlockSpec((1, tn // HEAD_DIM, tm, HEAD_DIM),
                                          lambda i, j, local=local: (i // tiles, local(j), i % tiles, 0)))
            out_shapes.append(jax.ShapeDtypeStruct((m // seq_len, n // HEAD_DIM, seq_len, HEAD_DIM), dtype))
        else:
            out_specs.append(pl.BlockSpec((tm, tn), lambda i, j, local=local: (i, local(j))))
            out_shapes.append(jax.ShapeDtypeStruct((m, n), dtype))
    vmem = 2 * tm * d * 4 + tm * d * 2 + 2 * d * tn * 2 + (2 * len(segments) + 2) * tm * tn * 4 + 4 * MIB
    return pl.pallas_call(
        functools.partial(_norm_proj_kernel, bounds=tuple(bounds), by_head=tuple(s[2] for s in segments)),
        grid=(m // tm, lo),
        in_specs=[pl.BlockSpec((tm, d), lambda i, j: (i, 0)),
                  pl.BlockSpec((1, d), lambda i, j: (0, 0)),
                  pl.BlockSpec((None, d, tn), lambda i, j: (layer, 0, j + cb))],
        out_specs=out_specs,
        out_shape=out_shapes,
        scratch_shapes=[pltpu.VMEM((tm, d), BF16)],
        compiler_params=_params(("parallel", "arbitrary"), vmem),
        name="norm_proj",
    )(x, g, w)


def _sb_prompt_kernel(bias_ref, q_ref, k_ref, v_ref, o_ref, kb_ref, vb_ref, u_ref):
    tq, kc = SB_QTILE, SB_KCHUNK
    kb_ref[...] = k_ref[0, 0].astype(BF16)
    vb_ref[...] = v_ref[0, 0].astype(BF16)
    u_ref[...] = _suffix_matrix(kc)
    bias = bias_ref[pl.program_id(1)] * LOG2E
    scale = HEAD_DIM ** -0.5 * LOG2E
    r = lax.broadcasted_iota(jnp.int32, (tq, kc), 0)
    c = lax.broadcasted_iota(jnp.int32, (tq, kc), 1)
    for i in range(q_ref.shape[2] // tq):
        q = q_ref[0, 0, i * tq:(i + 1) * tq, :]
        acc = jnp.zeros((tq, HEAD_DIM), F32)
        carry = jnp.zeros((tq, 1), F32)
        for j in range((i + 1) * tq // kc - 1, -1, -1):
            rows = slice(j * kc, (j + 1) * kc)
            vis = (c + (j * kc - i * tq) < r) if (j + 1) * kc > i * tq else None
            z = lax.dot_general(q, kb_ref[rows, :], _NT, preferred_element_type=F32) * scale + bias
            w, carry = _sb_weights(z, vis, u_ref[...], carry)
            acc = acc + jnp.dot(w.astype(BF16), vb_ref[rows, :], preferred_element_type=F32)
        o_ref[0, i * tq:(i + 1) * tq, :] = acc.astype(o_ref.dtype)


def _sb_prompt(q, k, v, bias):
    b, nh, t, _ = q.shape
    assert t % SB_QTILE == 0 and SB_QTILE % SB_KCHUNK == 0
    head_spec = pl.BlockSpec((1, 1, t, HEAD_DIM), lambda bb, h: (bb, h, 0, 0))
    return pl.pallas_call(
        _sb_prompt_kernel,
        grid=(b, nh),
        in_specs=[pl.BlockSpec(memory_space=pltpu.SMEM), head_spec, head_spec, head_spec],
        out_specs=pl.BlockSpec((1, t, HEAD_DIM), lambda bb, h: (bb, 0, h)),
        out_shape=jax.ShapeDtypeStruct((b, t, nh * HEAD_DIM), BF16),
        scratch_shapes=[pltpu.VMEM((t, HEAD_DIM), BF16), pltpu.VMEM((t, HEAD_DIM), BF16),
                        pltpu.VMEM((SB_KCHUNK, SB_KCHUNK), BF16)],
        compiler_params=_params(("parallel", "parallel"), 32 * MIB),
        name="sb_prompt",
    )(bias, q, k, v)


def _sb_decode_kernel(pt_ref, q_ref, knew_ref, vnew_ref, bias_ref, *refs, pages_per_step, n_heads):
    del pt_ref
    k_refs = refs[:pages_per_step]
    v_refs = refs[pages_per_step:2 * pages_per_step]
    o_ref, acc_ref, car_ref = refs[2 * pages_per_step:]
    p = pl.program_id(1)
    rows = n_heads * SAMPLE_ROWS
    scale = HEAD_DIM ** -0.5 * LOG2E
    umat = _suffix_matrix(SB_BLOCK)
    bias = bias_ref[...] * LOG2E
    q = q_ref[0]
    qh = [q[h * SAMPLE_ROWS:(h + 1) * SAMPLE_ROWS].astype(BF16) for h in range(n_heads)]

    def block(keys, values, vis):
        zs = [lax.dot_general(qh[h], keys(h), _NT, preferred_element_type=F32) for h in range(n_heads)]
        z = jnp.concatenate(zs, axis=0) * scale + bias
        w, carry = _sb_weights(z, vis, umat, car_ref[...])
        outs = [jnp.dot(w[h * SAMPLE_ROWS:(h + 1) * SAMPLE_ROWS].astype(BF16), values(h),
                        preferred_element_type=F32) for h in range(n_heads)]
        acc_ref[...] += jnp.concatenate(outs, axis=0)
        car_ref[...] = carry

    @pl.when(p == 0)
    def _():
        acc_ref[...] = jnp.zeros_like(acc_ref)
        car_ref[...] = jnp.zeros_like(car_ref)
        t = lax.broadcasted_iota(jnp.int32, (rows, SB_BLOCK), 0) % SAMPLE_ROWS
        s = lax.broadcasted_iota(jnp.int32, (rows, SB_BLOCK), 1)
        zeros = jnp.zeros((SB_BLOCK - SAMPLE_ROWS, HEAD_DIM), F32)

        def new_block(ref, h):
            head = ref[0, :, h * HEAD_DIM:(h + 1) * HEAD_DIM]
            return jnp.concatenate([head, zeros], axis=0).astype(BF16)

        block(functools.partial(new_block, knew_ref), functools.partial(new_block, vnew_ref), s < t)

    for c in range(pages_per_step):
        block(lambda h, r=k_refs[c]: r[h].astype(BF16), lambda h, r=v_refs[c]: r[h].astype(BF16), None)

    @pl.when(p == pl.num_programs(1) - 1)
    def _():
        o_ref[0] = acc_ref[...]


def _sb_decode(q_rows, k_new, v_new, bias_rows, cache_k, cache_v, page_flat, layer, *, pages_per_step):
    bsz, rows, _ = q_rows.shape
    n_heads = rows // SAMPLE_ROWS
    width = n_heads * HEAD_DIM
    page = cache_k.shape[3]
    n_pages = page_flat.shape[0] // bsz
    assert page == SB_BLOCK and n_pages % pages_per_step == 0
    steps = n_pages // pages_per_step

    def page_map(c):
        return lambda b, p, pt: (layer, pt[b * n_pages + n_pages - 1 - (p * pages_per_step + c)], 0, 0, 0)

    def page_specs():
        return [pl.BlockSpec((None, None, n_heads, page, HEAD_DIM), page_map(c)) for c in range(pages_per_step)]

    seq_map = lambda b, p, pt: (b, 0, 0)
    grid_spec = pltpu.PrefetchScalarGridSpec(
        num_scalar_prefetch=1,
        grid=(bsz, steps),
        in_specs=[pl.BlockSpec((1, rows, HEAD_DIM), seq_map),
                  pl.BlockSpec((1, SAMPLE_ROWS, width), seq_map),
                  pl.BlockSpec((1, SAMPLE_ROWS, width), seq_map),
                  pl.BlockSpec((rows, 1), lambda b, p, pt: (0, 0))] + page_specs() + page_specs(),
        out_specs=pl.BlockSpec((1, rows, HEAD_DIM), seq_map),
        scratch_shapes=[pltpu.VMEM((rows, HEAD_DIM), F32), pltpu.VMEM((rows, 1), F32)],
    )
    vmem = 2 * 2 * pages_per_step * page * width * 4 + 12 * MIB
    return pl.pallas_call(
        functools.partial(_sb_decode_kernel, pages_per_step=pages_per_step, n_heads=n_heads),
        grid_spec=grid_spec,
        out_shape=jax.ShapeDtypeStruct((bsz, rows, HEAD_DIM), F32),
        compiler_params=_params(("parallel", "arbitrary"), vmem),
        name="sb_decode",
    )(page_flat, q_rows, k_new, v_new, bias_rows, *([cache_k] * pages_per_step), *([cache_v] * pages_per_step))


def _mixer_kernel(*refs, pool, n_prefix, halo_valid_from, mem_by_head):
    if pool:
        (u_ref, halo_ref, qm_ref, mk_ref, mv_ref, x_ref, g_ref, wo_ref, gm_ref, sc_ref,
         o_ref, cat_ref, ubuf_ref) = refs
    else:
        tok_ref, qm_ref, mk_ref, mv_ref, x_ref, g_ref, wo_ref, o_ref, cat_ref = refs
    t = pl.program_id(1)
    tm = x_ref.shape[1]
    tok_width = cat_ref.shape[1] - MEM_WIDTH

    if pool:
        group = tok_width // len(POOL_WINDOWS)
        halo = halo_ref[0]
        ubuf_ref[0:HALO, :] = jnp.where(t >= halo_valid_from, halo, 0.0)
        ubuf_ref[HALO:HALO + tm, :] = u_ref[0]
        pos = t * tm + lax.broadcasted_iota(jnp.int32, (tm, 1), 0) + (n_prefix + 1)
        for gi, win in enumerate(POOL_WINDOWS):
            cols = slice(gi * group, (gi + 1) * group)
            own = ubuf_ref[HALO:HALO + tm, cols]
            tot = own
            for back in range(1, win):
                tot = tot + ubuf_ref[HALO - back:HALO - back + tm, cols]
            count = jnp.minimum(pos, win).astype(F32)
            diff = tot / count - own
            mixed = jnp.dot(diff.astype(BF16), gm_ref[gi], preferred_element_type=F32) * sc_ref[:, cols]
            cat_ref[:, cols] = mixed.astype(BF16)
    else:
        cat_ref[:, 0:tok_width] = tok_ref[0].astype(BF16)

    scale = HEAD_DIM ** -0.5
    if mem_by_head:
        mk_heads = pltpu.einshape("mhd->hmd", mk_ref[0].astype(BF16))
        mv_heads = pltpu.einshape("mhd->hmd", mv_ref[0].astype(BF16))
    for h in range(MEM_HEADS):
        cols = slice(h * HEAD_DIM, (h + 1) * HEAD_DIM)
        qh = qm_ref[0, :, cols].astype(BF16)
        if mem_by_head:
            kh, vh = mk_heads[h], mv_heads[h]
        else:
            kh, vh = mk_ref[0, :, cols].astype(BF16), mv_ref[0, :, cols].astype(BF16)
        s = lax.dot_general(qh, kh, _NT, preferred_element_type=F32) * scale
        e = jnp.exp(s - jnp.max(s, axis=-1, keepdims=True))
        prob = e / jnp.sum(e, axis=-1, keepdims=True)
        oh = jnp.dot(prob.astype(BF16), vh, preferred_element_type=F32)
        cat_ref[:, tok_width + h * HEAD_DIM:tok_width + (h + 1) * HEAD_DIM] = oh.astype(BF16)

    y = jnp.dot(cat_ref[...], wo_ref[...], preferred_element_type=F32)
    o_ref[0] = x_ref[0] + _rms(y, g_ref[1:2, :])


def _mixer_out(x, g, w_out, layer, qm_src, qm_block, mk, mv, *, tm, tok=None, pool_src=None, halo_src=None,
               halo_valid_from=0, n_prefix=0, group_maps=None, pool_scale=None):
    b, t, d = x.shape
    tok_width = d - MEM_WIDTH
    mem_len = mk.shape[1]
    pool = tok is None
    mem_by_head = mk.ndim == 4
    qm_spec = pl.BlockSpec((1, tm, MEM_WIDTH), lambda bb, i: (bb, i, qm_block))
    if mem_by_head:
        mem_spec = pl.BlockSpec((1, mem_len, MEM_HEADS, HEAD_DIM), lambda bb, i: (bb, 0, 0, 0))
    else:
        mem_spec = pl.BlockSpec((1, mem_len, MEM_WIDTH), lambda bb, i: (bb, 0, 0))
    x_spec = pl.BlockSpec((1, tm, d), lambda bb, i: (bb, i, 0))
    g_spec = pl.BlockSpec((2, d), lambda bb, i: (0, 0))
    wo_spec = pl.BlockSpec((None, d, d), lambda bb, i: (layer, 0, 0))
    scratch = [pltpu.VMEM((tm, d), BF16)]
    if pool:
        halo_blocks = tm // HALO
        if halo_src is pool_src:
            halo_map = lambda bb, i: (bb, jnp.maximum(i * halo_blocks - 1, 0), 0)
        else:
            halo_map = lambda bb, i: (bb, 0, 0)
        _, ng, gw, _ = group_maps.shape
        in_specs = [pl.BlockSpec((1, tm, tok_width), lambda bb, i: (bb, i, 0)),
                    pl.BlockSpec((1, HALO, tok_width), halo_map),
                    qm_spec, mem_spec, mem_spec, x_spec, g_spec, wo_spec,
                    pl.BlockSpec((None, ng, gw, gw), lambda bb, i: (layer, 0, 0, 0)),
                    pl.BlockSpec((1, tok_width), lambda bb, i: (0, 0))]
        args = (pool_src, halo_src, qm_src, mk, mv, x, g, w_out, group_maps, pool_scale)
        scratch.append(pltpu.VMEM((HALO + tm, tok_width), F32))
    else:
        in_specs = [pl.BlockSpec((1, tm, tok_width), lambda bb, i: (bb, i, 0)),
                    qm_spec, mem_spec, mem_spec, x_spec, g_spec, wo_spec]
        args = (tok, qm_src, mk, mv, x, g, w_out)
    vmem = (2 * (tm * tok_width * 4 + tm * MEM_WIDTH * 4 + 2 * tm * d * 4 + 4 * mem_len * MEM_WIDTH * 4 + d * d * 2)
            + (HALO + tm) * tok_width * 4 + tm * d * 2 + 3 * tm * d * 4 + 6 * MIB)
    return pl.pallas_call(
        functools.partial(_mixer_kernel, pool=pool, n_prefix=n_prefix, halo_valid_from=halo_valid_from,
                          mem_by_head=mem_by_head),
        grid=(b, t // tm),
        in_specs=in_specs,
        out_specs=x_spec,
        out_shape=jax.ShapeDtypeStruct((b, t, d), F32),
        scratch_shapes=scratch,
        compiler_params=_params(("parallel", "arbitrary"), vmem),
        name="mixer_pool" if pool else "mixer_sb",
    )(*args)


def kernel(x_prompt, x_sample, state_pool, cache_sb_k, cache_sb_v, cache_mem_k, cache_mem_v, page_table, mem_prompt, norm_ffn1, ffn1_w_gate_up, ffn1_w_down, norm_mix, norm_mem, mem_w_kv, pool_w_in, pool_group_maps, pool_scale, pool_w_out, sb_w_in, sb_logit_bias, sb_w_out, norm_ffn2, ffn2_w_gate_up, ffn2_w_down):
    b, t, d = x_prompt.shape
    bs, ts, _ = x_sample.shape
    depth = norm_ffn1.shape[0]
    mem_len = mem_prompt.shape[1]
    tok_width = d - MEM_WIDTH
    n_heads = tok_width // HEAD_DIM
    assert ts <= SAMPLE_ROWS

    tm_ffn, tf_ffn = 1024, 512
    tm_proj, tn_proj = 1024, 512
    tm_mix = 256
    ms = bs * SAMPLE_ROWS

    w1_gu, w1_d = ffn1_w_gate_up.astype(BF16), ffn1_w_down.astype(BF16)
    w2_gu, w2_d = ffn2_w_gate_up.astype(BF16), ffn2_w_down.astype(BF16)
    w_kv = mem_w_kv.astype(BF16)
    wp_in, wp_out, gmaps = pool_w_in.astype(BF16), pool_w_out.astype(BF16), pool_group_maps.astype(BF16)
    ws_in, ws_out = sb_w_in.astype(BF16), sb_w_out.astype(BF16)

    xp = x_prompt.reshape(b * t, d)
    xs = jnp.pad(x_sample, ((0, 0), (0, SAMPLE_ROWS - ts), (0, 0))).reshape(ms, d)
    mem2 = mem_prompt.reshape(b * mem_len, d)
    page_flat = page_table.reshape(-1).astype(jnp.int32)
    cache_k = jnp.transpose(cache_sb_k, (0, 1, 3, 2, 4))
    cache_v = jnp.transpose(cache_sb_v, (0, 1, 3, 2, 4))

    pool_p, pool_s, kp, vp, ks, vs, mkp, mvp = [], [], [], [], [], [], [], []
    for i in range(depth):
        li = i // N_MIXERS
        xp = _ffn_half(xp, norm_ffn1[i], w1_gu, w1_d, i, tm=tm_ffn, tf=tf_ffn)
        xs = _ffn_half(xs, norm_ffn1[i], w1_gu, w1_d, i, tm=ms, tf=tf_ffn)

        g_mem = norm_mem[i][None]
        flat = lambda n: (n, F32, False)
        mk_p, mv_p = _norm_proj(mem2, g_mem, w_kv, i, 0, (flat(MEM_WIDTH), flat(MEM_WIDTH)), tm=tm_proj, tn=tn_proj)
        mk_p = mk_p.reshape(b, mem_len, MEM_WIDTH)
        mv_p = mv_p.reshape(b, mem_len, MEM_WIDTH)
        mkp.append(mk_p.reshape(b, mem_len, MEM_HEADS, HEAD_DIM))
        mvp.append(mv_p.reshape(b, mem_len, MEM_HEADS, HEAD_DIM))
        mk_s, mv_s = cache_mem_k[i], cache_mem_v[i]

        g_mix = norm_mix[i]
        g_in = g_mix[0:1]
        xp3 = xp.reshape(b, t, d)
        xs3 = xs.reshape(bs, SAMPLE_ROWS, d)
        if i % N_MIXERS == 0:
            scale = pool_scale[li][None]
            (proj_p,) = _norm_proj(xp, g_in, wp_in, li, 0, (flat(d),), tm=tm_proj, tn=tn_proj)
            (proj_s,) = _norm_proj(xs, g_in, wp_in, li, 0, (flat(d),), tm=ms, tn=tn_proj)
            proj_p = proj_p.reshape(b, t, d)
            proj_s = proj_s.reshape(bs, SAMPLE_ROWS, d)
            qm_block = tok_width // MEM_WIDTH
            xp3 = _mixer_out(xp3, g_mix, wp_out, li, proj_p, qm_block, mk_p, mv_p, tm=tm_mix, pool_src=proj_p,
                             halo_src=proj_p, halo_valid_from=1, n_prefix=0, group_maps=gmaps, pool_scale=scale)
            prefix = state_pool[li]
            n_prefix = prefix.shape[1]
            halo_s = jnp.pad(prefix, ((0, 0), (HALO - n_prefix, 0), (0, 0)))
            xs3 = _mixer_out(xs3, g_mix, wp_out, li, proj_s, qm_block, mk_s, mv_s, tm=SAMPLE_ROWS, pool_src=proj_s,
                             halo_src=halo_s, halo_valid_from=0, n_prefix=n_prefix, group_maps=gmaps,
                             pool_scale=scale)
            pool_p.append(proj_p[:, t - POOL_BUF:, :tok_width])
            u_ext = jnp.concatenate([prefix, proj_s[:, :ts, :tok_width]], axis=1)
            pool_s.append(u_ext[:, u_ext.shape[1] - POOL_BUF:])
        else:
            bias = sb_logit_bias[li].astype(F32)
            q_p, k_p, v_p, qm_p = _norm_proj(
                xp, g_in, ws_in, li, 0,
                ((tok_width, BF16, True), (tok_width, F32, True), (tok_width, F32, True), flat(MEM_WIDTH)),
                tm=tm_proj, tn=tn_proj, seq_len=t)
            o_p = _sb_prompt(q_p, k_p, v_p, bias)
            xp3 = _mixer_out(xp3, g_mix, ws_out, li, qm_p.reshape(b, t, MEM_WIDTH), 0, mk_p, mv_p, tm=tm_mix,
                             tok=o_p)
            kp.append(jnp.transpose(k_p, (0, 2, 1, 3)))
            vp.append(jnp.transpose(v_p, (0, 2, 1, 3)))

            q_s, k_s, v_s, qm_s = _norm_proj(
                xs, g_in, ws_in, li, 0, (flat(tok_width), flat(tok_width), flat(tok_width), flat(MEM_WIDTH)),
                tm=ms, tn=tn_proj)
            k_s3 = k_s.reshape(bs, SAMPLE_ROWS, tok_width)
            v_s3 = v_s.reshape(bs, SAMPLE_ROWS, tok_width)
            q_rows = q_s.reshape(bs, SAMPLE_ROWS, n_heads, HEAD_DIM).transpose(0, 2, 1, 3)
            q_rows = q_rows.reshape(bs, n_heads * SAMPLE_ROWS, HEAD_DIM)
            bias_rows = jnp.repeat(bias, SAMPLE_ROWS)[:, None]
            o_rows = _sb_decode(q_rows, k_s3, v_s3, bias_rows, cache_k, cache_v, page_flat, li,
                                pages_per_step=8)
            o_s = o_rows.reshape(bs, n_heads, SAMPLE_ROWS, HEAD_DIM).transpose(0, 2, 1, 3)
            o_s = o_s.reshape(bs, SAMPLE_ROWS, tok_width)
            xs3 = _mixer_out(xs3, g_mix, ws_out, li, qm_s.reshape(bs, SAMPLE_ROWS, MEM_WIDTH), 0, mk_s, mv_s,
                             tm=SAMPLE_ROWS, tok=o_s)
            ks.append(k_s3[:, :ts].reshape(bs, ts, n_heads, HEAD_DIM))
            vs.append(v_s3[:, :ts].reshape(bs, ts, n_heads, HEAD_DIM))
        xp = xp3.reshape(b * t, d)
        xs = xs3.reshape(ms, d)

        xp = _ffn_half(xp, norm_ffn2[i], w2_gu, w2_d, i, tm=tm_ffn, tf=tf_ffn)
        xs = _ffn_half(xs, norm_ffn2[i], w2_gu, w2_d, i, tm=ms, tf=tf_ffn)

    y_p = xp.reshape(b, t, d)
    y_s = xs.reshape(bs, SAMPLE_ROWS, d)[:, :ts]
    return (y_p, y_s, jnp.stack(pool_p), jnp.stack(kp), jnp.stack(vp), jnp.stack(mkp), jnp.stack(mvp),
            jnp.stack(pool_s), jnp.stack(ks), jnp.stack(vs))
```

```python
import functools

import jax
import jax.numpy as jnp
from jax import lax
from jax.experimental import pallas as pl
from jax.experimental.pallas import tpu as pltpu

F32 = jnp.float32
BF16 = jnp.bfloat16

HEAD_DIM = 128
MEM_HEADS = 4
MEM_WIDTH = MEM_HEADS * HEAD_DIM
POOL_WINDOWS = (2, 4, 8, 16)
POOL_BUF = max(POOL_WINDOWS) - 1
HALO = 16
N_MIXERS = 2
SB_BLOCK = 128
SB_QTILE = 512
SB_KCHUNK = 256
RMS_EPS = 1e-6
SAMPLE_ROWS = 8
LOG2E = 1.4426950408889634
MIB = 1024 * 1024
VMEM_CAP = 62 * MIB

_NT = (((1,), (1,)), ((), ()))


def _params(semantics, vmem_bytes):
    return pltpu.CompilerParams(dimension_semantics=semantics,
                                vmem_limit_bytes=int(min(vmem_bytes, VMEM_CAP)))


def _rms(x, g):
    ms = jnp.mean(x * x, axis=-1, keepdims=True)
    return x * lax.rsqrt(ms + RMS_EPS) * g


def _suffix_matrix(n):
    r = lax.broadcasted_iota(jnp.int32, (n, n), 0)
    c = lax.broadcasted_iota(jnp.int32, (n, n), 1)
    return jnp.where(r > c, 1.0, 0.0).astype(BF16)


def _sb_weights(z2, vis, umat, carry):
    neg_abs = lax.bitcast_convert_type(lax.bitcast_convert_type(z2, jnp.uint32) | jnp.uint32(0x80000000), F32)
    sp = jnp.maximum(z2, 0.0) + jnp.log2(1.0 + jnp.exp2(neg_abs))
    if vis is not None:
        sp = jnp.where(vis, sp, 0.0)
    later = jnp.dot(sp.astype(BF16), umat, preferred_element_type=F32)
    w = jnp.exp2(z2 - sp - later - carry)
    if vis is not None:
        w = jnp.where(vis, w, 0.0)
    return w, carry + jnp.sum(sp, axis=-1, keepdims=True)


def _ffn_kernel(x_ref, g_ref, wg_ref, wu_ref, wd_ref, o_ref, h_ref, *, acc_cols):
    j = pl.program_id(1)

    @pl.when(j == 0)
    def _():
        h_ref[...] = _rms(x_ref[...], g_ref[0:1, :]).astype(BF16)
        o_ref[...] = jnp.zeros_like(o_ref)

    h = h_ref[...]
    gate = jnp.dot(h, wg_ref[...], preferred_element_type=F32)
    up = jnp.dot(h, wu_ref[...], preferred_element_type=F32)
    a = (gate * jax.nn.sigmoid(gate) * up).astype(BF16)
    for c0 in range(0, o_ref.shape[1], acc_cols):
        cols = slice(c0, c0 + acc_cols)
        o_ref[:, cols] += jnp.dot(a, wd_ref[:, cols], preferred_element_type=F32)

    @pl.when(j == pl.num_programs(1) - 1)
    def _():
        o_ref[...] = x_ref[...] + 0.5 * _rms(o_ref[...], g_ref[1:2, :])


def _ffn_half(x, g, w_gu, w_d, layer, *, tm, tf):
    m, d = x.shape
    f = w_d.shape[1]
    nf = f // tf
    vmem = 2 * (2 * tm * d * 4) + tm * d * 2 + 2 * 3 * d * tf * 2 + 6 * tm * tf * 4 + 2 * MIB
    return pl.pallas_call(
        functools.partial(_ffn_kernel, acc_cols=min(d, 512)),
        grid=(m // tm, nf),
        in_specs=[pl.BlockSpec((tm, d), lambda i, j: (i, 0)),
                  pl.BlockSpec((2, d), lambda i, j: (0, 0)),
                  pl.BlockSpec((None, d, tf), lambda i, j: (layer, 0, j)),
                  pl.BlockSpec((None, d, tf), lambda i, j: (layer, 0, j + nf)),
                  pl.BlockSpec((None, tf, d), lambda i, j: (layer, j, 0))],
        out_specs=pl.BlockSpec((tm, d), lambda i, j: (i, 0)),
        out_shape=jax.ShapeDtypeStruct((m, d), F32),
        scratch_shapes=[pltpu.VMEM((tm, d), BF16)],
        compiler_params=_params(("parallel", "arbitrary"), vmem),
        name="ffn_half",
    )(x, g, w_gu, w_gu, w_d)


def _norm_proj_kernel(x_ref, g_ref, w_ref, *refs, bounds, by_head, n_carried):
    o_refs, h_ref = refs[n_carried:-1], refs[-1]
    j = pl.program_id(1)

    @pl.when(j == 0)
    def _():
        h_ref[...] = _rms(x_ref[...], g_ref[...]).astype(BF16)

    res = jnp.dot(h_ref[...], w_ref[...], preferred_element_type=F32)
    for o_ref, (lo, hi), heads in zip(o_refs, bounds, by_head):
        @pl.when(jnp.logical_and(j >= lo, j < hi))
        def _(o_ref=o_ref, heads=heads):
            out = res.astype(o_ref.dtype)
            if heads:
                for hh in range(o_ref.shape[1]):
                    o_ref[0, hh] = out[:, hh * HEAD_DIM:(hh + 1) * HEAD_DIM]
            else:
                o_ref[...] = out


def _norm_proj(x, g, w, layer, col0, segments, *, tm, tn, seq_len=None, slots=None):
    m, d = x.shape
    assert col0 % tn == 0 and m % tm == 0 and all(n % tn == 0 for n, _, _ in segments)
    cb = col0 // tn
    n_slots, slot, carried = slots if slots is not None else (1, 0, {})
    bounds, lo = [], 0
    for n, _, _ in segments:
        bounds.append((lo, lo + n // tn))
        lo += n // tn
    out_specs, out_shapes, aliases, carried_args = [], [], {}, []
    for si, ((n, dtype, heads), (a, b)) in enumerate(zip(segments, bounds)):
        local = lambda j, a=a, b=b: jnp.clip(j - a, 0, b - a - 1)
        if heads:
            assert seq_len % tm == 0 and tn % HEAD_DIM == 0
            tiles = seq_len // tm
            out_specs.append(pl.BlockSpec((None, 1, tn // HEAD_DIM, tm, HEAD_DIM),
                                          lambda i, j, local=local: (slot, i // tiles, local(j), i % tiles, 0)))
            out_shapes.append(jax.ShapeDtypeStruct((n_slots, m // seq_len, n // HEAD_DIM, seq_len, HEAD_DIM), dtype))
            if si in carried:
                aliases[3 + len(carried_args)] = si
                carried_args.append(carried[si])
        else:
            out_specs.append(pl.BlockSpec((tm, tn), lambda i, j, local=local: (i, local(j))))
            out_shapes.append(jax.ShapeDtypeStruct((m, n), dtype))
    vmem = 2 * tm * d * 4 + tm * d * 2 + 2 * d * tn * 2 + (2 * len(segments) + 2) * tm * tn * 4 + 4 * MIB
    return pl.pallas_call(
        functools.partial(_norm_proj_kernel, bounds=tuple(bounds), by_head=tuple(s[2] for s in segments),
                          n_carried=len(carried_args)),
        grid=(m // tm, lo),
        in_specs=[pl.BlockSpec((tm, d), lambda i, j: (i, 0)),
                  pl.BlockSpec((1, d), lambda i, j: (0, 0)),
                  pl.BlockSpec((None, d, tn), lambda i, j: (layer, 0, j + cb))]
                 + [pl.BlockSpec(memory_space=pl.ANY)] * len(carried_args),
        out_specs=out_specs,
        out_shape=out_shapes,
        input_output_aliases=aliases,
        scratch_shapes=[pltpu.VMEM((tm, d), BF16)],
        compiler_params=_params(("parallel", "arbitrary"), vmem),
        name="norm_proj",
    )(x, g, w, *carried_args)


def _sb_prompt_kernel(bias_ref, q_ref, k_ref, v_ref, o_ref, kb_ref, vb_ref, u_ref):
    tq, kc = SB_QTILE, SB_KCHUNK
    kb_ref[...] = k_ref[0, 0].astype(BF16)
    vb_ref[...] = v_ref[0, 0].astype(BF16)
    u_ref[...] = _suffix_matrix(kc)
    bias = bias_ref[pl.program_id(1)] * LOG2E
    scale = HEAD_DIM ** -0.5 * LOG2E
    r = lax.broadcasted_iota(jnp.int32, (tq, kc), 0)
    c = lax.broadcasted_iota(jnp.int32, (tq, kc), 1)
    for i in range(q_ref.shape[2] // tq):
        q = q_ref[0, 0, i * tq:(i + 1) * tq, :]
        acc = jnp.zeros((tq, HEAD_DIM), F32)
        carry = jnp.zeros((tq, 1), F32)
        for j in range((i + 1) * tq // kc - 1, -1, -1):
            rows = slice(j * kc, (j + 1) * kc)
            vis = (c + (j * kc - i * tq) < r) if (j + 1) * kc > i * tq else None
            z = lax.dot_general(q, kb_ref[rows, :], _NT, preferred_element_type=F32) * scale + bias
            w, carry = _sb_weights(z, vis, u_ref[...], carry)
            acc = acc + jnp.dot(w.astype(BF16), vb_ref[rows, :], preferred_element_type=F32)
        o_ref[0, i * tq:(i + 1) * tq, :] = acc.astype(o_ref.dtype)


def _sb_prompt(q, k, v, bias, slot):
    _, b, nh, t, _ = q.shape
    assert t % SB_QTILE == 0 and SB_QTILE % SB_KCHUNK == 0
    head_spec = pl.BlockSpec((None, 1, 1, t, HEAD_DIM), lambda bb, h: (slot, bb, h, 0, 0))
    return pl.pallas_call(
        _sb_prompt_kernel,
        grid=(b, nh),
        in_specs=[pl.BlockSpec(memory_space=pltpu.SMEM), head_spec, head_spec, head_spec],
        out_specs=pl.BlockSpec((1, t, HEAD_DIM), lambda bb, h: (bb, 0, h)),
        out_shape=jax.ShapeDtypeStruct((b, t, nh * HEAD_DIM), BF16),
        scratch_shapes=[pltpu.VMEM((t, HEAD_DIM), BF16), pltpu.VMEM((t, HEAD_DIM), BF16),
                        pltpu.VMEM((SB_KCHUNK, SB_KCHUNK), BF16)],
        compiler_params=_params(("parallel", "parallel"), 32 * MIB),
        name="sb_prompt",
    )(bias, q, k, v)


def _sb_decode_kernel(pt_ref, q_ref, knew_ref, vnew_ref, bias_ref, *refs, pages_per_step, n_heads):
    del pt_ref
    k_refs = refs[:pages_per_step]
    v_refs = refs[pages_per_step:2 * pages_per_step]
    o_ref, acc_ref, car_ref = refs[2 * pages_per_step:]
    p = pl.program_id(1)
    rows = n_heads * SAMPLE_ROWS
    scale = HEAD_DIM ** -0.5 * LOG2E
    chunk = min(SB_KCHUNK, pages_per_step * SB_BLOCK)
    umat = _suffix_matrix(chunk)
    bias = bias_ref[...] * LOG2E
    q = q_ref[0]
    qh = [q[h * SAMPLE_ROWS:(h + 1) * SAMPLE_ROWS].astype(BF16) for h in range(n_heads)]

    def sweep(keys, values, n_keys, vis):
        zs = [lax.dot_general(qh[h], keys(h), _NT, preferred_element_type=F32) for h in range(n_heads)]
        z = jnp.concatenate(zs, axis=0) * scale + bias
        step = min(chunk, n_keys)
        carry = car_ref[...]
        ws = []
        for c0 in range(n_keys - step, -1, -step):
            w, carry = _sb_weights(z[:, c0:c0 + step], vis, umat[:step, :step], carry)
            ws.insert(0, w)
        w = jnp.concatenate(ws, axis=1)
        outs = [jnp.dot(w[h * SAMPLE_ROWS:(h + 1) * SAMPLE_ROWS].astype(BF16), values(h),
                        preferred_element_type=F32) for h in range(n_heads)]
        acc_ref[...] += jnp.concatenate(outs, axis=0)
        car_ref[...] = carry

    @pl.when(p == 0)
    def _():
        acc_ref[...] = jnp.zeros_like(acc_ref)
        car_ref[...] = jnp.zeros_like(car_ref)
        t = lax.broadcasted_iota(jnp.int32, (rows, SB_BLOCK), 0) % SAMPLE_ROWS
        s = lax.broadcasted_iota(jnp.int32, (rows, SB_BLOCK), 1)
        zeros = jnp.zeros((SB_BLOCK - SAMPLE_ROWS, HEAD_DIM), F32)

        def new_block(ref, h):
            head = ref[0, :, h * HEAD_DIM:(h + 1) * HEAD_DIM]
            return jnp.concatenate([head, zeros], axis=0).astype(BF16)

        sweep(functools.partial(new_block, knew_ref), functools.partial(new_block, vnew_ref), SB_BLOCK, s < t)

    def pages(page_refs, h):
        return jnp.concatenate([r[h].astype(BF16) for r in reversed(page_refs)], axis=0)

    sweep(functools.partial(pages, k_refs), functools.partial(pages, v_refs), pages_per_step * SB_BLOCK, None)

    @pl.when(p == pl.num_programs(1) - 1)
    def _():
        o_ref[0] = acc_ref[...]


def _sb_decode(q_rows, k_new, v_new, bias_rows, cache_k, cache_v, page_flat, layer, *, pages_per_step):
    bsz, rows, _ = q_rows.shape
    n_heads = rows // SAMPLE_ROWS
    width = n_heads * HEAD_DIM
    page = cache_k.shape[3]
    n_pages = page_flat.shape[0] // bsz
    assert page == SB_BLOCK and n_pages % pages_per_step == 0
    steps = n_pages // pages_per_step

    def page_map(c):
        return lambda b, p, pt: (layer, pt[b * n_pages + n_pages - 1 - (p * pages_per_step + c)], 0, 0, 0)

    def page_specs():
        return [pl.BlockSpec((None, None, n_heads, page, HEAD_DIM), page_map(c)) for c in range(pages_per_step)]

    seq_map = lambda b, p, pt: (b, 0, 0)
    grid_spec = pltpu.PrefetchScalarGridSpec(
        num_scalar_prefetch=1,
        grid=(bsz, steps),
        in_specs=[pl.BlockSpec((1, rows, HEAD_DIM), seq_map),
                  pl.BlockSpec((1, SAMPLE_ROWS, width), seq_map),
                  pl.BlockSpec((1, SAMPLE_ROWS, width), seq_map),
                  pl.BlockSpec((rows, 1), lambda b, p, pt: (0, 0))] + page_specs() + page_specs(),
        out_specs=pl.BlockSpec((1, rows, HEAD_DIM), seq_map),
        scratch_shapes=[pltpu.VMEM((rows, HEAD_DIM), F32), pltpu.VMEM((rows, 1), F32)],
    )
    vmem = 2 * 2 * pages_per_step * page * width * 4 + 12 * MIB
    return pl.pallas_call(
        functools.partial(_sb_decode_kernel, pages_per_step=pages_per_step, n_heads=n_heads),
        grid_spec=grid_spec,
        out_shape=jax.ShapeDtypeStruct((bsz, rows, HEAD_DIM), F32),
        compiler_params=_params(("parallel", "arbitrary"), vmem),
        name="sb_decode",
    )(page_flat, q_rows, k_new, v_new, bias_rows, *([cache_k] * pages_per_step), *([cache_v] * pages_per_step))


def _mixer_kernel(*refs, pool, n_prefix, halo_valid_from, mem_by_head):
    if pool:
        (u_ref, halo_ref, qm_ref, mk_ref, mv_ref, x_ref, g_ref, wo_ref, gm_ref, sc_ref,
         o_ref, cat_ref, ubuf_ref) = refs
    else:
        tok_ref, qm_ref, mk_ref, mv_ref, x_ref, g_ref, wo_ref, o_ref, cat_ref = refs
    t = pl.program_id(1)
    tm = x_ref.shape[1]
    tok_width = cat_ref.shape[1] - MEM_WIDTH

    if pool:
        group = tok_width // len(POOL_WINDOWS)
        halo = halo_ref[0]
        ubuf_ref[0:HALO, :] = jnp.where(t >= halo_valid_from, halo, 0.0)
        ubuf_ref[HALO:HALO + tm, :] = u_ref[0]
        pos = t * tm + lax.broadcasted_iota(jnp.int32, (tm, 1), 0) + (n_prefix + 1)
        for gi, win in enumerate(POOL_WINDOWS):
            cols = slice(gi * group, (gi + 1) * group)
            own = ubuf_ref[HALO:HALO + tm, cols]
            tot = own
            for back in range(1, win):
                tot = tot + ubuf_ref[HALO - back:HALO - back + tm, cols]
            count = jnp.minimum(pos, win).astype(F32)
            diff = tot / count - own
            mixed = jnp.dot(diff.astype(BF16), gm_ref[gi], preferred_element_type=F32) * sc_ref[:, cols]
            cat_ref[:, cols] = mixed.astype(BF16)
    else:
        cat_ref[:, 0:tok_width] = tok_ref[0].astype(BF16)

    scale = HEAD_DIM ** -0.5
    if mem_by_head:
        mk_heads = pltpu.einshape("mhd->hmd", mk_ref[0].astype(BF16))
        mv_heads = pltpu.einshape("mhd->hmd", mv_ref[0].astype(BF16))
    for h in range(MEM_HEADS):
        cols = slice(h * HEAD_DIM, (h + 1) * HEAD_DIM)
        qh = qm_ref[0, :, cols].astype(BF16)
        if mem_by_head:
            kh, vh = mk_heads[h], mv_heads[h]
        else:
            kh, vh = mk_ref[0, :, cols].astype(BF16), mv_ref[0, :, cols].astype(BF16)
        s = lax.dot_general(qh, kh, _NT, preferred_element_type=F32) * scale
        e = jnp.exp(s - jnp.max(s, axis=-1, keepdims=True))
        prob = e / jnp.sum(e, axis=-1, keepdims=True)
        oh = jnp.dot(prob.astype(BF16), vh, preferred_element_type=F32)
        cat_ref[:, tok_width + h * HEAD_DIM:tok_width + (h + 1) * HEAD_DIM] = oh.astype(BF16)

    y = jnp.dot(cat_ref[...], wo_ref[...], preferred_element_type=F32)
    o_ref[0] = x_ref[0] + _rms(y, g_ref[1:2, :])


def _mixer_out(x, g, w_out, layer, qm_src, qm_block, mk, mv, *, tm, tok=None, pool_src=None, halo_src=None,
               halo_valid_from=0, n_prefix=0, group_maps=None, pool_scale=None):
    b, t, d = x.shape
    tok_width = d - MEM_WIDTH
    mem_len = mk.shape[1]
    pool = tok is None
    mem_by_head = mk.ndim == 4
    qm_spec = pl.BlockSpec((1, tm, MEM_WIDTH), lambda bb, i: (bb, i, qm_block))
    if mem_by_head:
        mem_spec = pl.BlockSpec((1, mem_len, MEM_HEADS, HEAD_DIM), lambda bb, i: (bb, 0, 0, 0))
    else:
        mem_spec = pl.BlockSpec((1, mem_len, MEM_WIDTH), lambda bb, i: (bb, 0, 0))
    x_spec = pl.BlockSpec((1, tm, d), lambda bb, i: (bb, i, 0))
    g_spec = pl.BlockSpec((2, d), lambda bb, i: (0, 0))
    wo_spec = pl.BlockSpec((None, d, d), lambda bb, i: (layer, 0, 0))
    scratch = [pltpu.VMEM((tm, d), BF16)]
    if pool:
        halo_blocks = tm // HALO
        if halo_src is pool_src:
            halo_map = lambda bb, i: (bb, jnp.maximum(i * halo_blocks - 1, 0), 0)
        else:
            halo_map = lambda bb, i: (bb, 0, 0)
        _, ng, gw, _ = group_maps.shape
        in_specs = [pl.BlockSpec((1, tm, tok_width), lambda bb, i: (bb, i, 0)),
                    pl.BlockSpec((1, HALO, tok_width), halo_map),
                    qm_spec, mem_spec, mem_spec, x_spec, g_spec, wo_spec,
                    pl.BlockSpec((None, ng, gw, gw), lambda bb, i: (layer, 0, 0, 0)),
                    pl.BlockSpec((1, tok_width), lambda bb, i: (0, 0))]
        args = (pool_src, halo_src, qm_src, mk, mv, x, g, w_out, group_maps, pool_scale)
        scratch.append(pltpu.VMEM((HALO + tm, tok_width), F32))
    else:
        in_specs = [pl.BlockSpec((1, tm, tok_width), lambda bb, i: (bb, i, 0)),
                    qm_spec, mem_spec, mem_spec, x_spec, g_spec, wo_spec]
        args = (tok, qm_src, mk, mv, x, g, w_out)
    vmem = (2 * (tm * tok_width * 4 + tm * MEM_WIDTH * 4 + 2 * tm * d * 4 + 4 * mem_len * MEM_WIDTH * 4 + d * d * 2)
            + (HALO + tm) * tok_width * 4 + tm * d * 2 + 3 * tm * d * 4 + 6 * MIB)
    return pl.pallas_call(
        functools.partial(_mixer_kernel, pool=pool, n_prefix=n_prefix, halo_valid_from=halo_valid_from,
                          mem_by_head=mem_by_head),
        grid=(b, t // tm),
        in_specs=in_specs,
        out_specs=x_spec,
        out_shape=jax.ShapeDtypeStruct((b, t, d), F32),
        scratch_shapes=scratch,
        compiler_params=_params(("parallel", "arbitrary"), vmem),
        name="mixer_pool" if pool else "mixer_sb",
    )(*args)


def kernel(x_prompt, x_sample, state_pool, cache_sb_k, cache_sb_v, cache_mem_k, cache_mem_v, page_table, mem_prompt, norm_ffn1, ffn1_w_gate_up, ffn1_w_down, norm_mix, norm_mem, mem_w_kv, pool_w_in, pool_group_maps, pool_scale, pool_w_out, sb_w_in, sb_logit_bias, sb_w_out, norm_ffn2, ffn2_w_gate_up, ffn2_w_down):
    b, t, d = x_prompt.shape
    bs, ts, _ = x_sample.shape
    depth = norm_ffn1.shape[0]
    mem_len = mem_prompt.shape[1]
    tok_width = d - MEM_WIDTH
    n_heads = tok_width // HEAD_DIM
    assert ts <= SAMPLE_ROWS

    tm_ffn, tf_ffn = 1024, 512
    tm_proj, tn_proj = 1024, 512
    tm_mix = 256
    ms = bs * SAMPLE_ROWS

    w1_gu, w1_d = ffn1_w_gate_up.astype(BF16), ffn1_w_down.astype(BF16)
    w2_gu, w2_d = ffn2_w_gate_up.astype(BF16), ffn2_w_down.astype(BF16)
    w_kv = mem_w_kv.astype(BF16)
    wp_in, wp_out, gmaps = pool_w_in.astype(BF16), pool_w_out.astype(BF16), pool_group_maps.astype(BF16)
    ws_in, ws_out = sb_w_in.astype(BF16), sb_w_out.astype(BF16)

    xp = x_prompt.reshape(b * t, d)
    xs = jnp.pad(x_sample, ((0, 0), (0, SAMPLE_ROWS - ts), (0, 0))).reshape(ms, d)
    mem2 = mem_prompt.reshape(b * mem_len, d)
    page_flat = page_table.reshape(-1).astype(jnp.int32)
    cache_k = jnp.transpose(cache_sb_k, (0, 1, 3, 2, 4))
    cache_v = jnp.transpose(cache_sb_v, (0, 1, 3, 2, 4))

    pool_p, pool_s, ks, vs, mkp, mvp = [], [], [], [], [], []
    qkv_p = None
    for i in range(depth):
        li = i // N_MIXERS
        xp = _ffn_half(xp, norm_ffn1[i], w1_gu, w1_d, i, tm=tm_ffn, tf=tf_ffn)
        xs = _ffn_half(xs, norm_ffn1[i], w1_gu, w1_d, i, tm=ms, tf=tf_ffn)

        g_mem = norm_mem[i][None]
        flat = lambda n: (n, F32, False)
        mk_p, mv_p = _norm_proj(mem2, g_mem, w_kv, i, 0, (flat(MEM_WIDTH), flat(MEM_WIDTH)), tm=tm_proj, tn=tn_proj)
        mk_p = mk_p.reshape(b, mem_len, MEM_WIDTH)
        mv_p = mv_p.reshape(b, mem_len, MEM_WIDTH)
        mkp.append(mk_p.reshape(b, mem_len, MEM_HEADS, HEAD_DIM))
        mvp.append(mv_p.reshape(b, mem_len, MEM_HEADS, HEAD_DIM))
        mk_s, mv_s = cache_mem_k[i], cache_mem_v[i]

        g_mix = norm_mix[i]
        g_in = g_mix[0:1]
        xp3 = xp.reshape(b, t, d)
        xs3 = xs.reshape(bs, SAMPLE_ROWS, d)
        if i % N_MIXERS == 0:
            scale = pool_scale[li][None]
            (proj_p,) = _norm_proj(xp, g_in, wp_in, li, 0, (flat(d),), tm=tm_proj, tn=2 * tn_proj)
            (proj_s,) = _norm_proj(xs, g_in, wp_in, li, 0, (flat(d),), tm=ms, tn=tn_proj)
            proj_p = proj_p.reshape(b, t, d)
            proj_s = proj_s.reshape(bs, SAMPLE_ROWS, d)
            qm_block = tok_width // MEM_WIDTH
            xp3 = _mixer_out(xp3, g_mix, wp_out, li, proj_p, qm_block, mk_p, mv_p, tm=tm_mix, pool_src=proj_p,
                             halo_src=proj_p, halo_valid_from=1, n_prefix=0, group_maps=gmaps, pool_scale=scale)
            prefix = state_pool[li]
            n_prefix = prefix.shape[1]
            halo_s = jnp.pad(prefix, ((0, 0), (HALO - n_prefix, 0), (0, 0)))
            xs3 = _mixer_out(xs3, g_mix, wp_out, li, proj_s, qm_block, mk_s, mv_s, tm=SAMPLE_ROWS, pool_src=proj_s,
                             halo_src=halo_s, halo_valid_from=0, n_prefix=n_prefix, group_maps=gmaps,
                             pool_scale=scale)
            pool_p.append(proj_p[:, t - POOL_BUF:, :tok_width])
            u_ext = jnp.concatenate([prefix, proj_s[:, :ts, :tok_width]], axis=1)
            pool_s.append(u_ext[:, u_ext.shape[1] - POOL_BUF:])
        else:
            bias = sb_logit_bias[li].astype(F32)
            carried = {} if qkv_p is None else dict(enumerate(qkv_p))
            *qkv_p, qm_p = _norm_proj(
                xp, g_in, ws_in, li, 0,
                ((tok_width, BF16, True), (tok_width, F32, True), (tok_width, F32, True), flat(MEM_WIDTH)),
                tm=tm_proj, tn=tn_proj, seq_len=t, slots=(sb_w_in.shape[0], li, carried))
            o_p = _sb_prompt(*qkv_p, bias, li)
            xp3 = _mixer_out(xp3, g_mix, ws_out, li, qm_p.reshape(b, t, MEM_WIDTH), 0, mk_p, mv_p, tm=tm_mix,
                             tok=o_p)

            q_s, k_s, v_s, qm_s = _norm_proj(
                xs, g_in, ws_in, li, 0, (flat(tok_width), flat(tok_width), flat(tok_width), flat(MEM_WIDTH)),
                tm=ms, tn=tn_proj)
            k_s3 = k_s.reshape(bs, SAMPLE_ROWS, tok_width)
            v_s3 = v_s.reshape(bs, SAMPLE_ROWS, tok_width)
            q_rows = q_s.reshape(bs, SAMPLE_ROWS, n_heads, HEAD_DIM).transpose(0, 2, 1, 3)
            q_rows = q_rows.reshape(bs, n_heads * SAMPLE_ROWS, HEAD_DIM)
            bias_rows = jnp.repeat(bias, SAMPLE_ROWS)[:, None]
            o_rows = _sb_decode(q_rows, k_s3, v_s3, bias_rows, cache_k, cache_v, page_flat, li,
                                pages_per_step=8)
            o_s = o_rows.reshape(bs, n_heads, SAMPLE_ROWS, HEAD_DIM).transpose(0, 2, 1, 3)
            o_s = o_s.reshape(bs, SAMPLE_ROWS, tok_width)
            xs3 = _mixer_out(xs3, g_mix, ws_out, li, qm_s.reshape(bs, SAMPLE_ROWS, MEM_WIDTH), 0, mk_s, mv_s,
                             tm=SAMPLE_ROWS, tok=o_s)
            ks.append(k_s3[:, :ts].reshape(bs, ts, n_heads, HEAD_DIM))
            vs.append(v_s3[:, :ts].reshape(bs, ts, n_heads, HEAD_DIM))
        xp = xp3.reshape(b * t, d)
        xs = xs3.reshape(ms, d)

        xp = _ffn_half(xp, norm_ffn2[i], w2_gu, w2_d, i, tm=tm_ffn, tf=tf_ffn)
        xs = _ffn_half(xs, norm_ffn2[i], w2_gu, w2_d, i, tm=ms, tf=tf_ffn)

    y_p = xp.reshape(b, t, d)
    y_s = xs.reshape(bs, SAMPLE_ROWS, d)[:, :ts]
    k_p, v_p = (jnp.transpose(a, (0, 1, 3, 2, 4)) for a in qkv_p[1:])
    return (y_p, y_s, jnp.stack(pool_p), k_p, v_p, jnp.stack(mkp), jnp.stack(mvp),
            jnp.stack(pool_s), jnp.stack(ks), jnp.stack(vs))
```

```python
import functools

import jax
import jax.numpy as jnp
from jax import lax
from jax.experimental import pallas as pl
from jax.experimental.pallas import tpu as pltpu

F32 = jnp.float32
BF16 = jnp.bfloat16

HEAD_DIM = 128
MEM_HEADS = 4
MEM_WIDTH = MEM_HEADS * HEAD_DIM
POOL_WINDOWS = (2, 4, 8, 16)
POOL_BUF = max(POOL_WINDOWS) - 1
HALO = 16
N_MIXERS = 2
SB_BLOCK = 128
SB_QTILE = 512
SB_KCHUNK = 256
RMS_EPS = 1e-6
SAMPLE_ROWS = 8
LOG2E = 1.4426950408889634
MIB = 1024 * 1024
VMEM_CAP = 62 * MIB

_NT = (((1,), (1,)), ((), ()))


def _params(semantics, vmem_bytes):
    return pltpu.CompilerParams(dimension_semantics=semantics,
                                vmem_limit_bytes=int(min(vmem_bytes, VMEM_CAP)))


def _rms(x, g):
    ms = jnp.mean(x * x, axis=-1, keepdims=True)
    return x * lax.rsqrt(ms + RMS_EPS) * g


def _suffix_matrix(n):
    r = lax.broadcasted_iota(jnp.int32, (n, n), 0)
    c = lax.broadcasted_iota(jnp.int32, (n, n), 1)
    return jnp.where(r > c, 1.0, 0.0).astype(BF16)


def _sb_weights(z2, vis, umat, carry):
    neg_abs = lax.bitcast_convert_type(lax.bitcast_convert_type(z2, jnp.uint32) | jnp.uint32(0x80000000), F32)
    sp = jnp.maximum(z2, 0.0) + jnp.log2(1.0 + jnp.exp2(neg_abs))
    if vis is not None:
        sp = jnp.where(vis, sp, 0.0)
    later = jnp.dot(sp.astype(BF16), umat, preferred_element_type=F32)
    w = jnp.exp2(z2 - sp - later - carry)
    if vis is not None:
        w = jnp.where(vis, w, 0.0)
    return w, carry + jnp.sum(sp, axis=-1, keepdims=True)


def _ffn_kernel(x_ref, g_ref, wg_ref, wu_ref, wd_ref, o_ref, *refs, acc_cols, emit_weights):
    h_ref = refs[-1]
    j = pl.program_id(1)

    @pl.when(j == 0)
    def _():
        h_ref[...] = _rms(x_ref[...], g_ref[0:1, :]).astype(BF16)
        o_ref[...] = jnp.zeros_like(o_ref)

    if emit_weights:
        for src, dst in zip((wg_ref, wu_ref, wd_ref), refs[:3]):
            dst[...] = src[...].astype(BF16)
        wg_ref, wu_ref, wd_ref = refs[:3]
    h = h_ref[...]
    gate = jnp.dot(h, wg_ref[...], preferred_element_type=F32)
    up = jnp.dot(h, wu_ref[...], preferred_element_type=F32)
    a = (gate * jax.nn.sigmoid(gate) * up).astype(BF16)
    for c0 in range(0, o_ref.shape[1], acc_cols):
        cols = slice(c0, c0 + acc_cols)
        o_ref[:, cols] += jnp.dot(a, wd_ref[:, cols], preferred_element_type=F32)

    @pl.when(j == pl.num_programs(1) - 1)
    def _():
        o_ref[...] = x_ref[...] + 0.5 * _rms(o_ref[...], g_ref[1:2, :])


def _ffn_half(x, g, w_gate, w_up, w_down, *, tm, tf):
    m, d = x.shape
    f = w_down.shape[0]
    vmem = 2 * (2 * tm * d * 4) + tm * d * 2 + 2 * 3 * d * tf * 2 + 6 * tm * tf * 4 + 2 * MIB
    return pl.pallas_call(
        functools.partial(_ffn_kernel, acc_cols=min(d, 512), emit_weights=False),
        grid=(m // tm, f // tf),
        in_specs=[pl.BlockSpec((tm, d), lambda i, j: (i, 0)),
                  pl.BlockSpec((2, d), lambda i, j: (0, 0)),
                  pl.BlockSpec((d, tf), lambda i, j: (0, j)),
                  pl.BlockSpec((d, tf), lambda i, j: (0, j)),
                  pl.BlockSpec((tf, d), lambda i, j: (j, 0))],
        out_specs=pl.BlockSpec((tm, d), lambda i, j: (i, 0)),
        out_shape=jax.ShapeDtypeStruct((m, d), F32),
        scratch_shapes=[pltpu.VMEM((tm, d), BF16)],
        compiler_params=_params(("parallel", "arbitrary"), vmem),
        name="ffn_half",
    )(x, g, w_gate, w_up, w_down)


def _ffn_half_casting(x, g, w_gu, w_d, layer, *, tf):
    m, d = x.shape
    f = w_d.shape[1]
    nf = f // tf
    vmem = 4 * m * d * 4 + 2 * 3 * d * tf * (4 + 2) + 3 * d * tf * 2 + 6 * m * tf * 4 + 4 * MIB
    return pl.pallas_call(
        functools.partial(_ffn_kernel, acc_cols=min(d, 512), emit_weights=True),
        grid=(1, nf),
        in_specs=[pl.BlockSpec((m, d), lambda i, j: (0, 0)),
                  pl.BlockSpec((2, d), lambda i, j: (0, 0)),
                  pl.BlockSpec((None, d, tf), lambda i, j: (layer, 0, j)),
                  pl.BlockSpec((None, d, tf), lambda i, j: (layer, 0, j + nf)),
                  pl.BlockSpec((None, tf, d), lambda i, j: (layer, j, 0))],
        out_specs=[pl.BlockSpec((m, d), lambda i, j: (0, 0)),
                   pl.BlockSpec((d, tf), lambda i, j: (0, j)),
                   pl.BlockSpec((d, tf), lambda i, j: (0, j)),
                   pl.BlockSpec((tf, d), lambda i, j: (j, 0))],
        out_shape=[jax.ShapeDtypeStruct((m, d), F32), jax.ShapeDtypeStruct((d, f), BF16),
                   jax.ShapeDtypeStruct((d, f), BF16), jax.ShapeDtypeStruct((f, d), BF16)],
        scratch_shapes=[pltpu.VMEM((m, d), BF16)],
        compiler_params=_params(("arbitrary", "arbitrary"), vmem),
        name="ffn_half_casting",
    )(x, g, w_gu, w_gu, w_d)


def _norm_proj_kernel(x_ref, g_ref, w_ref, *refs, bounds, by_head, n_carried):
    o_refs, h_ref = refs[n_carried:-1], refs[-1]
    j = pl.program_id(1)

    @pl.when(j == 0)
    def _():
        h_ref[...] = _rms(x_ref[...], g_ref[...]).astype(BF16)

    res = jnp.dot(h_ref[...], w_ref[...], preferred_element_type=F32)
    for o_ref, (lo, hi), heads in zip(o_refs, bounds, by_head):
        @pl.when(jnp.logical_and(j >= lo, j < hi))
        def _(o_ref=o_ref, heads=heads):
            out = res.astype(o_ref.dtype)
            if heads:
                for hh in range(o_ref.shape[1]):
                    o_ref[0, hh] = out[:, hh * HEAD_DIM:(hh + 1) * HEAD_DIM]
            else:
                o_ref[...] = out


def _norm_proj(x, g, w, layer, col0, segments, *, tm, tn, seq_len=None, slots=None):
    m, d = x.shape
    assert col0 % tn == 0 and m % tm == 0 and all(n % tn == 0 for n, _, _ in segments)
    cb = col0 // tn
    n_slots, slot, carried = slots if slots is not None else (1, 0, {})
    bounds, lo = [], 0
    for n, _, _ in segments:
        bounds.append((lo, lo + n // tn))
        lo += n // tn
    out_specs, out_shapes, aliases, carried_args = [], [], {}, []
    for si, ((n, dtype, heads), (a, b)) in enumerate(zip(segments, bounds)):
        local = lambda j, a=a, b=b: jnp.clip(j - a, 0, b - a - 1)
        if heads:
            assert seq_len % tm == 0 and tn % HEAD_DIM == 0
            tiles = seq_len // tm
            out_specs.append(pl.BlockSpec((None, 1, tn // HEAD_DIM, tm, HEAD_DIM),
                                          lambda i, j, local=local: (slot, i // tiles, local(j), i % tiles, 0)))
            out_shapes.append(jax.ShapeDtypeStruct((n_slots, m // seq_len, n // HEAD_DIM, seq_len, HEAD_DIM), dtype))
            if si in carried:
                aliases[3 + len(carried_args)] = si
                carried_args.append(carried[si])
        else:
            out_specs.append(pl.BlockSpec((tm, tn), lambda i, j, local=local: (i, local(j))))
            out_shapes.append(jax.ShapeDtypeStruct((m, n), dtype))
    vmem = 2 * tm * d * 4 + tm * d * 2 + 2 * d * tn * 2 + (2 * len(segments) + 2) * tm * tn * 4 + 4 * MIB
    return pl.pallas_call(
        functools.partial(_norm_proj_kernel, bounds=tuple(bounds), by_head=tuple(s[2] for s in segments),
                          n_carried=len(carried_args)),
        grid=(m // tm, lo),
        in_specs=[pl.BlockSpec((tm, d), lambda i, j: (i, 0)),
                  pl.BlockSpec((1, d), lambda i, j: (0, 0)),
                  pl.BlockSpec((None, d, tn), lambda i, j: (layer, 0, j + cb))]
                 + [pl.BlockSpec(memory_space=pl.ANY)] * len(carried_args),
        out_specs=out_specs,
        out_shape=out_shapes,
        input_output_aliases=aliases,
        scratch_shapes=[pltpu.VMEM((tm, d), BF16)],
        compiler_params=_params(("parallel", "arbitrary"), vmem),
        name="norm_proj",
    )(x, g, w, *carried_args)


def _sb_prompt_kernel(bias_ref, q_ref, k_ref, v_ref, o_ref, kb_ref, vb_ref, u_ref):
    tq, kc = SB_QTILE, SB_KCHUNK
    kb_ref[...] = k_ref[0, 0].astype(BF16)
    vb_ref[...] = v_ref[0, 0].astype(BF16)
    u_ref[...] = _suffix_matrix(kc)
    bias = bias_ref[pl.program_id(1)] * LOG2E
    scale = HEAD_DIM ** -0.5 * LOG2E
    r = lax.broadcasted_iota(jnp.int32, (tq, kc), 0)
    c = lax.broadcasted_iota(jnp.int32, (tq, kc), 1)
    for i in range(q_ref.shape[2] // tq):
        q = q_ref[0, 0, i * tq:(i + 1) * tq, :]
        acc = jnp.zeros((tq, HEAD_DIM), F32)
        carry = jnp.zeros((tq, 1), F32)
        for j in range((i + 1) * tq // kc - 1, -1, -1):
            rows = slice(j * kc, (j + 1) * kc)
            vis = (c + (j * kc - i * tq) < r) if (j + 1) * kc > i * tq else None
            z = lax.dot_general(q, kb_ref[rows, :], _NT, preferred_element_type=F32) * scale + bias
            w, carry = _sb_weights(z, vis, u_ref[...], carry)
            acc = acc + jnp.dot(w.astype(BF16), vb_ref[rows, :], preferred_element_type=F32)
        o_ref[0, i * tq:(i + 1) * tq, :] = acc.astype(o_ref.dtype)


def _sb_prompt(q, k, v, bias, slot):
    _, b, nh, t, _ = q.shape
    assert t % SB_QTILE == 0 and SB_QTILE % SB_KCHUNK == 0
    head_spec = pl.BlockSpec((None, 1, 1, t, HEAD_DIM), lambda bb, h: (slot, bb, h, 0, 0))
    return pl.pallas_call(
        _sb_prompt_kernel,
        grid=(b, nh),
        in_specs=[pl.BlockSpec(memory_space=pltpu.SMEM), head_spec, head_spec, head_spec],
        out_specs=pl.BlockSpec((1, t, HEAD_DIM), lambda bb, h: (bb, 0, h)),
        out_shape=jax.ShapeDtypeStruct((b, t, nh * HEAD_DIM), BF16),
        scratch_shapes=[pltpu.VMEM((t, HEAD_DIM), BF16), pltpu.VMEM((t, HEAD_DIM), BF16),
                        pltpu.VMEM((SB_KCHUNK, SB_KCHUNK), BF16)],
        compiler_params=_params(("parallel", "parallel"), 32 * MIB),
        name="sb_prompt",
    )(bias, q, k, v)


def _sb_decode_kernel(pt_ref, q_ref, knew_ref, vnew_ref, bias_ref, *refs, pages_per_step, n_heads):
    del pt_ref
    k_refs = refs[:pages_per_step]
    v_refs = refs[pages_per_step:2 * pages_per_step]
    o_ref, acc_ref, car_ref = refs[2 * pages_per_step:]
    p = pl.program_id(1)
    rows = n_heads * SAMPLE_ROWS
    scale = HEAD_DIM ** -0.5 * LOG2E
    chunk = min(SB_KCHUNK, pages_per_step * SB_BLOCK)
    umat = _suffix_matrix(chunk)
    bias = bias_ref[...] * LOG2E
    q = q_ref[0]
    qh = [q[h * SAMPLE_ROWS:(h + 1) * SAMPLE_ROWS].astype(BF16) for h in range(n_heads)]

    def sweep(keys, values, n_keys, vis):
        zs = [lax.dot_general(qh[h], keys(h), _NT, preferred_element_type=F32) for h in range(n_heads)]
        z = jnp.concatenate(zs, axis=0) * scale + bias
        step = min(chunk, n_keys)
        carry = car_ref[...]
        ws = []
        for c0 in range(n_keys - step, -1, -step):
            w, carry = _sb_weights(z[:, c0:c0 + step], vis, umat[:step, :step], carry)
            ws.insert(0, w)
        w = jnp.concatenate(ws, axis=1)
        outs = [jnp.dot(w[h * SAMPLE_ROWS:(h + 1) * SAMPLE_ROWS].astype(BF16), values(h),
                        preferred_element_type=F32) for h in range(n_heads)]
        acc_ref[...] += jnp.concatenate(outs, axis=0)
        car_ref[...] = carry

    @pl.when(p == 0)
    def _():
        acc_ref[...] = jnp.zeros_like(acc_ref)
        car_ref[...] = jnp.zeros_like(car_ref)
        t = lax.broadcasted_iota(jnp.int32, (rows, SB_BLOCK), 0) % SAMPLE_ROWS
        s = lax.broadcasted_iota(jnp.int32, (rows, SB_BLOCK), 1)
        zeros = jnp.zeros((SB_BLOCK - SAMPLE_ROWS, HEAD_DIM), F32)

        def new_block(ref, h):
            head = ref[0, :, h * HEAD_DIM:(h + 1) * HEAD_DIM]
            return jnp.concatenate([head, zeros], axis=0).astype(BF16)

        sweep(functools.partial(new_block, knew_ref), functools.partial(new_block, vnew_ref), SB_BLOCK, s < t)

    def pages(page_refs, h):
        return jnp.concatenate([r[h].astype(BF16) for r in reversed(page_refs)], axis=0)

    sweep(functools.partial(pages, k_refs), functools.partial(pages, v_refs), pages_per_step * SB_BLOCK, None)

    @pl.when(p == pl.num_programs(1) - 1)
    def _():
        o_ref[0] = acc_ref[...]


def _sb_decode(q_rows, k_new, v_new, bias_rows, cache_k, cache_v, page_flat, layer, *, pages_per_step):
    bsz, rows, _ = q_rows.shape
    n_heads = rows // SAMPLE_ROWS
    width = n_heads * HEAD_DIM
    page = cache_k.shape[3]
    n_pages = page_flat.shape[0] // bsz
    assert page == SB_BLOCK and n_pages % pages_per_step == 0
    steps = n_pages // pages_per_step

    def page_map(c):
        return lambda b, p, pt: (layer, pt[b * n_pages + n_pages - 1 - (p * pages_per_step + c)], 0, 0, 0)

    def page_specs():
        return [pl.BlockSpec((None, None, n_heads, page, HEAD_DIM), page_map(c)) for c in range(pages_per_step)]

    seq_map = lambda b, p, pt: (b, 0, 0)
    grid_spec = pltpu.PrefetchScalarGridSpec(
        num_scalar_prefetch=1,
        grid=(bsz, steps),
        in_specs=[pl.BlockSpec((1, rows, HEAD_DIM), seq_map),
                  pl.BlockSpec((1, SAMPLE_ROWS, width), seq_map),
                  pl.BlockSpec((1, SAMPLE_ROWS, width), seq_map),
                  pl.BlockSpec((rows, 1), lambda b, p, pt: (0, 0))] + page_specs() + page_specs(),
        out_specs=pl.BlockSpec((1, rows, HEAD_DIM), seq_map),
        scratch_shapes=[pltpu.VMEM((rows, HEAD_DIM), F32), pltpu.VMEM((rows, 1), F32)],
    )
    vmem = 2 * 2 * pages_per_step * page * width * 4 + 12 * MIB
    return pl.pallas_call(
        functools.partial(_sb_decode_kernel, pages_per_step=pages_per_step, n_heads=n_heads),
        grid_spec=grid_spec,
        out_shape=jax.ShapeDtypeStruct((bsz, rows, HEAD_DIM), F32),
        compiler_params=_params(("parallel", "arbitrary"), vmem),
        name="sb_decode",
    )(page_flat, q_rows, k_new, v_new, bias_rows, *([cache_k] * pages_per_step), *([cache_v] * pages_per_step))


def _mixer_kernel(*refs, pool, n_prefix, halo_valid_from, mem_by_head):
    if pool:
        (u_ref, halo_ref, qm_ref, mk_ref, mv_ref, x_ref, g_ref, wo_ref, gm_ref, sc_ref,
         o_ref, cat_ref, ubuf_ref) = refs
    else:
        tok_ref, qm_ref, mk_ref, mv_ref, x_ref, g_ref, wo_ref, o_ref, cat_ref = refs
    t = pl.program_id(1)
    tm = x_ref.shape[1]
    tok_width = cat_ref.shape[1] - MEM_WIDTH

    if pool:
        group = tok_width // len(POOL_WINDOWS)
        halo = halo_ref[0]
        ubuf_ref[0:HALO, :] = jnp.where(t >= halo_valid_from, halo, 0.0)
        ubuf_ref[HALO:HALO + tm, :] = u_ref[0]
        pos = t * tm + lax.broadcasted_iota(jnp.int32, (tm, 1), 0) + (n_prefix + 1)
        for gi, win in enumerate(POOL_WINDOWS):
            cols = slice(gi * group, (gi + 1) * group)
            own = ubuf_ref[HALO:HALO + tm, cols]
            tot = own
            for back in range(1, win):
                tot = tot + ubuf_ref[HALO - back:HALO - back + tm, cols]
            count = jnp.minimum(pos, win).astype(F32)
            diff = tot / count - own
            mixed = jnp.dot(diff.astype(BF16), gm_ref[gi], preferred_element_type=F32) * sc_ref[:, cols]
            cat_ref[:, cols] = mixed.astype(BF16)
    else:
        cat_ref[:, 0:tok_width] = tok_ref[0].astype(BF16)

    scale = HEAD_DIM ** -0.5
    if mem_by_head:
        mk_heads = pltpu.einshape("mhd->hmd", mk_ref[0].astype(BF16))
        mv_heads = pltpu.einshape("mhd->hmd", mv_ref[0].astype(BF16))
    for h in range(MEM_HEADS):
        cols = slice(h * HEAD_DIM, (h + 1) * HEAD_DIM)
        qh = qm_ref[0, :, cols].astype(BF16)
        if mem_by_head:
            kh, vh = mk_heads[h], mv_heads[h]
        else:
            kh, vh = mk_ref[0, :, cols].astype(BF16), mv_ref[0, :, cols].astype(BF16)
        s = lax.dot_general(qh, kh, _NT, preferred_element_type=F32) * scale
        e = jnp.exp(s - jnp.max(s, axis=-1, keepdims=True))
        prob = e / jnp.sum(e, axis=-1, keepdims=True)
        oh = jnp.dot(prob.astype(BF16), vh, preferred_element_type=F32)
        cat_ref[:, tok_width + h * HEAD_DIM:tok_width + (h + 1) * HEAD_DIM] = oh.astype(BF16)

    y = jnp.dot(cat_ref[...], wo_ref[...], preferred_element_type=F32)
    o_ref[0] = x_ref[0] + _rms(y, g_ref[1:2, :])


def _mixer_out(x, g, w_out, layer, qm_src, qm_block, mk, mv, *, tm, tok=None, pool_src=None, halo_src=None,
               halo_valid_from=0, n_prefix=0, group_maps=None, pool_scale=None):
    b, t, d = x.shape
    tok_width = d - MEM_WIDTH
    mem_len = mk.shape[1]
    pool = tok is None
    mem_by_head = mk.ndim == 4
    qm_spec = pl.BlockSpec((1, tm, MEM_WIDTH), lambda bb, i: (bb, i, qm_block))
    if mem_by_head:
        mem_spec = pl.BlockSpec((1, mem_len, MEM_HEADS, HEAD_DIM), lambda bb, i: (bb, 0, 0, 0))
    else:
        mem_spec = pl.BlockSpec((1, mem_len, MEM_WIDTH), lambda bb, i: (bb, 0, 0))
    x_spec = pl.BlockSpec((1, tm, d), lambda bb, i: (bb, i, 0))
    g_spec = pl.BlockSpec((2, d), lambda bb, i: (0, 0))
    wo_spec = pl.BlockSpec((None, d, d), lambda bb, i: (layer, 0, 0))
    scratch = [pltpu.VMEM((tm, d), BF16)]
    if pool:
        halo_blocks = tm // HALO
        if halo_src is pool_src:
            halo_map = lambda bb, i: (bb, jnp.maximum(i * halo_blocks - 1, 0), 0)
        else:
            halo_map = lambda bb, i: (bb, 0, 0)
        _, ng, gw, _ = group_maps.shape
        in_specs = [pl.BlockSpec((1, tm, tok_width), lambda bb, i: (bb, i, 0)),
                    pl.BlockSpec((1, HALO, tok_width), halo_map),
                    qm_spec, mem_spec, mem_spec, x_spec, g_spec, wo_spec,
                    pl.BlockSpec((None, ng, gw, gw), lambda bb, i: (layer, 0, 0, 0)),
                    pl.BlockSpec((1, tok_width), lambda bb, i: (0, 0))]
        args = (pool_src, halo_src, qm_src, mk, mv, x, g, w_out, group_maps, pool_scale)
        scratch.append(pltpu.VMEM((HALO + tm, tok_width), F32))
    else:
        in_specs = [pl.BlockSpec((1, tm, tok_width), lambda bb, i: (bb, i, 0)),
                    qm_spec, mem_spec, mem_spec, x_spec, g_spec, wo_spec]
        args = (tok, qm_src, mk, mv, x, g, w_out)
    vmem = (2 * (tm * tok_width * 4 + tm * MEM_WIDTH * 4 + 2 * tm * d * 4 + 4 * mem_len * MEM_WIDTH * 4 + d * d * 2)
            + (HALO + tm) * tok_width * 4 + tm * d * 2 + 3 * tm * d * 4 + 6 * MIB)
    return pl.pallas_call(
        functools.partial(_mixer_kernel, pool=pool, n_prefix=n_prefix, halo_valid_from=halo_valid_from,
                          mem_by_head=mem_by_head),
        grid=(b, t // tm),
        in_specs=in_specs,
        out_specs=x_spec,
        out_shape=jax.ShapeDtypeStruct((b, t, d), F32),
        scratch_shapes=scratch,
        compiler_params=_params(("parallel", "arbitrary"), vmem),
        name="mixer_pool" if pool else "mixer_sb",
    )(*args)


def kernel(x_prompt, x_sample, state_pool, cache_sb_k, cache_sb_v, cache_mem_k, cache_mem_v, page_table, mem_prompt, norm_ffn1, ffn1_w_gate_up, ffn1_w_down, norm_mix, norm_mem, mem_w_kv, pool_w_in, pool_group_maps, pool_scale, pool_w_out, sb_w_in, sb_logit_bias, sb_w_out, norm_ffn2, ffn2_w_gate_up, ffn2_w_down):
    b, t, d = x_prompt.shape
    bs, ts, _ = x_sample.shape
    depth = norm_ffn1.shape[0]
    mem_len = mem_prompt.shape[1]
    tok_width = d - MEM_WIDTH
    n_heads = tok_width // HEAD_DIM
    assert ts <= SAMPLE_ROWS

    tm_ffn, tf_ffn = 1024, 512
    tm_proj, tn_proj = 1024, 512
    tm_mix = 256
    ms = bs * SAMPLE_ROWS

    w_kv = mem_w_kv.astype(BF16)
    wp_in, wp_out, gmaps = pool_w_in.astype(BF16), pool_w_out.astype(BF16), pool_group_maps.astype(BF16)
    ws_in, ws_out = sb_w_in.astype(BF16), sb_w_out.astype(BF16)

    xp = x_prompt.reshape(b * t, d)
    xs = jnp.pad(x_sample, ((0, 0), (0, SAMPLE_ROWS - ts), (0, 0))).reshape(ms, d)
    mem2 = mem_prompt.reshape(b * mem_len, d)
    page_flat = page_table.reshape(-1).astype(jnp.int32)
    cache_k = jnp.transpose(cache_sb_k, (0, 1, 3, 2, 4))
    cache_v = jnp.transpose(cache_sb_v, (0, 1, 3, 2, 4))

    pool_p, pool_s, ks, vs, mkp, mvp = [], [], [], [], [], []
    qkv_p = None
    for i in range(depth):
        li = i // N_MIXERS
        xs, *w_bf = _ffn_half_casting(xs, norm_ffn1[i], ffn1_w_gate_up, ffn1_w_down, i, tf=tf_ffn)
        xp = _ffn_half(xp, norm_ffn1[i], *w_bf, tm=tm_ffn, tf=tf_ffn)

        g_mem = norm_mem[i][None]
        flat = lambda n: (n, F32, False)
        mk_p, mv_p = _norm_proj(mem2, g_mem, w_kv, i, 0, (flat(MEM_WIDTH), flat(MEM_WIDTH)), tm=tm_proj, tn=tn_proj)
        mk_p = mk_p.reshape(b, mem_len, MEM_WIDTH)
        mv_p = mv_p.reshape(b, mem_len, MEM_WIDTH)
        mkp.append(mk_p.reshape(b, mem_len, MEM_HEADS, HEAD_DIM))
        mvp.append(mv_p.reshape(b, mem_len, MEM_HEADS, HEAD_DIM))
        mk_s, mv_s = cache_mem_k[i], cache_mem_v[i]

        g_mix = norm_mix[i]
        g_in = g_mix[0:1]
        xp3 = xp.reshape(b, t, d)
        xs3 = xs.reshape(bs, SAMPLE_ROWS, d)
        if i % N_MIXERS == 0:
            scale = pool_scale[li][None]
            (proj_p,) = _norm_proj(xp, g_in, wp_in, li, 0, (flat(d),), tm=tm_proj, tn=2 * tn_proj)
            (proj_s,) = _norm_proj(xs, g_in, wp_in, li, 0, (flat(d),), tm=ms, tn=tn_proj)
            proj_p = proj_p.reshape(b, t, d)
            proj_s = proj_s.reshape(bs, SAMPLE_ROWS, d)
            qm_block = tok_width // MEM_WIDTH
            xp3 = _mixer_out(xp3, g_mix, wp_out, li, proj_p, qm_block, mk_p, mv_p, tm=tm_mix, pool_src=proj_p,
                             halo_src=proj_p, halo_valid_from=1, n_prefix=0, group_maps=gmaps, pool_scale=scale)
            prefix = state_pool[li]
            n_prefix = prefix.shape[1]
            halo_s = jnp.pad(prefix, ((0, 0), (HALO - n_prefix, 0), (0, 0)))
            xs3 = _mixer_out(xs3, g_mix, wp_out, li, proj_s, qm_block, mk_s, mv_s, tm=SAMPLE_ROWS, pool_src=proj_s,
                             halo_src=halo_s, halo_valid_from=0, n_prefix=n_prefix, group_maps=gmaps,
                             pool_scale=scale)
            pool_p.append(proj_p[:, t - POOL_BUF:, :tok_width])
            u_ext = jnp.concatenate([prefix, proj_s[:, :ts, :tok_width]], axis=1)
            pool_s.append(u_ext[:, u_ext.shape[1] - POOL_BUF:])
        else:
            bias = sb_logit_bias[li].astype(F32)
            carried = {} if qkv_p is None else dict(enumerate(qkv_p))
            *qkv_p, qm_p = _norm_proj(
                xp, g_in, ws_in, li, 0,
                ((tok_width, BF16, True), (tok_width, F32, True), (tok_width, F32, True), flat(MEM_WIDTH)),
                tm=tm_proj, tn=tn_proj, seq_len=t, slots=(sb_w_in.shape[0], li, carried))
            o_p = _sb_prompt(*qkv_p, bias, li)
            xp3 = _mixer_out(xp3, g_mix, ws_out, li, qm_p.reshape(b, t, MEM_WIDTH), 0, mk_p, mv_p, tm=tm_mix,
                             tok=o_p)

            q_s, k_s, v_s, qm_s = _norm_proj(
                xs, g_in, ws_in, li, 0, (flat(tok_width), flat(tok_width), flat(tok_width), flat(MEM_WIDTH)),
                tm=ms, tn=tn_proj)
            k_s3 = k_s.reshape(bs, SAMPLE_ROWS, tok_width)
            v_s3 = v_s.reshape(bs, SAMPLE_ROWS, tok_width)
            q_rows = q_s.reshape(bs, SAMPLE_ROWS, n_heads, HEAD_DIM).transpose(0, 2, 1, 3)
            q_rows = q_rows.reshape(bs, n_heads * SAMPLE_ROWS, HEAD_DIM)
            bias_rows = jnp.repeat(bias, SAMPLE_ROWS)[:, None]
            o_rows = _sb_decode(q_rows, k_s3, v_s3, bias_rows, cache_k, cache_v, page_flat, li,
                                pages_per_step=8)
            o_s = o_rows.reshape(bs, n_heads, SAMPLE_ROWS, HEAD_DIM).transpose(0, 2, 1, 3)
            o_s = o_s.reshape(bs, SAMPLE_ROWS, tok_width)
            xs3 = _mixer_out(xs3, g_mix, ws_out, li, qm_s.reshape(bs, SAMPLE_ROWS, MEM_WIDTH), 0, mk_s, mv_s,
                             tm=SAMPLE_ROWS, tok=o_s)
            ks.append(k_s3[:, :ts].reshape(bs, ts, n_heads, HEAD_DIM))
            vs.append(v_s3[:, :ts].reshape(bs, ts, n_heads, HEAD_DIM))
        xp = xp3.reshape(b * t, d)
        xs = xs3.reshape(ms, d)

        xs, *w_bf = _ffn_half_casting(xs, norm_ffn2[i], ffn2_w_gate_up, ffn2_w_down, i, tf=tf_ffn)
        xp = _ffn_half(xp, norm_ffn2[i], *w_bf, tm=tm_ffn, tf=tf_ffn)

    y_p = xp.reshape(b, t, d)
    y_s = xs.reshape(bs, SAMPLE_ROWS, d)[:, :ts]
    k_p, v_p = (jnp.transpose(a, (0, 1, 3, 2, 4)) for a in qkv_p[1:])
    return (y_p, y_s, jnp.stack(pool_p), k_p, v_p, jnp.stack(mkp), jnp.stack(mvp),
            jnp.stack(pool_s), jnp.stack(ks), jnp.stack(vs))
```

```python
import functools

import jax
import jax.numpy as jnp
from jax import lax
from jax.experimental import pallas as pl
from jax.experimental.pallas import tpu as pltpu

F32 = jnp.float32
BF16 = jnp.bfloat16

HEAD_DIM = 128
MEM_HEADS = 4
MEM_WIDTH = MEM_HEADS * HEAD_DIM
POOL_WINDOWS = (2, 4, 8, 16)
POOL_BUF = max(POOL_WINDOWS) - 1
HALO = 16
N_MIXERS = 2
SB_BLOCK = 128
SB_QTILE = 512
SB_KCHUNK = 256
RMS_EPS = 1e-6
SAMPLE_ROWS = 8
LOG2E = 1.4426950408889634
MIB = 1024 * 1024
VMEM_CAP = 62 * MIB

_NT = (((1,), (1,)), ((), ()))


def _params(semantics, vmem_bytes):
    return pltpu.CompilerParams(dimension_semantics=semantics,
                                vmem_limit_bytes=int(min(vmem_bytes, VMEM_CAP)))


def _rms(x, g):
    ms = jnp.mean(x * x, axis=-1, keepdims=True)
    return x * lax.rsqrt(ms + RMS_EPS) * g


def _rms_rows(src_ref, gain, dst_ref, residual_ref=None, zero_ref=None):
    chunk = 16
    group = min(8, src_ref.shape[0] // chunk)
    assert src_ref.shape[0] % (group * chunk) == 0

    def body(c, carry):
        for s in range(group):
            rows = pl.ds(pl.multiple_of((c * group + s) * chunk, chunk), chunk)
            y = _rms(src_ref[rows, :], gain)
            if residual_ref is not None:
                y = residual_ref[rows, :] + y
            dst_ref[rows, :] = y.astype(dst_ref.dtype)
            if zero_ref is not None:
                zero_ref[rows, :] = jnp.zeros((chunk, zero_ref.shape[1]), zero_ref.dtype)
        return carry

    lax.fori_loop(0, src_ref.shape[0] // (group * chunk), body, 0)


def _suffix_matrix(n):
    r = lax.broadcasted_iota(jnp.int32, (n, n), 0)
    c = lax.broadcasted_iota(jnp.int32, (n, n), 1)
    return jnp.where(r > c, 1.0, 0.0).astype(BF16)


def _sb_weights(z2, vis, umat, carry):
    neg_abs = lax.bitcast_convert_type(lax.bitcast_convert_type(z2, jnp.uint32) | jnp.uint32(0x80000000), F32)
    sp = jnp.maximum(z2, 0.0) + jnp.log2(1.0 + jnp.exp2(neg_abs))
    if vis is not None:
        sp = jnp.where(vis, sp, 0.0)
    later = jnp.dot(sp.astype(BF16), umat, preferred_element_type=F32)
    w = jnp.exp2(z2 - sp - later - carry)
    if vis is not None:
        w = jnp.where(vis, w, 0.0)
    return w, carry + jnp.sum(sp, axis=-1, keepdims=True)


def _ffn_kernel(x_ref, g_ref, wg_ref, wu_ref, wd_ref, o_ref, *refs, acc_cols, emit_weights):
    h_ref, acc_ref = refs[-2:]
    j = pl.program_id(1)

    @pl.when(j == 0)
    def _():
        _rms_rows(x_ref, g_ref[0:1, :], h_ref, zero_ref=acc_ref)

    if emit_weights:
        for src, dst in zip((wg_ref, wu_ref, wd_ref), refs[:3]):
            dst[...] = src[...].astype(BF16)
        wg_ref, wu_ref, wd_ref = refs[:3]
    h = h_ref[...]
    gate = jnp.dot(h, wg_ref[...], preferred_element_type=F32)
    up = jnp.dot(h, wu_ref[...], preferred_element_type=F32)
    a = (gate * jax.nn.sigmoid(gate) * up).astype(BF16)
    for c0 in range(0, acc_ref.shape[1], acc_cols):
        cols = slice(c0, c0 + acc_cols)
        acc_ref[:, cols] += jnp.dot(a, wd_ref[:, cols], preferred_element_type=F32)

    @pl.when(j == pl.num_programs(1) - 1)
    def _():
        _rms_rows(acc_ref, 0.5 * g_ref[1:2, :], o_ref, residual_ref=x_ref)


def _ffn_half(x, g, w_gate, w_up, w_down, *, tm, tf):
    m, d = x.shape
    f = w_down.shape[0]
    vmem = 2 * (2 * tm * d * 4) + tm * d * (2 + 4) + 2 * 3 * d * tf * 2 + 2 * tm * tf * 4 + 2 * MIB
    return pl.pallas_call(
        functools.partial(_ffn_kernel, acc_cols=min(d, 512), emit_weights=False),
        grid=(m // tm, f // tf),
        in_specs=[pl.BlockSpec((tm, d), lambda i, j: (i, 0)),
                  pl.BlockSpec((2, d), lambda i, j: (0, 0)),
                  pl.BlockSpec((d, tf), lambda i, j: (0, j)),
                  pl.BlockSpec((d, tf), lambda i, j: (0, j)),
                  pl.BlockSpec((tf, d), lambda i, j: (j, 0))],
        out_specs=pl.BlockSpec((tm, d), lambda i, j: (i, 0)),
        out_shape=jax.ShapeDtypeStruct((m, d), F32),
        scratch_shapes=[pltpu.VMEM((tm, d), BF16), pltpu.VMEM((tm, d), F32)],
        compiler_params=_params(("parallel", "arbitrary"), vmem),
        name="ffn_half",
    )(x, g, w_gate, w_up, w_down)


def _ffn_half_casting(x, g, w_gu, w_d, layer, *, tf):
    m, d = x.shape
    f = w_d.shape[1]
    nf = f // tf
    vmem = 4 * m * d * 4 + 2 * 3 * d * tf * (4 + 2) + 3 * d * tf * 2 + 6 * m * tf * 4 + 4 * MIB
    return pl.pallas_call(
        functools.partial(_ffn_kernel, acc_cols=min(d, 512), emit_weights=True),
        grid=(1, nf),
        in_specs=[pl.BlockSpec((m, d), lambda i, j: (0, 0)),
                  pl.BlockSpec((2, d), lambda i, j: (0, 0)),
                  pl.BlockSpec((None, d, tf), lambda i, j: (layer, 0, j)),
                  pl.BlockSpec((None, d, tf), lambda i, j: (layer, 0, j + nf)),
                  pl.BlockSpec((None, tf, d), lambda i, j: (layer, j, 0))],
        out_specs=[pl.BlockSpec((m, d), lambda i, j: (0, 0)),
                   pl.BlockSpec((d, tf), lambda i, j: (0, j)),
                   pl.BlockSpec((d, tf), lambda i, j: (0, j)),
                   pl.BlockSpec((tf, d), lambda i, j: (j, 0))],
        out_shape=[jax.ShapeDtypeStruct((m, d), F32), jax.ShapeDtypeStruct((d, f), BF16),
                   jax.ShapeDtypeStruct((d, f), BF16), jax.ShapeDtypeStruct((f, d), BF16)],
        scratch_shapes=[pltpu.VMEM((m, d), BF16), pltpu.VMEM((m, d), F32)],
        compiler_params=_params(("arbitrary", "arbitrary"), vmem),
        name="ffn_half_casting",
    )(x, g, w_gu, w_gu, w_d)


def _norm_proj_kernel(x_ref, g_ref, w_ref, *refs, bounds, by_head, n_carried):
    o_refs, h_ref = refs[n_carried:-1], refs[-1]
    j = pl.program_id(1)

    @pl.when(j == 0)
    def _():
        h_ref[...] = _rms(x_ref[...], g_ref[...]).astype(BF16)

    res = jnp.dot(h_ref[...], w_ref[...], preferred_element_type=F32)
    for o_ref, (lo, hi), heads in zip(o_refs, bounds, by_head):
        @pl.when(jnp.logical_and(j >= lo, j < hi))
        def _(o_ref=o_ref, heads=heads):
            out = res.astype(o_ref.dtype)
            if heads:
                for hh in range(o_ref.shape[1]):
                    o_ref[0, hh] = out[:, hh * HEAD_DIM:(hh + 1) * HEAD_DIM]
            else:
                o_ref[...] = out


def _norm_proj(x, g, w, layer, col0, segments, *, tm, tn, seq_len=None, slots=None):
    m, d = x.shape
    assert col0 % tn == 0 and m % tm == 0 and all(n % tn == 0 for n, _, _ in segments)
    cb = col0 // tn
    n_slots, slot, carried = slots if slots is not None else (1, 0, {})
    bounds, lo = [], 0
    for n, _, _ in segments:
        bounds.append((lo, lo + n // tn))
        lo += n // tn
    out_specs, out_shapes, aliases, carried_args = [], [], {}, []
    for si, ((n, dtype, heads), (a, b)) in enumerate(zip(segments, bounds)):
        local = lambda j, a=a, b=b: jnp.clip(j - a, 0, b - a - 1)
        if heads:
            assert seq_len % tm == 0 and tn % HEAD_DIM == 0
            tiles = seq_len // tm
            out_specs.append(pl.BlockSpec((None, 1, tn // HEAD_DIM, tm, HEAD_DIM),
                                          lambda i, j, local=local: (slot, i // tiles, local(j), i % tiles, 0)))
            out_shapes.append(jax.ShapeDtypeStruct((n_slots, m // seq_len, n // HEAD_DIM, seq_len, HEAD_DIM), dtype))
            if si in carried:
                aliases[3 + len(carried_args)] = si
                carried_args.append(carried[si])
        else:
            out_specs.append(pl.BlockSpec((tm, tn), lambda i, j, local=local: (i, local(j))))
            out_shapes.append(jax.ShapeDtypeStruct((m, n), dtype))
    vmem = 2 * tm * d * 4 + tm * d * 2 + 2 * d * tn * 2 + (2 * len(segments) + 2) * tm * tn * 4 + 4 * MIB
    return pl.pallas_call(
        functools.partial(_norm_proj_kernel, bounds=tuple(bounds), by_head=tuple(s[2] for s in segments),
                          n_carried=len(carried_args)),
        grid=(m // tm, lo),
        in_specs=[pl.BlockSpec((tm, d), lambda i, j: (i, 0)),
                  pl.BlockSpec((1, d), lambda i, j: (0, 0)),
                  pl.BlockSpec((None, d, tn), lambda i, j: (layer, 0, j + cb))]
                 + [pl.BlockSpec(memory_space=pl.ANY)] * len(carried_args),
        out_specs=out_specs,
        out_shape=out_shapes,
        input_output_aliases=aliases,
        scratch_shapes=[pltpu.VMEM((tm, d), BF16)],
        compiler_params=_params(("parallel", "arbitrary"), vmem),
        name="norm_proj",
    )(x, g, w, *carried_args)


def _sb_prompt_kernel(bias_ref, q_ref, k_ref, v_ref, o_ref, kb_ref, vb_ref, u_ref):
    tq, kc = SB_QTILE, SB_KCHUNK
    kb_ref[...] = k_ref[0, 0].astype(BF16)
    vb_ref[...] = v_ref[0, 0].astype(BF16)
    u_ref[...] = _suffix_matrix(kc)
    bias = bias_ref[pl.program_id(1)] * LOG2E
    scale = HEAD_DIM ** -0.5 * LOG2E
    r = lax.broadcasted_iota(jnp.int32, (tq, kc), 0)
    c = lax.broadcasted_iota(jnp.int32, (tq, kc), 1)
    for i in range(q_ref.shape[2] // tq):
        q = q_ref[0, 0, i * tq:(i + 1) * tq, :]
        acc = jnp.zeros((tq, HEAD_DIM), F32)
        carry = jnp.zeros((tq, 1), F32)
        for j in range((i + 1) * tq // kc - 1, -1, -1):
            rows = slice(j * kc, (j + 1) * kc)
            vis = (c + (j * kc - i * tq) < r) if (j + 1) * kc > i * tq else None
            z = lax.dot_general(q, kb_ref[rows, :], _NT, preferred_element_type=F32) * scale + bias
            w, carry = _sb_weights(z, vis, u_ref[...], carry)
            acc = acc + jnp.dot(w.astype(BF16), vb_ref[rows, :], preferred_element_type=F32)
        o_ref[0, i * tq:(i + 1) * tq, :] = acc.astype(o_ref.dtype)


def _sb_prompt(q, k, v, bias, slot):
    _, b, nh, t, _ = q.shape
    assert t % SB_QTILE == 0 and SB_QTILE % SB_KCHUNK == 0
    head_spec = pl.BlockSpec((None, 1, 1, t, HEAD_DIM), lambda bb, h: (slot, bb, h, 0, 0))
    return pl.pallas_call(
        _sb_prompt_kernel,
        grid=(b, nh),
        in_specs=[pl.BlockSpec(memory_space=pltpu.SMEM), head_spec, head_spec, head_spec],
        out_specs=pl.BlockSpec((1, t, HEAD_DIM), lambda bb, h: (bb, 0, h)),
        out_shape=jax.ShapeDtypeStruct((b, t, nh * HEAD_DIM), BF16),
        scratch_shapes=[pltpu.VMEM((t, HEAD_DIM), BF16), pltpu.VMEM((t, HEAD_DIM), BF16),
                        pltpu.VMEM((SB_KCHUNK, SB_KCHUNK), BF16)],
        compiler_params=_params(("parallel", "parallel"), 32 * MIB),
        name="sb_prompt",
    )(bias, q, k, v)


def _sb_decode_kernel(pt_ref, q_ref, knew_ref, vnew_ref, bias_ref, *refs, pages_per_step, n_heads):
    del pt_ref
    k_refs = refs[:pages_per_step]
    v_refs = refs[pages_per_step:2 * pages_per_step]
    o_ref, acc_ref, car_ref = refs[2 * pages_per_step:]
    p = pl.program_id(1)
    rows = n_heads * SAMPLE_ROWS
    scale = HEAD_DIM ** -0.5 * LOG2E
    chunk = min(SB_KCHUNK, pages_per_step * SB_BLOCK)
    umat = _suffix_matrix(chunk)
    bias = bias_ref[...] * LOG2E
    q = q_ref[0]
    qh = [q[h * SAMPLE_ROWS:(h + 1) * SAMPLE_ROWS].astype(BF16) for h in range(n_heads)]

    def sweep(keys, values, n_keys, vis):
        zs = [lax.dot_general(qh[h], keys(h), _NT, preferred_element_type=F32) for h in range(n_heads)]
        z = jnp.concatenate(zs, axis=0) * scale + bias
        step = min(chunk, n_keys)
        carry = car_ref[...]
        ws = []
        for c0 in range(n_keys - step, -1, -step):
            w, carry = _sb_weights(z[:, c0:c0 + step], vis, umat[:step, :step], carry)
            ws.insert(0, w)
        w = jnp.concatenate(ws, axis=1)
        outs = [jnp.dot(w[h * SAMPLE_ROWS:(h + 1) * SAMPLE_ROWS].astype(BF16), values(h),
                        preferred_element_type=F32) for h in range(n_heads)]
        acc_ref[...] += jnp.concatenate(outs, axis=0)
        car_ref[...] = carry

    @pl.when(p == 0)
    def _():
        acc_ref[...] = jnp.zeros_like(acc_ref)
        car_ref[...] = jnp.zeros_like(car_ref)
        t = lax.broadcasted_iota(jnp.int32, (rows, SB_BLOCK), 0) % SAMPLE_ROWS
        s = lax.broadcasted_iota(jnp.int32, (rows, SB_BLOCK), 1)
        zeros = jnp.zeros((SB_BLOCK - SAMPLE_ROWS, HEAD_DIM), F32)

        def new_block(ref, h):
            head = ref[0, :, h * HEAD_DIM:(h + 1) * HEAD_DIM]
            return jnp.concatenate([head, zeros], axis=0).astype(BF16)

        sweep(functools.partial(new_block, knew_ref), functools.partial(new_block, vnew_ref), SB_BLOCK, s < t)

    def pages(page_refs, h):
        return jnp.concatenate([r[h].astype(BF16) for r in reversed(page_refs)], axis=0)

    sweep(functools.partial(pages, k_refs), functools.partial(pages, v_refs), pages_per_step * SB_BLOCK, None)

    @pl.when(p == pl.num_programs(1) - 1)
    def _():
        o_ref[0] = acc_ref[...]


def _sb_decode(q_rows, k_new, v_new, bias_rows, cache_k, cache_v, page_flat, layer, *, pages_per_step):
    bsz, rows, _ = q_rows.shape
    n_heads = rows // SAMPLE_ROWS
    width = n_heads * HEAD_DIM
    page = cache_k.shape[3]
    n_pages = page_flat.shape[0] // bsz
    assert page == SB_BLOCK and n_pages % pages_per_step == 0
    steps = n_pages // pages_per_step

    def page_map(c):
        return lambda b, p, pt: (layer, pt[b * n_pages + n_pages - 1 - (p * pages_per_step + c)], 0, 0, 0)

    def page_specs():
        return [pl.BlockSpec((None, None, n_heads, page, HEAD_DIM), page_map(c)) for c in range(pages_per_step)]

    seq_map = lambda b, p, pt: (b, 0, 0)
    grid_spec = pltpu.PrefetchScalarGridSpec(
        num_scalar_prefetch=1,
        grid=(bsz, steps),
        in_specs=[pl.BlockSpec((1, rows, HEAD_DIM), seq_map),
                  pl.BlockSpec((1, SAMPLE_ROWS, width), seq_map),
                  pl.BlockSpec((1, SAMPLE_ROWS, width), seq_map),
                  pl.BlockSpec((rows, 1), lambda b, p, pt: (0, 0))] + page_specs() + page_specs(),
        out_specs=pl.BlockSpec((1, rows, HEAD_DIM), seq_map),
        scratch_shapes=[pltpu.VMEM((rows, HEAD_DIM), F32), pltpu.VMEM((rows, 1), F32)],
    )
    vmem = 2 * 2 * pages_per_step * page * width * 4 + 12 * MIB
    return pl.pallas_call(
        functools.partial(_sb_decode_kernel, pages_per_step=pages_per_step, n_heads=n_heads),
        grid_spec=grid_spec,
        out_shape=jax.ShapeDtypeStruct((bsz, rows, HEAD_DIM), F32),
        compiler_params=_params(("parallel", "arbitrary"), vmem),
        name="sb_decode",
    )(page_flat, q_rows, k_new, v_new, bias_rows, *([cache_k] * pages_per_step), *([cache_v] * pages_per_step))


def _mixer_kernel(*refs, pool, n_prefix, halo_valid_from, mem_by_head):
    if pool:
        (u_ref, halo_ref, qm_ref, mk_ref, mv_ref, x_ref, g_ref, wo_ref, gm_ref, sc_ref,
         o_ref, cat_ref, ubuf_ref) = refs
    else:
        tok_ref, qm_ref, mk_ref, mv_ref, x_ref, g_ref, wo_ref, o_ref, cat_ref = refs
    t = pl.program_id(1)
    tm = x_ref.shape[1]
    tok_width = cat_ref.shape[1] - MEM_WIDTH

    if pool:
        group = tok_width // len(POOL_WINDOWS)
        halo = halo_ref[0]
        ubuf_ref[0:HALO, :] = jnp.where(t >= halo_valid_from, halo, 0.0)
        ubuf_ref[HALO:HALO + tm, :] = u_ref[0]
        pos = t * tm + lax.broadcasted_iota(jnp.int32, (tm, 1), 0) + (n_prefix + 1)
        for gi, win in enumerate(POOL_WINDOWS):
            cols = slice(gi * group, (gi + 1) * group)
            own = ubuf_ref[HALO:HALO + tm, cols]
            tot = own
            for back in range(1, win):
                tot = tot + ubuf_ref[HALO - back:HALO - back + tm, cols]
            count = jnp.minimum(pos, win).astype(F32)
            diff = tot / count - own
            mixed = jnp.dot(diff.astype(BF16), gm_ref[gi], preferred_element_type=F32) * sc_ref[:, cols]
            cat_ref[:, cols] = mixed.astype(BF16)
    else:
        cat_ref[:, 0:tok_width] = tok_ref[0].astype(BF16)

    scale = HEAD_DIM ** -0.5
    if mem_by_head:
        mk_heads = pltpu.einshape("mhd->hmd", mk_ref[0].astype(BF16))
        mv_heads = pltpu.einshape("mhd->hmd", mv_ref[0].astype(BF16))
    for h in range(MEM_HEADS):
        cols = slice(h * HEAD_DIM, (h + 1) * HEAD_DIM)
        qh = qm_ref[0, :, cols].astype(BF16)
        if mem_by_head:
            kh, vh = mk_heads[h], mv_heads[h]
        else:
            kh, vh = mk_ref[0, :, cols].astype(BF16), mv_ref[0, :, cols].astype(BF16)
        s = lax.dot_general(qh, kh, _NT, preferred_element_type=F32) * scale
        e = jnp.exp(s - jnp.max(s, axis=-1, keepdims=True))
        prob = e / jnp.sum(e, axis=-1, keepdims=True)
        oh = jnp.dot(prob.astype(BF16), vh, preferred_element_type=F32)
        cat_ref[:, tok_width + h * HEAD_DIM:tok_width + (h + 1) * HEAD_DIM] = oh.astype(BF16)

    y = jnp.dot(cat_ref[...], wo_ref[...], preferred_element_type=F32)
    o_ref[0] = x_ref[0] + _rms(y, g_ref[1:2, :])


def _mixer_out(x, g, w_out, layer, qm_src, qm_block, mk, mv, *, tm, tok=None, pool_src=None, halo_src=None,
               halo_valid_from=0, n_prefix=0, group_maps=None, pool_scale=None):
    b, t, d = x.shape
    tok_width = d - MEM_WIDTH
    mem_len = mk.shape[1]
    pool = tok is None
    mem_by_head = mk.ndim == 4
    qm_spec = pl.BlockSpec((1, tm, MEM_WIDTH), lambda bb, i: (bb, i, qm_block))
    if mem_by_head:
        mem_spec = pl.BlockSpec((1, mem_len, MEM_HEADS, HEAD_DIM), lambda bb, i: (bb, 0, 0, 0))
    else:
        mem_spec = pl.BlockSpec((1, mem_len, MEM_WIDTH), lambda bb, i: (bb, 0, 0))
    x_spec = pl.BlockSpec((1, tm, d), lambda bb, i: (bb, i, 0))
    g_spec = pl.BlockSpec((2, d), lambda bb, i: (0, 0))
    wo_spec = pl.BlockSpec((None, d, d), lambda bb, i: (layer, 0, 0))
    scratch = [pltpu.VMEM((tm, d), BF16)]
    if pool:
        halo_blocks = tm // HALO
        if halo_src is pool_src:
            halo_map = lambda bb, i: (bb, jnp.maximum(i * halo_blocks - 1, 0), 0)
        else:
            halo_map = lambda bb, i: (bb, 0, 0)
        _, ng, gw, _ = group_maps.shape
        in_specs = [pl.BlockSpec((1, tm, tok_width), lambda bb, i: (bb, i, 0)),
                    pl.BlockSpec((1, HALO, tok_width), halo_map),
                    qm_spec, mem_spec, mem_spec, x_spec, g_spec, wo_spec,
                    pl.BlockSpec((None, ng, gw, gw), lambda bb, i: (layer, 0, 0, 0)),
                    pl.BlockSpec((1, tok_width), lambda bb, i: (0, 0))]
        args = (pool_src, halo_src, qm_src, mk, mv, x, g, w_out, group_maps, pool_scale)
        scratch.append(pltpu.VMEM((HALO + tm, tok_width), F32))
    else:
        in_specs = [pl.BlockSpec((1, tm, tok_width), lambda bb, i: (bb, i, 0)),
                    qm_spec, mem_spec, mem_spec, x_spec, g_spec, wo_spec]
        args = (tok, qm_src, mk, mv, x, g, w_out)
    vmem = (2 * (tm * tok_width * 4 + tm * MEM_WIDTH * 4 + 2 * tm * d * 4 + 4 * mem_len * MEM_WIDTH * 4 + d * d * 2)
            + (HALO + tm) * tok_width * 4 + tm * d * 2 + 3 * tm * d * 4 + 6 * MIB)
    return pl.pallas_call(
        functools.partial(_mixer_kernel, pool=pool, n_prefix=n_prefix, halo_valid_from=halo_valid_from,
                          mem_by_head=mem_by_head),
        grid=(b, t // tm),
        in_specs=in_specs,
        out_specs=x_spec,
        out_shape=jax.ShapeDtypeStruct((b, t, d), F32),
        scratch_shapes=scratch,
        compiler_params=_params(("parallel", "arbitrary"), vmem),
        name="mixer_pool" if pool else "mixer_sb",
    )(*args)


def kernel(x_prompt, x_sample, state_pool, cache_sb_k, cache_sb_v, cache_mem_k, cache_mem_v, page_table, mem_prompt, norm_ffn1, ffn1_w_gate_up, ffn1_w_down, norm_mix, norm_mem, mem_w_kv, pool_w_in, pool_group_maps, pool_scale, pool_w_out, sb_w_in, sb_logit_bias, sb_w_out, norm_ffn2, ffn2_w_gate_up, ffn2_w_down):
    b, t, d = x_prompt.shape
    bs, ts, _ = x_sample.shape
    depth = norm_ffn1.shape[0]
    mem_len = mem_prompt.shape[1]
    tok_width = d - MEM_WIDTH
    n_heads = tok_width // HEAD_DIM
    assert ts <= SAMPLE_ROWS

    tm_ffn, tf_ffn = 1024, 512
    tm_proj, tn_proj = 1024, 512
    tm_mix = 512
    ms = bs * SAMPLE_ROWS

    w_kv = mem_w_kv.astype(BF16)
    wp_in, wp_out, gmaps = pool_w_in.astype(BF16), pool_w_out.astype(BF16), pool_group_maps.astype(BF16)
    ws_in, ws_out = sb_w_in.astype(BF16), sb_w_out.astype(BF16)

    xp = x_prompt.reshape(b * t, d)
    xs = jnp.pad(x_sample, ((0, 0), (0, SAMPLE_ROWS - ts), (0, 0))).reshape(ms, d)
    mem2 = mem_prompt.reshape(b * mem_len, d)
    page_flat = page_table.reshape(-1).astype(jnp.int32)
    cache_k = jnp.transpose(cache_sb_k, (0, 1, 3, 2, 4))
    cache_v = jnp.transpose(cache_sb_v, (0, 1, 3, 2, 4))

    pool_p, pool_s, ks, vs, mkp, mvp = [], [], [], [], [], []
    qkv_p = None
    for i in range(depth):
        li = i // N_MIXERS
        xs, *w_bf = _ffn_half_casting(xs, norm_ffn1[i], ffn1_w_gate_up, ffn1_w_down, i, tf=tf_ffn)
        xp = _ffn_half(xp, norm_ffn1[i], *w_bf, tm=tm_ffn, tf=tf_ffn)

        g_mem = norm_mem[i][None]
        flat = lambda n: (n, F32, False)
        mk_p, mv_p = _norm_proj(mem2, g_mem, w_kv, i, 0, (flat(MEM_WIDTH), flat(MEM_WIDTH)), tm=tm_proj, tn=tn_proj)
        mk_p = mk_p.reshape(b, mem_len, MEM_WIDTH)
        mv_p = mv_p.reshape(b, mem_len, MEM_WIDTH)
        mkp.append(mk_p.reshape(b, mem_len, MEM_HEADS, HEAD_DIM))
        mvp.append(mv_p.reshape(b, mem_len, MEM_HEADS, HEAD_DIM))
        mk_s, mv_s = cache_mem_k[i], cache_mem_v[i]

        g_mix = norm_mix[i]
        g_in = g_mix[0:1]
        xp3 = xp.reshape(b, t, d)
        xs3 = xs.reshape(bs, SAMPLE_ROWS, d)
        if i % N_MIXERS == 0:
            scale = pool_scale[li][None]
            (proj_p,) = _norm_proj(xp, g_in, wp_in, li, 0, (flat(d),), tm=tm_proj, tn=2 * tn_proj)
            (proj_s,) = _norm_proj(xs, g_in, wp_in, li, 0, (flat(d),), tm=ms, tn=tn_proj)
            proj_p = proj_p.reshape(b, t, d)
            proj_s = proj_s.reshape(bs, SAMPLE_ROWS, d)
            qm_block = tok_width // MEM_WIDTH
            xp3 = _mixer_out(xp3, g_mix, wp_out, li, proj_p, qm_block, mk_p, mv_p, tm=tm_mix, pool_src=proj_p,
                             halo_src=proj_p, halo_valid_from=1, n_prefix=0, group_maps=gmaps, pool_scale=scale)
            prefix = state_pool[li]
            n_prefix = prefix.shape[1]
            halo_s = jnp.pad(prefix, ((0, 0), (HALO - n_prefix, 0), (0, 0)))
            xs3 = _mixer_out(xs3, g_mix, wp_out, li, proj_s, qm_block, mk_s, mv_s, tm=SAMPLE_ROWS, pool_src=proj_s,
                             halo_src=halo_s, halo_valid_from=0, n_prefix=n_prefix, group_maps=gmaps,
                             pool_scale=scale)
            pool_p.append(proj_p[:, t - POOL_BUF:, :tok_width])
            u_ext = jnp.concatenate([prefix, proj_s[:, :ts, :tok_width]], axis=1)
            pool_s.append(u_ext[:, u_ext.shape[1] - POOL_BUF:])
        else:
            bias = sb_logit_bias[li].astype(F32)
            carried = {} if qkv_p is None else dict(enumerate(qkv_p))
            *qkv_p, qm_p = _norm_proj(
                xp, g_in, ws_in, li, 0,
                ((tok_width, BF16, True), (tok_width, F32, True), (tok_width, F32, True), flat(MEM_WIDTH)),
                tm=tm_proj, tn=tn_proj, seq_len=t, slots=(sb_w_in.shape[0], li, carried))
            o_p = _sb_prompt(*qkv_p, bias, li)
            xp3 = _mixer_out(xp3, g_mix, ws_out, li, qm_p.reshape(b, t, MEM_WIDTH), 0, mk_p, mv_p, tm=tm_mix,
                             tok=o_p)

            q_s, k_s, v_s, qm_s = _norm_proj(
                xs, g_in, ws_in, li, 0, (flat(tok_width), flat(tok_width), flat(tok_width), flat(MEM_WIDTH)),
                tm=ms, tn=tn_proj)
            k_s3 = k_s.reshape(bs, SAMPLE_ROWS, tok_width)
            v_s3 = v_s.reshape(bs, SAMPLE_ROWS, tok_width)
            q_rows = q_s.reshape(bs, SAMPLE_ROWS, n_heads, HEAD_DIM).transpose(0, 2, 1, 3)
            q_rows = q_rows.reshape(bs, n_heads * SAMPLE_ROWS, HEAD_DIM)
            bias_rows = jnp.repeat(bias, SAMPLE_ROWS)[:, None]
            o_rows = _sb_decode(q_rows, k_s3, v_s3, bias_rows, cache_k, cache_v, page_flat, li,
                                pages_per_step=8)
            o_s = o_rows.reshape(bs, n_heads, SAMPLE_ROWS, HEAD_DIM).transpose(0, 2, 1, 3)
            o_s = o_s.reshape(bs, SAMPLE_ROWS, tok_width)
            xs3 = _mixer_out(xs3, g_mix, ws_out, li, qm_s.reshape(bs, SAMPLE_ROWS, MEM_WIDTH), 0, mk_s, mv_s,
                             tm=SAMPLE_ROWS, tok=o_s)
            ks.append(k_s3[:, :ts].reshape(bs, ts, n_heads, HEAD_DIM))
            vs.append(v_s3[:, :ts].reshape(bs, ts, n_heads, HEAD_DIM))
        xp = xp3.reshape(b * t, d)
        xs = xs3.reshape(ms, d)

        xs, *w_bf = _ffn_half_casting(xs, norm_ffn2[i], ffn2_w_gate_up, ffn2_w_down, i, tf=tf_ffn)
        xp = _ffn_half(xp, norm_ffn2[i], *w_bf, tm=tm_ffn, tf=tf_ffn)

    y_p = xp.reshape(b, t, d)
    y_s = xs.reshape(bs, SAMPLE_ROWS, d)[:, :ts]
    k_p, v_p = (jnp.transpose(a, (0, 1, 3, 2, 4)) for a in qkv_p[1:])
    return (y_p, y_s, jnp.stack(pool_p), k_p, v_p, jnp.stack(mkp), jnp.stack(mvp),
            jnp.stack(pool_s), jnp.stack(ks), jnp.stack(vs))
```

```python
import functools

import jax
import jax.numpy as jnp
from jax import lax
from jax.experimental import pallas as pl
from jax.experimental.pallas import tpu as pltpu

F32 = jnp.float32
BF16 = jnp.bfloat16

HEAD_DIM = 128
MEM_HEADS = 4
MEM_WIDTH = MEM_HEADS * HEAD_DIM
POOL_WINDOWS = (2, 4, 8, 16)
POOL_BUF = max(POOL_WINDOWS) - 1
HALO = 16
N_MIXERS = 2
SB_BLOCK = 128
SB_QTILE = 512
SB_KCHUNK = 256
RMS_EPS = 1e-6
SAMPLE_ROWS = 8
LOG2E = 1.4426950408889634
MIB = 1024 * 1024
VMEM_CAP = 62 * MIB

_NT = (((1,), (1,)), ((), ()))


def _params(semantics, vmem_bytes):
    return pltpu.CompilerParams(dimension_semantics=semantics,
                                vmem_limit_bytes=int(min(vmem_bytes, VMEM_CAP)))


def _rms(x, g):
    ms = jnp.mean(x * x, axis=-1, keepdims=True)
    return x * lax.rsqrt(ms + RMS_EPS) * g


def _rms_rows(src_ref, gain, dst_ref, residual_ref=None, zero_ref=None):
    chunk = 16
    group = min(8, src_ref.shape[0] // chunk)
    assert src_ref.shape[0] % (group * chunk) == 0

    def body(c, carry):
        for s in range(group):
            rows = pl.ds(pl.multiple_of((c * group + s) * chunk, chunk), chunk)
            y = _rms(src_ref[rows, :], gain)
            if residual_ref is not None:
                y = residual_ref[rows, :] + y
            dst_ref[rows, :] = y.astype(dst_ref.dtype)
            if zero_ref is not None:
                zero_ref[rows, :] = jnp.zeros((chunk, zero_ref.shape[1]), zero_ref.dtype)
        return carry

    lax.fori_loop(0, src_ref.shape[0] // (group * chunk), body, 0)


def _suffix_matrix(n):
    r = lax.broadcasted_iota(jnp.int32, (n, n), 0)
    c = lax.broadcasted_iota(jnp.int32, (n, n), 1)
    return jnp.where(r > c, 1.0, 0.0).astype(BF16)


def _sb_weights(z2, vis, umat, carry):
    neg_abs = lax.bitcast_convert_type(lax.bitcast_convert_type(z2, jnp.uint32) | jnp.uint32(0x80000000), F32)
    sp = jnp.maximum(z2, 0.0) + jnp.log2(1.0 + jnp.exp2(neg_abs))
    if vis is not None:
        sp = jnp.where(vis, sp, 0.0)
    later = jnp.dot(sp.astype(BF16), umat, preferred_element_type=F32)
    w = jnp.exp2(z2 - sp - later - carry)
    if vis is not None:
        w = jnp.where(vis, w, 0.0)
    return w, carry + jnp.sum(sp, axis=-1, keepdims=True)


def _ffn_kernel(x_ref, g_ref, wg_ref, wu_ref, wd_ref, o_ref, *refs, acc_cols, emit_weights):
    h_ref, acc_ref = refs[-2:]
    j = pl.program_id(1)

    @pl.when(j == 0)
    def _():
        _rms_rows(x_ref, g_ref[0:1, :], h_ref, zero_ref=acc_ref)

    if emit_weights:
        for src, dst in zip((wg_ref, wu_ref, wd_ref), refs[:3]):
            dst[...] = src[...].astype(BF16)
        wg_ref, wu_ref, wd_ref = refs[:3]
    h = h_ref[...]
    gate = jnp.dot(h, wg_ref[...], preferred_element_type=F32)
    up = jnp.dot(h, wu_ref[...], preferred_element_type=F32)
    a = (gate * jax.nn.sigmoid(gate) * up).astype(BF16)
    for c0 in range(0, acc_ref.shape[1], acc_cols):
        cols = slice(c0, c0 + acc_cols)
        acc_ref[:, cols] += jnp.dot(a, wd_ref[:, cols], preferred_element_type=F32)

    @pl.when(j == pl.num_programs(1) - 1)
    def _():
        _rms_rows(acc_ref, 0.5 * g_ref[1:2, :], o_ref, residual_ref=x_ref)


def _ffn_half(x, g, w_gate, w_up, w_down, *, tm, tf):
    m, d = x.shape
    f = w_down.shape[0]
    vmem = 2 * (2 * tm * d * 4) + tm * d * (2 + 4) + 2 * 3 * d * tf * 2 + 2 * tm * tf * 4 + 2 * MIB
    return pl.pallas_call(
        functools.partial(_ffn_kernel, acc_cols=min(d, 512), emit_weights=False),
        grid=(m // tm, f // tf),
        in_specs=[pl.BlockSpec((tm, d), lambda i, j: (i, 0)),
                  pl.BlockSpec((2, d), lambda i, j: (0, 0)),
                  pl.BlockSpec((d, tf), lambda i, j: (0, j)),
                  pl.BlockSpec((d, tf), lambda i, j: (0, j)),
                  pl.BlockSpec((tf, d), lambda i, j: (j, 0))],
        out_specs=pl.BlockSpec((tm, d), lambda i, j: (i, 0)),
        out_shape=jax.ShapeDtypeStruct((m, d), F32),
        scratch_shapes=[pltpu.VMEM((tm, d), BF16), pltpu.VMEM((tm, d), F32)],
        compiler_params=_params(("parallel", "arbitrary"), vmem),
        name="ffn_half",
    )(x, g, w_gate, w_up, w_down)


def _ffn_half_casting(x, g, w_gu, w_d, layer, *, tf):
    m, d = x.shape
    f = w_d.shape[1]
    nf = f // tf
    vmem = 4 * m * d * 4 + 2 * 3 * d * tf * (4 + 2) + 3 * d * tf * 2 + 6 * m * tf * 4 + 4 * MIB
    return pl.pallas_call(
        functools.partial(_ffn_kernel, acc_cols=min(d, 512), emit_weights=True),
        grid=(1, nf),
        in_specs=[pl.BlockSpec((m, d), lambda i, j: (0, 0)),
                  pl.BlockSpec((2, d), lambda i, j: (0, 0)),
                  pl.BlockSpec((None, d, tf), lambda i, j: (layer, 0, j)),
                  pl.BlockSpec((None, d, tf), lambda i, j: (layer, 0, j + nf)),
                  pl.BlockSpec((None, tf, d), lambda i, j: (layer, j, 0))],
        out_specs=[pl.BlockSpec((m, d), lambda i, j: (0, 0)),
                   pl.BlockSpec((d, tf), lambda i, j: (0, j)),
                   pl.BlockSpec((d, tf), lambda i, j: (0, j)),
                   pl.BlockSpec((tf, d), lambda i, j: (j, 0))],
        out_shape=[jax.ShapeDtypeStruct((m, d), F32), jax.ShapeDtypeStruct((d, f), BF16),
                   jax.ShapeDtypeStruct((d, f), BF16), jax.ShapeDtypeStruct((f, d), BF16)],
        scratch_shapes=[pltpu.VMEM((m, d), BF16), pltpu.VMEM((m, d), F32)],
        compiler_params=_params(("arbitrary", "arbitrary"), vmem),
        name="ffn_half_casting",
    )(x, g, w_gu, w_gu, w_d)


def _norm_proj_kernel(x_ref, g_ref, w_ref, *refs, bounds, by_head, n_carried):
    o_refs, h_ref = refs[n_carried:-1], refs[-1]
    j = pl.program_id(1)

    @pl.when(j == 0)
    def _():
        h_ref[...] = _rms(x_ref[...], g_ref[...]).astype(BF16)

    for o_ref, (lo, hi), heads in zip(o_refs, bounds, by_head):
        @pl.when(jnp.logical_and(j >= lo, j < hi))
        def _(o_ref=o_ref, heads=heads):
            out = jnp.dot(h_ref[...], w_ref[...], preferred_element_type=F32).astype(o_ref.dtype)
            if heads:
                for hh in range(o_ref.shape[1]):
                    o_ref[0, hh] = out[:, hh * HEAD_DIM:(hh + 1) * HEAD_DIM]
            else:
                o_ref[...] = out


def _norm_proj(x, g, w, layer, col0, segments, *, tm, tn, seq_len=None, slots=None):
    m, d = x.shape
    assert col0 % tn == 0 and m % tm == 0 and all(n % tn == 0 for n, _, _ in segments)
    cb = col0 // tn
    n_slots, slot, carried = slots if slots is not None else (1, 0, {})
    bounds, lo = [], 0
    for n, _, _ in segments:
        bounds.append((lo, lo + n // tn))
        lo += n // tn
    out_specs, out_shapes, aliases, carried_args = [], [], {}, []
    for si, ((n, dtype, heads), (a, b)) in enumerate(zip(segments, bounds)):
        local = lambda j, a=a, b=b: jnp.clip(j - a, 0, b - a - 1)
        if heads:
            assert seq_len % tm == 0 and tn % HEAD_DIM == 0
            tiles = seq_len // tm
            out_specs.append(pl.BlockSpec((None, 1, tn // HEAD_DIM, tm, HEAD_DIM),
                                          lambda i, j, local=local: (slot, i // tiles, local(j), i % tiles, 0)))
            out_shapes.append(jax.ShapeDtypeStruct((n_slots, m // seq_len, n // HEAD_DIM, seq_len, HEAD_DIM), dtype))
            if si in carried:
                aliases[3 + len(carried_args)] = si
                carried_args.append(carried[si])
        else:
            out_specs.append(pl.BlockSpec((tm, tn), lambda i, j, local=local: (i, local(j))))
            out_shapes.append(jax.ShapeDtypeStruct((m, n), dtype))
    vmem = 2 * tm * d * 4 + tm * d * 2 + 2 * d * tn * 2 + (2 * len(segments) + 2) * tm * tn * 4 + 4 * MIB
    return pl.pallas_call(
        functools.partial(_norm_proj_kernel, bounds=tuple(bounds), by_head=tuple(s[2] for s in segments),
                          n_carried=len(carried_args)),
        grid=(m // tm, lo),
        in_specs=[pl.BlockSpec((tm, d), lambda i, j: (i, 0)),
                  pl.BlockSpec((1, d), lambda i, j: (0, 0)),
                  pl.BlockSpec((None, d, tn), lambda i, j: (layer, 0, j + cb))]
                 + [pl.BlockSpec(memory_space=pl.ANY)] * len(carried_args),
        out_specs=out_specs,
        out_shape=out_shapes,
        input_output_aliases=aliases,
        scratch_shapes=[pltpu.VMEM((tm, d), BF16)],
        compiler_params=_params(("parallel", "arbitrary"), vmem),
        name="norm_proj",
    )(x, g, w, *carried_args)


def _sb_prompt_kernel(bias_ref, q_ref, k_ref, v_ref, o_ref, kb_ref, vb_ref, u_ref):
    tq, kc = SB_QTILE, SB_KCHUNK
    kb_ref[...] = k_ref[0, 0].astype(BF16)
    vb_ref[...] = v_ref[0, 0].astype(BF16)
    u_ref[...] = _suffix_matrix(kc)
    bias = bias_ref[pl.program_id(1)] * LOG2E
    scale = HEAD_DIM ** -0.5 * LOG2E
    r = lax.broadcasted_iota(jnp.int32, (tq, kc), 0)
    c = lax.broadcasted_iota(jnp.int32, (tq, kc), 1)
    for i in range(q_ref.shape[2] // tq):
        q = q_ref[0, 0, i * tq:(i + 1) * tq, :]
        acc = jnp.zeros((tq, HEAD_DIM), F32)
        carry = jnp.zeros((tq, 1), F32)
        for j in range((i + 1) * tq // kc - 1, -1, -1):
            rows = slice(j * kc, (j + 1) * kc)
            vis = (c + (j * kc - i * tq) < r) if (j + 1) * kc > i * tq else None
            z = lax.dot_general(q, kb_ref[rows, :], _NT, preferred_element_type=F32) * scale + bias
            w, carry = _sb_weights(z, vis, u_ref[...], carry)
            acc = acc + jnp.dot(w.astype(BF16), vb_ref[rows, :], preferred_element_type=F32)
        o_ref[0, i * tq:(i + 1) * tq, :] = acc.astype(o_ref.dtype)


def _sb_prompt(q, k, v, bias, slot):
    _, b, nh, t, _ = q.shape
    assert t % SB_QTILE == 0 and SB_QTILE % SB_KCHUNK == 0
    head_spec = pl.BlockSpec((None, 1, 1, t, HEAD_DIM), lambda bb, h: (slot, bb, h, 0, 0))
    return pl.pallas_call(
        _sb_prompt_kernel,
        grid=(b, nh),
        in_specs=[pl.BlockSpec(memory_space=pltpu.SMEM), head_spec, head_spec, head_spec],
        out_specs=pl.BlockSpec((1, t, HEAD_DIM), lambda bb, h: (bb, 0, h)),
        out_shape=jax.ShapeDtypeStruct((b, t, nh * HEAD_DIM), BF16),
        scratch_shapes=[pltpu.VMEM((t, HEAD_DIM), BF16), pltpu.VMEM((t, HEAD_DIM), BF16),
                        pltpu.VMEM((SB_KCHUNK, SB_KCHUNK), BF16)],
        compiler_params=_params(("parallel", "parallel"), 32 * MIB),
        name="sb_prompt",
    )(bias, q, k, v)


def _sb_decode_kernel(pt_ref, q_ref, knew_ref, vnew_ref, bias_ref, *refs, pages_per_step, n_heads):
    del pt_ref
    k_refs = refs[:pages_per_step]
    v_refs = refs[pages_per_step:2 * pages_per_step]
    o_ref, acc_ref, car_ref = refs[2 * pages_per_step:]
    p = pl.program_id(1)
    rows = n_heads * SAMPLE_ROWS
    scale = HEAD_DIM ** -0.5 * LOG2E
    chunk = min(SB_KCHUNK, pages_per_step * SB_BLOCK)
    umat = _suffix_matrix(chunk)
    bias = bias_ref[...] * LOG2E
    q = q_ref[0]
    qh = [q[h * SAMPLE_ROWS:(h + 1) * SAMPLE_ROWS].astype(BF16) for h in range(n_heads)]

    def sweep(keys, values, n_keys, vis):
        zs = [lax.dot_general(qh[h], keys(h), _NT, preferred_element_type=F32) for h in range(n_heads)]
        z = jnp.concatenate(zs, axis=0) * scale + bias
        step = min(chunk, n_keys)
        carry = car_ref[...]
        ws = []
        for c0 in range(n_keys - step, -1, -step):
            w, carry = _sb_weights(z[:, c0:c0 + step], vis, umat[:step, :step], carry)
            ws.insert(0, w)
        w = jnp.concatenate(ws, axis=1)
        outs = [jnp.dot(w[h * SAMPLE_ROWS:(h + 1) * SAMPLE_ROWS].astype(BF16), values(h),
                        preferred_element_type=F32) for h in range(n_heads)]
        acc_ref[...] += jnp.concatenate(outs, axis=0)
        car_ref[...] = carry

    @pl.when(p == 0)
    def _():
        acc_ref[...] = jnp.zeros_like(acc_ref)
        car_ref[...] = jnp.zeros_like(car_ref)
        t = lax.broadcasted_iota(jnp.int32, (rows, SB_BLOCK), 0) % SAMPLE_ROWS
        s = lax.broadcasted_iota(jnp.int32, (rows, SB_BLOCK), 1)
        zeros = jnp.zeros((SB_BLOCK - SAMPLE_ROWS, HEAD_DIM), F32)

        def new_block(ref, h):
            head = ref[0, :, h * HEAD_DIM:(h + 1) * HEAD_DIM]
            return jnp.concatenate([head, zeros], axis=0).astype(BF16)

        sweep(functools.partial(new_block, knew_ref), functools.partial(new_block, vnew_ref), SB_BLOCK, s < t)

    def pages(page_refs, h):
        return jnp.concatenate([r[h].astype(BF16) for r in reversed(page_refs)], axis=0)

    sweep(functools.partial(pages, k_refs), functools.partial(pages, v_refs), pages_per_step * SB_BLOCK, None)

    @pl.when(p == pl.num_programs(1) - 1)
    def _():
        o_ref[0] = acc_ref[...]


def _sb_decode(q_rows, k_new, v_new, bias_rows, cache_k, cache_v, page_flat, layer, *, pages_per_step):
    bsz, rows, _ = q_rows.shape
    n_heads = rows // SAMPLE_ROWS
    width = n_heads * HEAD_DIM
    page = cache_k.shape[3]
    n_pages = page_flat.shape[0] // bsz
    assert page == SB_BLOCK and n_pages % pages_per_step == 0
    steps = n_pages // pages_per_step

    def page_map(c):
        return lambda b, p, pt: (layer, pt[b * n_pages + n_pages - 1 - (p * pages_per_step + c)], 0, 0, 0)

    def page_specs():
        return [pl.BlockSpec((None, None, n_heads, page, HEAD_DIM), page_map(c)) for c in range(pages_per_step)]

    seq_map = lambda b, p, pt: (b, 0, 0)
    grid_spec = pltpu.PrefetchScalarGridSpec(
        num_scalar_prefetch=1,
        grid=(bsz, steps),
        in_specs=[pl.BlockSpec((1, rows, HEAD_DIM), seq_map),
                  pl.BlockSpec((1, SAMPLE_ROWS, width), seq_map),
                  pl.BlockSpec((1, SAMPLE_ROWS, width), seq_map),
                  pl.BlockSpec((rows, 1), lambda b, p, pt: (0, 0))] + page_specs() + page_specs(),
        out_specs=pl.BlockSpec((1, rows, HEAD_DIM), seq_map),
        scratch_shapes=[pltpu.VMEM((rows, HEAD_DIM), F32), pltpu.VMEM((rows, 1), F32)],
    )
    vmem = 2 * 2 * pages_per_step * page * width * 4 + 12 * MIB
    return pl.pallas_call(
        functools.partial(_sb_decode_kernel, pages_per_step=pages_per_step, n_heads=n_heads),
        grid_spec=grid_spec,
        out_shape=jax.ShapeDtypeStruct((bsz, rows, HEAD_DIM), F32),
        compiler_params=_params(("parallel", "arbitrary"), vmem),
        name="sb_decode",
    )(page_flat, q_rows, k_new, v_new, bias_rows, *([cache_k] * pages_per_step), *([cache_v] * pages_per_step))


def _mixer_kernel(*refs, pool, n_prefix, halo_valid_from, mem_by_head):
    if pool:
        (u_ref, halo_ref, qm_ref, mk_ref, mv_ref, x_ref, g_ref, wo_ref, gm_ref, sc_ref,
         o_ref, cat_ref, ubuf_ref) = refs
    else:
        tok_ref, qm_ref, mk_ref, mv_ref, x_ref, g_ref, wo_ref, o_ref, cat_ref = refs
    t = pl.program_id(1)
    tm = x_ref.shape[1]
    tok_width = cat_ref.shape[1] - MEM_WIDTH

    if pool:
        group = tok_width // len(POOL_WINDOWS)
        halo = halo_ref[0]
        ubuf_ref[0:HALO, :] = jnp.where(t >= halo_valid_from, halo, 0.0)
        ubuf_ref[HALO:HALO + tm, :] = u_ref[0]
        pos = t * tm + lax.broadcasted_iota(jnp.int32, (tm, 1), 0) + (n_prefix + 1)
        for gi, win in enumerate(POOL_WINDOWS):
            cols = slice(gi * group, (gi + 1) * group)
            own = ubuf_ref[HALO:HALO + tm, cols]
            tot = own
            for back in range(1, win):
                tot = tot + ubuf_ref[HALO - back:HALO - back + tm, cols]
            count = jnp.minimum(pos, win).astype(F32)
            diff = tot / count - own
            mixed = jnp.dot(diff.astype(BF16), gm_ref[gi], preferred_element_type=F32) * sc_ref[:, cols]
            cat_ref[:, cols] = mixed.astype(BF16)
    else:
        cat_ref[:, 0:tok_width] = tok_ref[0].astype(BF16)

    scale = HEAD_DIM ** -0.5
    if mem_by_head:
        mk_heads = pltpu.einshape("mhd->hmd", mk_ref[0].astype(BF16))
        mv_heads = pltpu.einshape("mhd->hmd", mv_ref[0].astype(BF16))
    for h in range(MEM_HEADS):
        cols = slice(h * HEAD_DIM, (h + 1) * HEAD_DIM)
        qh = qm_ref[0, :, cols].astype(BF16)
        if mem_by_head:
            kh, vh = mk_heads[h], mv_heads[h]
        else:
            kh, vh = mk_ref[0, :, cols].astype(BF16), mv_ref[0, :, cols].astype(BF16)
        s = lax.dot_general(qh, kh, _NT, preferred_element_type=F32) * scale
        e = jnp.exp(s - jnp.max(s, axis=-1, keepdims=True))
        prob = e / jnp.sum(e, axis=-1, keepdims=True)
        oh = jnp.dot(prob.astype(BF16), vh, preferred_element_type=F32)
        cat_ref[:, tok_width + h * HEAD_DIM:tok_width + (h + 1) * HEAD_DIM] = oh.astype(BF16)

    y = jnp.dot(cat_ref[...], wo_ref[...], preferred_element_type=F32)
    o_ref[0] = x_ref[0] + _rms(y, g_ref[1:2, :])


def _mixer_out(x, g, w_out, layer, qm_src, qm_block, mk, mv, *, tm, tok=None, pool_src=None, halo_src=None,
               halo_valid_from=0, n_prefix=0, group_maps=None, pool_scale=None):
    b, t, d = x.shape
    tok_width = d - MEM_WIDTH
    mem_len = mk.shape[1]
    pool = tok is None
    mem_by_head = mk.ndim == 4
    qm_spec = pl.BlockSpec((1, tm, MEM_WIDTH), lambda bb, i: (bb, i, qm_block))
    if mem_by_head:
        mem_spec = pl.BlockSpec((1, mem_len, MEM_HEADS, HEAD_DIM), lambda bb, i: (bb, 0, 0, 0))
    else:
        mem_spec = pl.BlockSpec((1, mem_len, MEM_WIDTH), lambda bb, i: (bb, 0, 0))
    x_spec = pl.BlockSpec((1, tm, d), lambda bb, i: (bb, i, 0))
    g_spec = pl.BlockSpec((2, d), lambda bb, i: (0, 0))
    wo_spec = pl.BlockSpec((None, d, d), lambda bb, i: (layer, 0, 0))
    scratch = [pltpu.VMEM((tm, d), BF16)]
    if pool:
        halo_blocks = tm // HALO
        if halo_src is pool_src:
            halo_map = lambda bb, i: (bb, jnp.maximum(i * halo_blocks - 1, 0), 0)
        else:
            halo_map = lambda bb, i: (bb, 0, 0)
        _, ng, gw, _ = group_maps.shape
        in_specs = [pl.BlockSpec((1, tm, tok_width), lambda bb, i: (bb, i, 0)),
                    pl.BlockSpec((1, HALO, tok_width), halo_map),
                    qm_spec, mem_spec, mem_spec, x_spec, g_spec, wo_spec,
                    pl.BlockSpec((None, ng, gw, gw), lambda bb, i: (layer, 0, 0, 0)),
                    pl.BlockSpec((1, tok_width), lambda bb, i: (0, 0))]
        args = (pool_src, halo_src, qm_src, mk, mv, x, g, w_out, group_maps, pool_scale)
        scratch.append(pltpu.VMEM((HALO + tm, tok_width), F32))
    else:
        in_specs = [pl.BlockSpec((1, tm, tok_width), lambda bb, i: (bb, i, 0)),
                    qm_spec, mem_spec, mem_spec, x_spec, g_spec, wo_spec]
        args = (tok, qm_src, mk, mv, x, g, w_out)
    vmem = (2 * (tm * tok_width * 4 + tm * MEM_WIDTH * 4 + 2 * tm * d * 4 + 4 * mem_len * MEM_WIDTH * 4 + d * d * 2)
            + (HALO + tm) * tok_width * 4 + tm * d * 2 + 3 * tm * d * 4 + 6 * MIB)
    return pl.pallas_call(
        functools.partial(_mixer_kernel, pool=pool, n_prefix=n_prefix, halo_valid_from=halo_valid_from,
                          mem_by_head=mem_by_head),
        grid=(b, t // tm),
        in_specs=in_specs,
        out_specs=x_spec,
        out_shape=jax.ShapeDtypeStruct((b, t, d), F32),
        scratch_shapes=scratch,
        compiler_params=_params(("parallel", "arbitrary"), vmem),
        name="mixer_pool" if pool else "mixer_sb",
    )(*args)


def kernel(x_prompt, x_sample, state_pool, cache_sb_k, cache_sb_v, cache_mem_k, cache_mem_v, page_table, mem_prompt, norm_ffn1, ffn1_w_gate_up, ffn1_w_down, norm_mix, norm_mem, mem_w_kv, pool_w_in, pool_group_maps, pool_scale, pool_w_out, sb_w_in, sb_logit_bias, sb_w_out, norm_ffn2, ffn2_w_gate_up, ffn2_w_down):
    b, t, d = x_prompt.shape
    bs, ts, _ = x_sample.shape
    depth = norm_ffn1.shape[0]
    mem_len = mem_prompt.shape[1]
    tok_width = d - MEM_WIDTH
    n_heads = tok_width // HEAD_DIM
    assert ts <= SAMPLE_ROWS

    tm_ffn, tf_ffn = 1024, 512
    tm_proj, tn_proj = 1024, 512
    tm_mix = 512
    ms = bs * SAMPLE_ROWS

    w_kv = mem_w_kv.astype(BF16)
    wp_in, wp_out, gmaps = pool_w_in.astype(BF16), pool_w_out.astype(BF16), pool_group_maps.astype(BF16)
    ws_in, ws_out = sb_w_in.astype(BF16), sb_w_out.astype(BF16)

    xp = x_prompt.reshape(b * t, d)
    xs = jnp.pad(x_sample, ((0, 0), (0, SAMPLE_ROWS - ts), (0, 0))).reshape(ms, d)
    mem2 = mem_prompt.reshape(b * mem_len, d)
    page_flat = page_table.reshape(-1).astype(jnp.int32)
    cache_k = jnp.transpose(cache_sb_k, (0, 1, 3, 2, 4))
    cache_v = jnp.transpose(cache_sb_v, (0, 1, 3, 2, 4))

    pool_p, pool_s, ks, vs, mkp, mvp = [], [], [], [], [], []
    qkv_p = None
    for i in range(depth):
        li = i // N_MIXERS
        xs, *w_bf = _ffn_half_casting(xs, norm_ffn1[i], ffn1_w_gate_up, ffn1_w_down, i, tf=tf_ffn)
        xp = _ffn_half(xp, norm_ffn1[i], *w_bf, tm=tm_ffn, tf=tf_ffn)

        g_mem = norm_mem[i][None]
        flat = lambda n: (n, F32, False)
        mk_p, mv_p = _norm_proj(mem2, g_mem, w_kv, i, 0, (flat(MEM_WIDTH), flat(MEM_WIDTH)), tm=tm_proj, tn=tn_proj)
        mk_p = mk_p.reshape(b, mem_len, MEM_WIDTH)
        mv_p = mv_p.reshape(b, mem_len, MEM_WIDTH)
        mkp.append(mk_p.reshape(b, mem_len, MEM_HEADS, HEAD_DIM))
        mvp.append(mv_p.reshape(b, mem_len, MEM_HEADS, HEAD_DIM))
        mk_s, mv_s = cache_mem_k[i], cache_mem_v[i]

        g_mix = norm_mix[i]
        g_in = g_mix[0:1]
        xp3 = xp.reshape(b, t, d)
        xs3 = xs.reshape(bs, SAMPLE_ROWS, d)
        if i % N_MIXERS == 0:
            scale = pool_scale[li][None]
            (proj_p,) = _norm_proj(xp, g_in, wp_in, li, 0, (flat(d),), tm=tm_proj, tn=2 * tn_proj)
            (proj_s,) = _norm_proj(xs, g_in, wp_in, li, 0, (flat(d),), tm=ms, tn=tn_proj)
            proj_p = proj_p.reshape(b, t, d)
            proj_s = proj_s.reshape(bs, SAMPLE_ROWS, d)
            qm_block = tok_width // MEM_WIDTH
            xp3 = _mixer_out(xp3, g_mix, wp_out, li, proj_p, qm_block, mk_p, mv_p, tm=tm_mix, pool_src=proj_p,
                             halo_src=proj_p, halo_valid_from=1, n_prefix=0, group_maps=gmaps, pool_scale=scale)
            prefix = state_pool[li]
            n_prefix = prefix.shape[1]
            halo_s = jnp.pad(prefix, ((0, 0), (HALO - n_prefix, 0), (0, 0)))
            xs3 = _mixer_out(xs3, g_mix, wp_out, li, proj_s, qm_block, mk_s, mv_s, tm=SAMPLE_ROWS, pool_src=proj_s,
                             halo_src=halo_s, halo_valid_from=0, n_prefix=n_prefix, group_maps=gmaps,
                             pool_scale=scale)
            pool_p.append(proj_p[:, t - POOL_BUF:, :tok_width])
            u_ext = jnp.concatenate([prefix, proj_s[:, :ts, :tok_width]], axis=1)
            pool_s.append(u_ext[:, u_ext.shape[1] - POOL_BUF:])
        else:
            bias = sb_logit_bias[li].astype(F32)
            carried = {} if qkv_p is None else dict(enumerate(qkv_p))
            *qkv_p, qm_p = _norm_proj(
                xp, g_in, ws_in, li, 0,
                ((tok_width, BF16, True), (tok_width, F32, True), (tok_width, F32, True), flat(MEM_WIDTH)),
                tm=tm_proj, tn=tn_proj, seq_len=t, slots=(sb_w_in.shape[0], li, carried))
            o_p = _sb_prompt(*qkv_p, bias, li)
            xp3 = _mixer_out(xp3, g_mix, ws_out, li, qm_p.reshape(b, t, MEM_WIDTH), 0, mk_p, mv_p, tm=tm_mix,
                             tok=o_p)

            q_s, k_s, v_s, qm_s = _norm_proj(
                xs, g_in, ws_in, li, 0, (flat(tok_width), flat(tok_width), flat(tok_width), flat(MEM_WIDTH)),
                tm=ms, tn=tn_proj)
            k_s3 = k_s.reshape(bs, SAMPLE_ROWS, tok_width)
            v_s3 = v_s.reshape(bs, SAMPLE_ROWS, tok_width)
            q_rows = q_s.reshape(bs, SAMPLE_ROWS, n_heads, HEAD_DIM).transpose(0, 2, 1, 3)
            q_rows = q_rows.reshape(bs, n_heads * SAMPLE_ROWS, HEAD_DIM)
            bias_rows = jnp.repeat(bias, SAMPLE_ROWS)[:, None]
            o_rows = _sb_decode(q_rows, k_s3, v_s3, bias_rows, cache_k, cache_v, page_flat, li,
                                pages_per_step=8)
            o_s = o_rows.reshape(bs, n_heads, SAMPLE_ROWS, HEAD_DIM).transpose(0, 2, 1, 3)
            o_s = o_s.reshape(bs, SAMPLE_ROWS, tok_width)
            xs3 = _mixer_out(xs3, g_mix, ws_out, li, qm_s.reshape(bs, SAMPLE_ROWS, MEM_WIDTH), 0, mk_s, mv_s,
                             tm=SAMPLE_ROWS, tok=o_s)
            ks.append(k_s3[:, :ts].reshape(bs, ts, n_heads, HEAD_DIM))
            vs.append(v_s3[:, :ts].reshape(bs, ts, n_heads, HEAD_DIM))
        xp = xp3.reshape(b * t, d)
        xs = xs3.reshape(ms, d)

        xs, *w_bf = _ffn_half_casting(xs, norm_ffn2[i], ffn2_w_gate_up, ffn2_w_down, i, tf=tf_ffn)
        xp = _ffn_half(xp, norm_ffn2[i], *w_bf, tm=tm_ffn, tf=tf_ffn)

    y_p = xp.reshape(b, t, d)
    y_s = xs.reshape(bs, SAMPLE_ROWS, d)[:, :ts]
    k_p, v_p = (jnp.transpose(a, (0, 1, 3, 2, 4)) for a in qkv_p[1:])
    return (y_p, y_s, jnp.stack(pool_p), k_p, v_p, jnp.stack(mkp), jnp.stack(mvp),
            jnp.stack(pool_s), jnp.stack(ks), jnp.stack(vs))
```

```python
import functools

import jax
import jax.numpy as jnp
from jax import lax
from jax.experimental import pallas as pl
from jax.experimental.pallas import tpu as pltpu

F32 = jnp.float32
BF16 = jnp.bfloat16

HEAD_DIM = 128
MEM_HEADS = 4
MEM_WIDTH = MEM_HEADS * HEAD_DIM
POOL_WINDOWS = (2, 4, 8, 16)
POOL_BUF = max(POOL_WINDOWS) - 1
HALO = 16
N_MIXERS = 2
SB_BLOCK = 128
SB_QTILE = 512
SB_KCHUNK = 256
RMS_EPS = 1e-6
SAMPLE_ROWS = 8
LOG2E = 1.4426950408889634
MIB = 1024 * 1024
VMEM_CAP = 62 * MIB

_NT = (((1,), (1,)), ((), ()))


def _params(semantics, vmem_bytes):
    return pltpu.CompilerParams(dimension_semantics=semantics,
                                vmem_limit_bytes=int(min(vmem_bytes, VMEM_CAP)))


def _rms(x, g):
    ms = jnp.mean(x * x, axis=-1, keepdims=True)
    return x * lax.rsqrt(ms + RMS_EPS) * g


def _rms_rows(src_ref, gain, dst_ref, residual_ref=None, zero_ref=None):
    chunk = 16
    group = min(8, src_ref.shape[0] // chunk)
    assert src_ref.shape[0] % (group * chunk) == 0

    def body(c, carry):
        for s in range(group):
            rows = pl.ds(pl.multiple_of((c * group + s) * chunk, chunk), chunk)
            y = _rms(src_ref[rows, :], gain)
            if residual_ref is not None:
                y = residual_ref[rows, :] + y
            dst_ref[rows, :] = y.astype(dst_ref.dtype)
            if zero_ref is not None:
                zero_ref[rows, :] = jnp.zeros((chunk, zero_ref.shape[1]), zero_ref.dtype)
        return carry

    lax.fori_loop(0, src_ref.shape[0] // (group * chunk), body, 0)


def _suffix_matrix(n):
    r = lax.broadcasted_iota(jnp.int32, (n, n), 0)
    c = lax.broadcasted_iota(jnp.int32, (n, n), 1)
    return jnp.where(r > c, 1.0, 0.0).astype(BF16)


def _sb_weights(z2, vis, umat, carry):
    neg_abs = lax.bitcast_convert_type(lax.bitcast_convert_type(z2, jnp.uint32) | jnp.uint32(0x80000000), F32)
    sp = jnp.maximum(z2, 0.0) + jnp.log2(1.0 + jnp.exp2(neg_abs))
    if vis is not None:
        sp = jnp.where(vis, sp, 0.0)
    later = jnp.dot(sp.astype(BF16), umat, preferred_element_type=F32)
    w = jnp.exp2(z2 - sp - later - carry)
    if vis is not None:
        w = jnp.where(vis, w, 0.0)
    return w, carry + jnp.sum(sp, axis=-1, keepdims=True)


def _ffn_kernel(x_ref, g_ref, wg_ref, wu_ref, wd_ref, o_ref, *refs, acc_cols, emit_weights):
    h_ref, acc_ref = refs[-2:]
    j = pl.program_id(1)

    @pl.when(j == 0)
    def _():
        _rms_rows(x_ref, g_ref[0:1, :], h_ref, zero_ref=acc_ref)

    if emit_weights:
        for src, dst in zip((wg_ref, wu_ref, wd_ref), refs[:3]):
            dst[...] = src[...].astype(BF16)
        wg_ref, wu_ref, wd_ref = refs[:3]
    h = h_ref[...]
    gate = jnp.dot(h, wg_ref[...], preferred_element_type=F32)
    up = jnp.dot(h, wu_ref[...], preferred_element_type=F32)
    a = (gate * jax.nn.sigmoid(gate) * up).astype(BF16)
    for c0 in range(0, acc_ref.shape[1], acc_cols):
        cols = slice(c0, c0 + acc_cols)
        acc_ref[:, cols] += jnp.dot(a, wd_ref[:, cols], preferred_element_type=F32)

    @pl.when(j == pl.num_programs(1) - 1)
    def _():
        _rms_rows(acc_ref, 0.5 * g_ref[1:2, :], o_ref, residual_ref=x_ref)


def _ffn_half(x, g, w_gate, w_up, w_down, *, tm, tf):
    m, d = x.shape
    f = w_down.shape[0]
    vmem = 2 * (2 * tm * d * 4) + tm * d * (2 + 4) + 2 * 3 * d * tf * 2 + 2 * tm * tf * 4 + 2 * MIB
    return pl.pallas_call(
        functools.partial(_ffn_kernel, acc_cols=min(d, 512), emit_weights=False),
        grid=(m // tm, f // tf),
        in_specs=[pl.BlockSpec((tm, d), lambda i, j: (i, 0)),
                  pl.BlockSpec((2, d), lambda i, j: (0, 0)),
                  pl.BlockSpec((d, tf), lambda i, j: (0, j)),
                  pl.BlockSpec((d, tf), lambda i, j: (0, j)),
                  pl.BlockSpec((tf, d), lambda i, j: (j, 0))],
        out_specs=pl.BlockSpec((tm, d), lambda i, j: (i, 0)),
        out_shape=jax.ShapeDtypeStruct((m, d), F32),
        scratch_shapes=[pltpu.VMEM((tm, d), BF16), pltpu.VMEM((tm, d), F32)],
        compiler_params=_params(("parallel", "arbitrary"), vmem),
        name="ffn_half",
    )(x, g, w_gate, w_up, w_down)


def _ffn_half_casting(x, g, w_gu, w_d, layer, *, tf):
    m, d = x.shape
    f = w_d.shape[1]
    nf = f // tf
    vmem = 4 * m * d * 4 + 2 * 3 * d * tf * (4 + 2) + 3 * d * tf * 2 + 6 * m * tf * 4 + 4 * MIB
    return pl.pallas_call(
        functools.partial(_ffn_kernel, acc_cols=min(d, 512), emit_weights=True),
        grid=(1, nf),
        in_specs=[pl.BlockSpec((m, d), lambda i, j: (0, 0)),
                  pl.BlockSpec((2, d), lambda i, j: (0, 0)),
                  pl.BlockSpec((None, d, tf), lambda i, j: (layer, 0, j)),
                  pl.BlockSpec((None, d, tf), lambda i, j: (layer, 0, j + nf)),
                  pl.BlockSpec((None, tf, d), lambda i, j: (layer, j, 0))],
        out_specs=[pl.BlockSpec((m, d), lambda i, j: (0, 0)),
                   pl.BlockSpec((d, tf), lambda i, j: (0, j)),
                   pl.BlockSpec((d, tf), lambda i, j: (0, j)),
                   pl.BlockSpec((tf, d), lambda i, j: (j, 0))],
        out_shape=[jax.ShapeDtypeStruct((m, d), F32), jax.ShapeDtypeStruct((d, f), BF16),
                   jax.ShapeDtypeStruct((d, f), BF16), jax.ShapeDtypeStruct((f, d), BF16)],
        scratch_shapes=[pltpu.VMEM((m, d), BF16), pltpu.VMEM((m, d), F32)],
        compiler_params=_params(("arbitrary", "arbitrary"), vmem),
        name="ffn_half_casting",
    )(x, g, w_gu, w_gu, w_d)


def _norm_proj_kernel(x_ref, g_ref, w_ref, *refs, bounds, by_head, n_carried):
    o_refs, h_ref = refs[n_carried:-1], refs[-1]
    j = pl.program_id(1)

    @pl.when(j == 0)
    def _():
        h_ref[...] = _rms(x_ref[...], g_ref[...]).astype(BF16)

    for o_ref, (lo, hi), heads in zip(o_refs, bounds, by_head):
        @pl.when(jnp.logical_and(j >= lo, j < hi))
        def _(o_ref=o_ref, heads=heads):
            out = jnp.dot(h_ref[...], w_ref[...], preferred_element_type=F32).astype(o_ref.dtype)
            if heads:
                for hh in range(o_ref.shape[1]):
                    o_ref[0, hh] = out[:, hh * HEAD_DIM:(hh + 1) * HEAD_DIM]
            else:
                o_ref[...] = out


def _norm_proj(x, g, w, layer, col0, segments, *, tm, tn, seq_len=None, slots=None):
    m, d = x.shape
    assert col0 % tn == 0 and m % tm == 0 and all(n % tn == 0 for n, _, _ in segments)
    cb = col0 // tn
    n_slots, slot, carried = slots if slots is not None else (1, 0, {})
    bounds, lo = [], 0
    for n, _, _ in segments:
        bounds.append((lo, lo + n // tn))
        lo += n // tn
    out_specs, out_shapes, aliases, carried_args = [], [], {}, []
    for si, ((n, dtype, heads), (a, b)) in enumerate(zip(segments, bounds)):
        local = lambda j, a=a, b=b: jnp.clip(j - a, 0, b - a - 1)
        if heads:
            assert seq_len % tm == 0 and tn % HEAD_DIM == 0
            tiles = seq_len // tm
            out_specs.append(pl.BlockSpec((None, 1, tn // HEAD_DIM, tm, HEAD_DIM),
                                          lambda i, j, local=local: (slot, i // tiles, local(j), i % tiles, 0)))
            out_shapes.append(jax.ShapeDtypeStruct((n_slots, m // seq_len, n // HEAD_DIM, seq_len, HEAD_DIM), dtype))
            if si in carried:
                aliases[3 + len(carried_args)] = si
                carried_args.append(carried[si])
        else:
            out_specs.append(pl.BlockSpec((tm, tn), lambda i, j, local=local: (i, local(j))))
            out_shapes.append(jax.ShapeDtypeStruct((m, n), dtype))
    vmem = 2 * tm * d * 4 + tm * d * 2 + 2 * d * tn * 2 + (2 * len(segments) + 2) * tm * tn * 4 + 4 * MIB
    return pl.pallas_call(
        functools.partial(_norm_proj_kernel, bounds=tuple(bounds), by_head=tuple(s[2] for s in segments),
                          n_carried=len(carried_args)),
        grid=(m // tm, lo),
        in_specs=[pl.BlockSpec((tm, d), lambda i, j: (i, 0)),
                  pl.BlockSpec((1, d), lambda i, j: (0, 0)),
                  pl.BlockSpec((None, d, tn), lambda i, j: (layer, 0, j + cb))]
                 + [pl.BlockSpec(memory_space=pl.ANY)] * len(carried_args),
        out_specs=out_specs,
        out_shape=out_shapes,
        input_output_aliases=aliases,
        scratch_shapes=[pltpu.VMEM((tm, d), BF16)],
        compiler_params=_params(("parallel", "arbitrary"), vmem),
        name="norm_proj",
    )(x, g, w, *carried_args)


def _sb_prompt_kernel(bias_ref, q_ref, k_ref, v_ref, o_ref, kb_ref, vb_ref, u_ref):
    tq, kc = SB_QTILE, SB_KCHUNK
    kb_ref[...] = k_ref[0, 0].astype(BF16)
    vb_ref[...] = v_ref[0, 0].astype(BF16)
    u_ref[...] = _suffix_matrix(kc)
    bias = bias_ref[pl.program_id(1)] * LOG2E
    scale = HEAD_DIM ** -0.5 * LOG2E
    r = lax.broadcasted_iota(jnp.int32, (tq, kc), 0)
    c = lax.broadcasted_iota(jnp.int32, (tq, kc), 1)
    for i in range(q_ref.shape[2] // tq):
        q = q_ref[0, 0, i * tq:(i + 1) * tq, :]
        acc = jnp.zeros((tq, HEAD_DIM), F32)
        carry = jnp.zeros((tq, 1), F32)
        for j in range((i + 1) * tq // kc - 1, -1, -1):
            rows = slice(j * kc, (j + 1) * kc)
            vis = (c + (j * kc - i * tq) < r) if (j + 1) * kc > i * tq else None
            z = lax.dot_general(q, kb_ref[rows, :], _NT, preferred_element_type=F32) * scale + bias
            w, carry = _sb_weights(z, vis, u_ref[...], carry)
            acc = acc + jnp.dot(w.astype(BF16), vb_ref[rows, :], preferred_element_type=F32)
        o_ref[0, i * tq:(i + 1) * tq, :] = acc.astype(o_ref.dtype)


def _sb_prompt(q, k, v, bias, slot):
    _, b, nh, t, _ = q.shape
    assert t % SB_QTILE == 0 and SB_QTILE % SB_KCHUNK == 0
    head_spec = pl.BlockSpec((None, 1, 1, t, HEAD_DIM), lambda bb, h: (slot, bb, h, 0, 0))
    return pl.pallas_call(
        _sb_prompt_kernel,
        grid=(b, nh),
        in_specs=[pl.BlockSpec(memory_space=pltpu.SMEM), head_spec, head_spec, head_spec],
        out_specs=pl.BlockSpec((1, t, HEAD_DIM), lambda bb, h: (bb, 0, h)),
        out_shape=jax.ShapeDtypeStruct((b, t, nh * HEAD_DIM), BF16),
        scratch_shapes=[pltpu.VMEM((t, HEAD_DIM), BF16), pltpu.VMEM((t, HEAD_DIM), BF16),
                        pltpu.VMEM((SB_KCHUNK, SB_KCHUNK), BF16)],
        compiler_params=_params(("parallel", "parallel"), 32 * MIB),
        name="sb_prompt",
    )(bias, q, k, v)


def _sb_decode_kernel(pt_ref, q_ref, knew_ref, vnew_ref, bias_ref, *refs, pages_per_step, n_heads):
    del pt_ref
    k_refs = refs[:pages_per_step]
    v_refs = refs[pages_per_step:2 * pages_per_step]
    o_ref, acc_ref, car_ref = refs[2 * pages_per_step:]
    p = pl.program_id(1)
    rows = n_heads * SAMPLE_ROWS
    scale = HEAD_DIM ** -0.5 * LOG2E
    chunk = min(SB_KCHUNK, pages_per_step * SB_BLOCK)
    umat = _suffix_matrix(chunk)
    bias = bias_ref[...] * LOG2E
    q = q_ref[0]
    qh = [q[h * SAMPLE_ROWS:(h + 1) * SAMPLE_ROWS].astype(BF16) for h in range(n_heads)]

    def sweep(keys, values, n_keys, vis):
        zs = [lax.dot_general(qh[h], keys(h), _NT, preferred_element_type=F32) for h in range(n_heads)]
        z = jnp.concatenate(zs, axis=0) * scale + bias
        step = min(chunk, n_keys)
        carry = car_ref[...]
        ws = []
        for c0 in range(n_keys - step, -1, -step):
            w, carry = _sb_weights(z[:, c0:c0 + step], vis, umat[:step, :step], carry)
            ws.insert(0, w)
        w = jnp.concatenate(ws, axis=1)
        outs = [jnp.dot(w[h * SAMPLE_ROWS:(h + 1) * SAMPLE_ROWS].astype(BF16), values(h),
                        preferred_element_type=F32) for h in range(n_heads)]
        acc_ref[...] += jnp.concatenate(outs, axis=0)
        car_ref[...] = carry

    @pl.when(p == 0)
    def _():
        acc_ref[...] = jnp.zeros_like(acc_ref)
        car_ref[...] = jnp.zeros_like(car_ref)
        t = lax.broadcasted_iota(jnp.int32, (rows, SB_BLOCK), 0) % SAMPLE_ROWS
        s = lax.broadcasted_iota(jnp.int32, (rows, SB_BLOCK), 1)
        zeros = jnp.zeros((SB_BLOCK - SAMPLE_ROWS, HEAD_DIM), F32)

        def new_block(ref, h):
            head = ref[0, :, h * HEAD_DIM:(h + 1) * HEAD_DIM]
            return jnp.concatenate([head, zeros], axis=0).astype(BF16)

        sweep(functools.partial(new_block, knew_ref), functools.partial(new_block, vnew_ref), SB_BLOCK, s < t)

    def pages(page_refs, h):
        return jnp.concatenate([r[h].astype(BF16) for r in reversed(page_refs)], axis=0)

    sweep(functools.partial(pages, k_refs), functools.partial(pages, v_refs), pages_per_step * SB_BLOCK, None)

    @pl.when(p == pl.num_programs(1) - 1)
    def _():
        o_ref[0] = acc_ref[...]


def _sb_decode(q_rows, k_new, v_new, bias_rows, cache_k, cache_v, page_flat, layer, *, pages_per_step):
    bsz, rows, _ = q_rows.shape
    n_heads = rows // SAMPLE_ROWS
    width = n_heads * HEAD_DIM
    page = cache_k.shape[3]
    n_pages = page_flat.shape[0] // bsz
    assert page == SB_BLOCK and n_pages % pages_per_step == 0
    steps = n_pages // pages_per_step

    def page_map(c):
        return lambda b, p, pt: (layer, pt[b * n_pages + n_pages - 1 - (p * pages_per_step + c)], 0, 0, 0)

    def page_specs():
        return [pl.BlockSpec((None, None, n_heads, page, HEAD_DIM), page_map(c)) for c in range(pages_per_step)]

    seq_map = lambda b, p, pt: (b, 0, 0)
    grid_spec = pltpu.PrefetchScalarGridSpec(
        num_scalar_prefetch=1,
        grid=(bsz, steps),
        in_specs=[pl.BlockSpec((1, rows, HEAD_DIM), seq_map),
                  pl.BlockSpec((1, SAMPLE_ROWS, width), seq_map),
                  pl.BlockSpec((1, SAMPLE_ROWS, width), seq_map),
                  pl.BlockSpec((rows, 1), lambda b, p, pt: (0, 0))] + page_specs() + page_specs(),
        out_specs=pl.BlockSpec((1, rows, HEAD_DIM), seq_map),
        scratch_shapes=[pltpu.VMEM((rows, HEAD_DIM), F32), pltpu.VMEM((rows, 1), F32)],
    )
    vmem = 2 * 2 * pages_per_step * page * width * 4 + 12 * MIB
    return pl.pallas_call(
        functools.partial(_sb_decode_kernel, pages_per_step=pages_per_step, n_heads=n_heads),
        grid_spec=grid_spec,
        out_shape=jax.ShapeDtypeStruct((bsz, rows, HEAD_DIM), F32),
        compiler_params=_params(("parallel", "arbitrary"), vmem),
        name="sb_decode",
    )(page_flat, q_rows, k_new, v_new, bias_rows, *([cache_k] * pages_per_step), *([cache_v] * pages_per_step))


def _pool_mix(ubuf_ref, gm_ref, sc_ref, cat_ref, first_pos, tm):
    tok_width = ubuf_ref.shape[1]
    group = tok_width // len(POOL_WINDOWS)
    pos = first_pos + lax.broadcasted_iota(jnp.int32, (tm, 1), 0)
    for gi, win in enumerate(POOL_WINDOWS):
        cols = slice(gi * group, (gi + 1) * group)
        own = ubuf_ref[HALO:HALO + tm, cols]
        tot = own
        for back in range(1, win):
            tot = tot + ubuf_ref[HALO - back:HALO - back + tm, cols]
        count = jnp.minimum(pos, win).astype(F32)
        diff = tot / count - own
        mixed = jnp.dot(diff.astype(BF16), gm_ref[gi], preferred_element_type=F32) * sc_ref[:, cols]
        cat_ref[:, cols] = mixed.astype(BF16)


def _memory_attend(queries, mk_ref, mv_ref, cat_ref, mem_by_head):
    tok_width = cat_ref.shape[1] - MEM_WIDTH
    scale = HEAD_DIM ** -0.5
    if mem_by_head:
        mk_heads = pltpu.einshape("mhd->hmd", mk_ref[0].astype(BF16))
        mv_heads = pltpu.einshape("mhd->hmd", mv_ref[0].astype(BF16))
    for h in range(MEM_HEADS):
        cols = slice(h * HEAD_DIM, (h + 1) * HEAD_DIM)
        qh = queries(cols).astype(BF16)
        if mem_by_head:
            kh, vh = mk_heads[h], mv_heads[h]
        else:
            kh, vh = mk_ref[0, :, cols].astype(BF16), mv_ref[0, :, cols].astype(BF16)
        s = lax.dot_general(qh, kh, _NT, preferred_element_type=F32) * scale
        e = jnp.exp(s - jnp.max(s, axis=-1, keepdims=True))
        prob = e / jnp.sum(e, axis=-1, keepdims=True)
        oh = jnp.dot(prob.astype(BF16), vh, preferred_element_type=F32)
        cat_ref[:, tok_width + h * HEAD_DIM:tok_width + (h + 1) * HEAD_DIM] = oh.astype(BF16)


def _mixer_kernel(*refs, pool, n_prefix, halo_valid_from, mem_by_head):
    if pool:
        (u_ref, halo_ref, qm_ref, mk_ref, mv_ref, x_ref, g_ref, wo_ref, gm_ref, sc_ref,
         o_ref, cat_ref, ubuf_ref) = refs
    else:
        tok_ref, qm_ref, mk_ref, mv_ref, x_ref, g_ref, wo_ref, o_ref, cat_ref = refs
    t = pl.program_id(1)
    tm = x_ref.shape[1]
    tok_width = cat_ref.shape[1] - MEM_WIDTH

    if pool:
        ubuf_ref[0:HALO, :] = jnp.where(t >= halo_valid_from, halo_ref[0], 0.0)
        ubuf_ref[HALO:HALO + tm, :] = u_ref[0]
        _pool_mix(ubuf_ref, gm_ref, sc_ref, cat_ref, t * tm + n_prefix + 1, tm)
    else:
        cat_ref[:, 0:tok_width] = tok_ref[0].astype(BF16)
    _memory_attend(lambda cols: qm_ref[0, :, cols], mk_ref, mv_ref, cat_ref, mem_by_head)
    y = jnp.dot(cat_ref[...], wo_ref[...], preferred_element_type=F32)
    o_ref[0] = x_ref[0] + _rms(y, g_ref[1:2, :])


def _pool_layer_kernel(x_ref, g_ref, win_ref, mk_ref, mv_ref, wo_ref, gm_ref, sc_ref, o_ref, tail_ref,
                       cat_ref, ubuf_ref):
    t = pl.program_id(1)
    tm = x_ref.shape[1]
    tok_width = cat_ref.shape[1] - MEM_WIDTH

    @pl.when(t == 0)
    def _():
        ubuf_ref[0:HALO, :] = jnp.zeros((HALO, tok_width), F32)

    h = _rms(x_ref[0], g_ref[0:1, :]).astype(BF16)
    proj = jnp.dot(h, win_ref[...], preferred_element_type=F32)
    ubuf_ref[HALO:HALO + tm, :] = proj[:, 0:tok_width]
    _pool_mix(ubuf_ref, gm_ref, sc_ref, cat_ref, t * tm + 1, tm)
    _memory_attend(lambda cols: proj[:, tok_width + cols.start:tok_width + cols.stop], mk_ref, mv_ref, cat_ref, False)
    y = jnp.dot(cat_ref[...], wo_ref[...], preferred_element_type=F32)
    o_ref[0] = x_ref[0] + _rms(y, g_ref[1:2, :])
    tail = ubuf_ref[tm:tm + HALO, :]
    ubuf_ref[0:HALO, :] = tail
    tail_ref[0] = tail


def _pool_layer(x, g, w_in, w_out, layer, mk, mv, group_maps, pool_scale, *, tm):
    b, t, d = x.shape
    tok_width = d - MEM_WIDTH
    mem_len = mk.shape[1]
    _, ng, gw, _ = group_maps.shape
    x_spec = pl.BlockSpec((1, tm, d), lambda bb, i: (bb, i, 0))
    mem_spec = pl.BlockSpec((1, mem_len, MEM_WIDTH), lambda bb, i: (bb, 0, 0))
    weight_spec = pl.BlockSpec((None, d, d), lambda bb, i: (layer, 0, 0), pipeline_mode=pl.Buffered(1))
    vmem = (2 * d * d * 2 + 2 * (2 * tm * d * 4 + 4 * mem_len * MEM_WIDTH * 4 + ng * gw * gw * 2)
            + (HALO + tm) * tok_width * 4 + tm * d * 2 + 4 * tm * d * 4 + 6 * MIB)
    return pl.pallas_call(
        _pool_layer_kernel,
        grid=(b, t // tm),
        in_specs=[x_spec, pl.BlockSpec((2, d), lambda bb, i: (0, 0)), weight_spec, mem_spec, mem_spec, weight_spec,
                  pl.BlockSpec((None, ng, gw, gw), lambda bb, i: (layer, 0, 0, 0)),
                  pl.BlockSpec((1, tok_width), lambda bb, i: (0, 0))],
        out_specs=[x_spec, pl.BlockSpec((1, HALO, tok_width), lambda bb, i: (bb, 0, 0))],
        out_shape=[jax.ShapeDtypeStruct((b, t, d), F32), jax.ShapeDtypeStruct((b, HALO, tok_width), F32)],
        scratch_shapes=[pltpu.VMEM((tm, d), BF16), pltpu.VMEM((HALO + tm, tok_width), F32)],
        compiler_params=_params(("parallel", "arbitrary"), vmem),
        name="pool_layer",
    )(x, g, w_in, mk, mv, w_out, group_maps, pool_scale)


def _mixer_out(x, g, w_out, layer, qm_src, qm_block, mk, mv, *, tm, tok=None, pool_src=None, halo_src=None,
               halo_valid_from=0, n_prefix=0, group_maps=None, pool_scale=None):
    b, t, d = x.shape
    tok_width = d - MEM_WIDTH
    mem_len = mk.shape[1]
    pool = tok is None
    mem_by_head = mk.ndim == 4
    qm_spec = pl.BlockSpec((1, tm, MEM_WIDTH), lambda bb, i: (bb, i, qm_block))
    if mem_by_head:
        mem_spec = pl.BlockSpec((1, mem_len, MEM_HEADS, HEAD_DIM), lambda bb, i: (bb, 0, 0, 0))
    else:
        mem_spec = pl.BlockSpec((1, mem_len, MEM_WIDTH), lambda bb, i: (bb, 0, 0))
    x_spec = pl.BlockSpec((1, tm, d), lambda bb, i: (bb, i, 0))
    g_spec = pl.BlockSpec((2, d), lambda bb, i: (0, 0))
    wo_spec = pl.BlockSpec((None, d, d), lambda bb, i: (layer, 0, 0))
    scratch = [pltpu.VMEM((tm, d), BF16)]
    if pool:
        halo_blocks = tm // HALO
        if halo_src is pool_src:
            halo_map = lambda bb, i: (bb, jnp.maximum(i * halo_blocks - 1, 0), 0)
        else:
            halo_map = lambda bb, i: (bb, 0, 0)
        _, ng, gw, _ = group_maps.shape
        in_specs = [pl.BlockSpec((1, tm, tok_width), lambda bb, i: (bb, i, 0)),
                    pl.BlockSpec((1, HALO, tok_width), halo_map),
                    qm_spec, mem_spec, mem_spec, x_spec, g_spec, wo_spec,
                    pl.BlockSpec((None, ng, gw, gw), lambda bb, i: (layer, 0, 0, 0)),
                    pl.BlockSpec((1, tok_width), lambda bb, i: (0, 0))]
        args = (pool_src, halo_src, qm_src, mk, mv, x, g, w_out, group_maps, pool_scale)
        scratch.append(pltpu.VMEM((HALO + tm, tok_width), F32))
    else:
        in_specs = [pl.BlockSpec((1, tm, tok_width), lambda bb, i: (bb, i, 0)),
                    qm_spec, mem_spec, mem_spec, x_spec, g_spec, wo_spec]
        args = (tok, qm_src, mk, mv, x, g, w_out)
    vmem = (2 * (tm * tok_width * 4 + tm * MEM_WIDTH * 4 + 2 * tm * d * 4 + 4 * mem_len * MEM_WIDTH * 4 + d * d * 2)
            + (HALO + tm) * tok_width * 4 + tm * d * 2 + 3 * tm * d * 4 + 6 * MIB)
    return pl.pallas_call(
        functools.partial(_mixer_kernel, pool=pool, n_prefix=n_prefix, halo_valid_from=halo_valid_from,
                          mem_by_head=mem_by_head),
        grid=(b, t // tm),
        in_specs=in_specs,
        out_specs=x_spec,
        out_shape=jax.ShapeDtypeStruct((b, t, d), F32),
        scratch_shapes=scratch,
        compiler_params=_params(("parallel", "arbitrary"), vmem),
        name="mixer_pool" if pool else "mixer_sb",
    )(*args)


def kernel(x_prompt, x_sample, state_pool, cache_sb_k, cache_sb_v, cache_mem_k, cache_mem_v, page_table, mem_prompt, norm_ffn1, ffn1_w_gate_up, ffn1_w_down, norm_mix, norm_mem, mem_w_kv, pool_w_in, pool_group_maps, pool_scale, pool_w_out, sb_w_in, sb_logit_bias, sb_w_out, norm_ffn2, ffn2_w_gate_up, ffn2_w_down):
    b, t, d = x_prompt.shape
    bs, ts, _ = x_sample.shape
    depth = norm_ffn1.shape[0]
    mem_len = mem_prompt.shape[1]
    tok_width = d - MEM_WIDTH
    n_heads = tok_width // HEAD_DIM
    assert ts <= SAMPLE_ROWS

    tm_ffn, tf_ffn = 1024, 512
    tm_proj, tn_proj = 1024, 512
    tm_mix = 512
    ms = bs * SAMPLE_ROWS

    w_kv = mem_w_kv.astype(BF16)
    wp_in, wp_out, gmaps = pool_w_in.astype(BF16), pool_w_out.astype(BF16), pool_group_maps.astype(BF16)
    ws_in, ws_out = sb_w_in.astype(BF16), sb_w_out.astype(BF16)

    xp = x_prompt.reshape(b * t, d)
    xs = jnp.pad(x_sample, ((0, 0), (0, SAMPLE_ROWS - ts), (0, 0))).reshape(ms, d)
    mem2 = mem_prompt.reshape(b * mem_len, d)
    page_flat = page_table.reshape(-1).astype(jnp.int32)
    cache_k = jnp.transpose(cache_sb_k, (0, 1, 3, 2, 4))
    cache_v = jnp.transpose(cache_sb_v, (0, 1, 3, 2, 4))

    pool_p, pool_s, ks, vs, mkp, mvp = [], [], [], [], [], []
    qkv_p = None
    for i in range(depth):
        li = i // N_MIXERS
        xs, *w_bf = _ffn_half_casting(xs, norm_ffn1[i], ffn1_w_gate_up, ffn1_w_down, i, tf=tf_ffn)
        xp = _ffn_half(xp, norm_ffn1[i], *w_bf, tm=tm_ffn, tf=tf_ffn)

        g_mem = norm_mem[i][None]
        flat = lambda n: (n, F32, False)
        mk_p, mv_p = _norm_proj(mem2, g_mem, w_kv, i, 0, (flat(MEM_WIDTH), flat(MEM_WIDTH)), tm=tm_proj, tn=tn_proj)
        mk_p = mk_p.reshape(b, mem_len, MEM_WIDTH)
        mv_p = mv_p.reshape(b, mem_len, MEM_WIDTH)
        mkp.append(mk_p.reshape(b, mem_len, MEM_HEADS, HEAD_DIM))
        mvp.append(mv_p.reshape(b, mem_len, MEM_HEADS, HEAD_DIM))
        mk_s, mv_s = cache_mem_k[i], cache_mem_v[i]

        g_mix = norm_mix[i]
        g_in = g_mix[0:1]
        xp3 = xp.reshape(b, t, d)
        xs3 = xs.reshape(bs, SAMPLE_ROWS, d)
        if i % N_MIXERS == 0:
            scale = pool_scale[li][None]
            xp3, tail_p = _pool_layer(xp3, g_mix, wp_in, wp_out, li, mk_p, mv_p, gmaps, scale, tm=tm_mix)
            (proj_s,) = _norm_proj(xs, g_in, wp_in, li, 0, (flat(d),), tm=ms, tn=tn_proj)
            proj_s = proj_s.reshape(bs, SAMPLE_ROWS, d)
            qm_block = tok_width // MEM_WIDTH
            prefix = state_pool[li]
            n_prefix = prefix.shape[1]
            halo_s = jnp.pad(prefix, ((0, 0), (HALO - n_prefix, 0), (0, 0)))
            xs3 = _mixer_out(xs3, g_mix, wp_out, li, proj_s, qm_block, mk_s, mv_s, tm=SAMPLE_ROWS, pool_src=proj_s,
                             halo_src=halo_s, halo_valid_from=0, n_prefix=n_prefix, group_maps=gmaps,
                             pool_scale=scale)
            pool_p.append(tail_p[:, HALO - POOL_BUF:])
            u_ext = jnp.concatenate([prefix, proj_s[:, :ts, :tok_width]], axis=1)
            pool_s.append(u_ext[:, u_ext.shape[1] - POOL_BUF:])
        else:
            bias = sb_logit_bias[li].astype(F32)
            carried = {} if qkv_p is None else dict(enumerate(qkv_p))
            *qkv_p, qm_p = _norm_proj(
                xp, g_in, ws_in, li, 0,
                ((tok_width, BF16, True), (tok_width, F32, True), (tok_width, F32, True), flat(MEM_WIDTH)),
                tm=tm_proj, tn=tn_proj, seq_len=t, slots=(sb_w_in.shape[0], li, carried))
            o_p = _sb_prompt(*qkv_p, bias, li)
            xp3 = _mixer_out(xp3, g_mix, ws_out, li, qm_p.reshape(b, t, MEM_WIDTH), 0, mk_p, mv_p, tm=tm_mix,
                             tok=o_p)

            q_s, k_s, v_s, qm_s = _norm_proj(
                xs, g_in, ws_in, li, 0, (flat(tok_width), flat(tok_width), flat(tok_width), flat(MEM_WIDTH)),
                tm=ms, tn=tn_proj)
            k_s3 = k_s.reshape(bs, SAMPLE_ROWS, tok_width)
            v_s3 = v_s.reshape(bs, SAMPLE_ROWS, tok_width)
            q_rows = q_s.reshape(bs, SAMPLE_ROWS, n_heads, HEAD_DIM).transpose(0, 2, 1, 3)
            q_rows = q_rows.reshape(bs, n_heads * SAMPLE_ROWS, HEAD_DIM)
            bias_rows = jnp.repeat(bias, SAMPLE_ROWS)[:, None]
            o_rows = _sb_decode(q_rows, k_s3, v_s3, bias_rows, cache_k, cache_v, page_flat, li,
                                pages_per_step=8)
            o_s = o_rows.reshape(bs, n_heads, SAMPLE_ROWS, HEAD_DIM).transpose(0, 2, 1, 3)
            o_s = o_s.reshape(bs, SAMPLE_ROWS, tok_width)
            xs3 = _mixer_out(xs3, g_mix, ws_out, li, qm_s.reshape(bs, SAMPLE_ROWS, MEM_WIDTH), 0, mk_s, mv_s,
                             tm=SAMPLE_ROWS, tok=o_s)
            ks.append(k_s3[:, :ts].reshape(bs, ts, n_heads, HEAD_DIM))
            vs.append(v_s3[:, :ts].reshape(bs, ts, n_heads, HEAD_DIM))
        xp = xp3.reshape(b * t, d)
        xs = xs3.reshape(ms, d)

        xs, *w_bf = _ffn_half_casting(xs, norm_ffn2[i], ffn2_w_gate_up, ffn2_w_down, i, tf=tf_ffn)
        xp = _ffn_half(xp, norm_ffn2[i], *w_bf, tm=tm_ffn, tf=tf_ffn)

    y_p = xp.reshape(b, t, d)
    y_s = xs.reshape(bs, SAMPLE_ROWS, d)[:, :ts]
    k_p, v_p = (jnp.transpose(a, (0, 1, 3, 2, 4)) for a in qkv_p[1:])
    return (y_p, y_s, jnp.stack(pool_p), k_p, v_p, jnp.stack(mkp), jnp.stack(mvp),
            jnp.stack(pool_s), jnp.stack(ks), jnp.stack(vs))
```

```python
import functools

import jax
import jax.numpy as jnp
from jax import lax
from jax.experimental import pallas as pl
from jax.experimental.pallas import tpu as pltpu

F32 = jnp.float32
BF16 = jnp.bfloat16

HEAD_DIM = 128
MEM_HEADS = 4
MEM_WIDTH = MEM_HEADS * HEAD_DIM
POOL_WINDOWS = (2, 4, 8, 16)
POOL_BUF = max(POOL_WINDOWS) - 1
HALO = 16
N_MIXERS = 2
SB_BLOCK = 128
SB_QTILE = 512
SB_KCHUNK = 256
RMS_EPS = 1e-6
SAMPLE_ROWS = 8
LOG2E = 1.4426950408889634
MIB = 1024 * 1024
VMEM_CAP = 62 * MIB

_NT = (((1,), (1,)), ((), ()))


def _params(semantics, vmem_bytes):
    return pltpu.CompilerParams(dimension_semantics=semantics,
                                vmem_limit_bytes=int(min(vmem_bytes, VMEM_CAP)))


def _rms(x, g):
    ms = jnp.mean(x * x, axis=-1, keepdims=True)
    return x * lax.rsqrt(ms + RMS_EPS) * g


def _rms_rows(src_ref, gain, dst_ref, residual_ref=None, zero_ref=None):
    chunk = 16
    group = min(8, src_ref.shape[0] // chunk)
    assert src_ref.shape[0] % (group * chunk) == 0

    def body(c, carry):
        for s in range(group):
            rows = pl.ds(pl.multiple_of((c * group + s) * chunk, chunk), chunk)
            y = _rms(src_ref[rows, :], gain)
            if residual_ref is not None:
                y = residual_ref[rows, :] + y
            dst_ref[rows, :] = y.astype(dst_ref.dtype)
            if zero_ref is not None:
                zero_ref[rows, :] = jnp.zeros((chunk, zero_ref.shape[1]), zero_ref.dtype)
        return carry

    lax.fori_loop(0, src_ref.shape[0] // (group * chunk), body, 0)


def _suffix_matrix(n):
    r = lax.broadcasted_iota(jnp.int32, (n, n), 0)
    c = lax.broadcasted_iota(jnp.int32, (n, n), 1)
    return jnp.where(r > c, 1.0, 0.0).astype(BF16)


def _sb_weights(z2, vis, umat, carry):
    neg_abs = lax.bitcast_convert_type(lax.bitcast_convert_type(z2, jnp.uint32) | jnp.uint32(0x80000000), F32)
    sp = jnp.maximum(z2, 0.0) + jnp.log2(1.0 + jnp.exp2(neg_abs))
    if vis is not None:
        sp = jnp.where(vis, sp, 0.0)
    later = jnp.dot(sp.astype(BF16), umat, preferred_element_type=F32)
    w = jnp.exp2(z2 - sp - later - carry)
    if vis is not None:
        w = jnp.where(vis, w, 0.0)
    return w, carry + jnp.sum(sp, axis=-1, keepdims=True)


def _ffn_kernel(x_ref, g_ref, wg_ref, wu_ref, wd_ref, o_ref, *refs, acc_cols, emit_weights):
    h_ref, acc_ref = refs[-2:]
    j = pl.program_id(1)

    @pl.when(j == 0)
    def _():
        _rms_rows(x_ref, g_ref[0:1, :], h_ref, zero_ref=acc_ref)

    if emit_weights:
        for src, dst in zip((wg_ref, wu_ref, wd_ref), refs[:3]):
            dst[...] = src[...].astype(BF16)
        wg_ref, wu_ref, wd_ref = refs[:3]
    h = h_ref[...]
    gate = jnp.dot(h, wg_ref[...], preferred_element_type=F32)
    up = jnp.dot(h, wu_ref[...], preferred_element_type=F32)
    a = (gate * jax.nn.sigmoid(gate) * up).astype(BF16)
    for c0 in range(0, acc_ref.shape[1], acc_cols):
        cols = slice(c0, c0 + acc_cols)
        acc_ref[:, cols] += jnp.dot(a, wd_ref[:, cols], preferred_element_type=F32)

    @pl.when(j == pl.num_programs(1) - 1)
    def _():
        _rms_rows(acc_ref, 0.5 * g_ref[1:2, :], o_ref, residual_ref=x_ref)


def _ffn_half(x, g, w_gate, w_up, w_down, *, tm, tf):
    m, d = x.shape
    f = w_down.shape[0]
    vmem = 2 * (2 * tm * d * 4) + tm * d * (2 + 4) + 2 * 3 * d * tf * 2 + 2 * tm * tf * 4 + 2 * MIB
    return pl.pallas_call(
        functools.partial(_ffn_kernel, acc_cols=min(d, 512), emit_weights=False),
        grid=(m // tm, f // tf),
        in_specs=[pl.BlockSpec((tm, d), lambda i, j: (i, 0)),
                  pl.BlockSpec((2, d), lambda i, j: (0, 0)),
                  pl.BlockSpec((d, tf), lambda i, j: (0, j)),
                  pl.BlockSpec((d, tf), lambda i, j: (0, j)),
                  pl.BlockSpec((tf, d), lambda i, j: (j, 0))],
        out_specs=pl.BlockSpec((tm, d), lambda i, j: (i, 0)),
        out_shape=jax.ShapeDtypeStruct((m, d), F32),
        scratch_shapes=[pltpu.VMEM((tm, d), BF16), pltpu.VMEM((tm, d), F32)],
        compiler_params=_params(("parallel", "arbitrary"), vmem),
        name="ffn_half",
    )(x, g, w_gate, w_up, w_down)


def _ffn_half_casting(x, g, w_gu, w_d, layer, *, tf):
    m, d = x.shape
    f = w_d.shape[1]
    nf = f // tf
    vmem = 4 * m * d * 4 + 2 * 3 * d * tf * (4 + 2) + 3 * d * tf * 2 + 6 * m * tf * 4 + 4 * MIB
    return pl.pallas_call(
        functools.partial(_ffn_kernel, acc_cols=min(d, 512), emit_weights=True),
        grid=(1, nf),
        in_specs=[pl.BlockSpec((m, d), lambda i, j: (0, 0)),
                  pl.BlockSpec((2, d), lambda i, j: (0, 0)),
                  pl.BlockSpec((None, d, tf), lambda i, j: (layer, 0, j)),
                  pl.BlockSpec((None, d, tf), lambda i, j: (layer, 0, j + nf)),
                  pl.BlockSpec((None, tf, d), lambda i, j: (layer, j, 0))],
        out_specs=[pl.BlockSpec((m, d), lambda i, j: (0, 0)),
                   pl.BlockSpec((d, tf), lambda i, j: (0, j)),
                   pl.BlockSpec((d, tf), lambda i, j: (0, j)),
                   pl.BlockSpec((tf, d), lambda i, j: (j, 0))],
        out_shape=[jax.ShapeDtypeStruct((m, d), F32), jax.ShapeDtypeStruct((d, f), BF16),
                   jax.ShapeDtypeStruct((d, f), BF16), jax.ShapeDtypeStruct((f, d), BF16)],
        scratch_shapes=[pltpu.VMEM((m, d), BF16), pltpu.VMEM((m, d), F32)],
        compiler_params=_params(("arbitrary", "arbitrary"), vmem),
        name="ffn_half_casting",
    )(x, g, w_gu, w_gu, w_d)


def _norm_proj_kernel(x_ref, g_ref, w_ref, *refs, bounds, by_head, n_carried):
    o_refs, h_ref = refs[n_carried:-1], refs[-1]
    j = pl.program_id(1)

    @pl.when(j == 0)
    def _():
        h_ref[...] = _rms(x_ref[...], g_ref[...]).astype(BF16)

    for o_ref, (lo, hi), heads in zip(o_refs, bounds, by_head):
        @pl.when(jnp.logical_and(j >= lo, j < hi))
        def _(o_ref=o_ref, heads=heads):
            out = jnp.dot(h_ref[...], w_ref[...], preferred_element_type=F32).astype(o_ref.dtype)
            if heads:
                for hh in range(o_ref.shape[1]):
                    o_ref[0, hh] = out[:, hh * HEAD_DIM:(hh + 1) * HEAD_DIM]
            else:
                o_ref[...] = out


def _norm_proj(x, g, w, layer, col0, segments, *, tm, tn, seq_len=None, slots=None):
    m, d = x.shape
    assert col0 % tn == 0 and m % tm == 0 and all(n % tn == 0 for n, _, _ in segments)
    cb = col0 // tn
    n_slots, slot, carried = slots if slots is not None else (1, 0, {})
    bounds, lo = [], 0
    for n, _, _ in segments:
        bounds.append((lo, lo + n // tn))
        lo += n // tn
    out_specs, out_shapes, aliases, carried_args = [], [], {}, []
    for si, ((n, dtype, heads), (a, b)) in enumerate(zip(segments, bounds)):
        local = lambda j, a=a, b=b: jnp.clip(j - a, 0, b - a - 1)
        if heads:
            assert seq_len % tm == 0 and tn % HEAD_DIM == 0
            tiles = seq_len // tm
            out_specs.append(pl.BlockSpec((None, 1, tn // HEAD_DIM, tm, HEAD_DIM),
                                          lambda i, j, local=local: (slot, i // tiles, local(j), i % tiles, 0)))
            out_shapes.append(jax.ShapeDtypeStruct((n_slots, m // seq_len, n // HEAD_DIM, seq_len, HEAD_DIM), dtype))
            if si in carried:
                aliases[3 + len(carried_args)] = si
                carried_args.append(carried[si])
        else:
            out_specs.append(pl.BlockSpec((tm, tn), lambda i, j, local=local: (i, local(j))))
            out_shapes.append(jax.ShapeDtypeStruct((m, n), dtype))
    vmem = 2 * tm * d * 4 + tm * d * 2 + 2 * d * tn * 2 + (2 * len(segments) + 2) * tm * tn * 4 + 4 * MIB
    return pl.pallas_call(
        functools.partial(_norm_proj_kernel, bounds=tuple(bounds), by_head=tuple(s[2] for s in segments),
                          n_carried=len(carried_args)),
        grid=(m // tm, lo),
        in_specs=[pl.BlockSpec((tm, d), lambda i, j: (i, 0)),
                  pl.BlockSpec((1, d), lambda i, j: (0, 0)),
                  pl.BlockSpec((None, d, tn), lambda i, j: (layer, 0, j + cb))]
                 + [pl.BlockSpec(memory_space=pl.ANY)] * len(carried_args),
        out_specs=out_specs,
        out_shape=out_shapes,
        input_output_aliases=aliases,
        scratch_shapes=[pltpu.VMEM((tm, d), BF16)],
        compiler_params=_params(("parallel", "arbitrary"), vmem),
        name="norm_proj",
    )(x, g, w, *carried_args)


def _sb_prompt_kernel(bias_ref, q_ref, k_ref, v_ref, o_ref, kb_ref, vb_ref, u_ref):
    tq, kc = SB_QTILE, SB_KCHUNK
    kb_ref[...] = k_ref[0, 0].astype(BF16)
    vb_ref[...] = v_ref[0, 0].astype(BF16)
    u_ref[...] = _suffix_matrix(kc)
    bias = bias_ref[pl.program_id(1)] * LOG2E
    scale = HEAD_DIM ** -0.5 * LOG2E
    r = lax.broadcasted_iota(jnp.int32, (tq, kc), 0)
    c = lax.broadcasted_iota(jnp.int32, (tq, kc), 1)
    for i in range(q_ref.shape[2] // tq):
        q = q_ref[0, 0, i * tq:(i + 1) * tq, :]
        acc = jnp.zeros((tq, HEAD_DIM), F32)
        carry = jnp.zeros((tq, 1), F32)
        for j in range((i + 1) * tq // kc - 1, -1, -1):
            rows = slice(j * kc, (j + 1) * kc)
            vis = (c + (j * kc - i * tq) < r) if (j + 1) * kc > i * tq else None
            z = lax.dot_general(q, kb_ref[rows, :], _NT, preferred_element_type=F32) * scale + bias
            w, carry = _sb_weights(z, vis, u_ref[...], carry)
            acc = acc + jnp.dot(w.astype(BF16), vb_ref[rows, :], preferred_element_type=F32)
        o_ref[0, i * tq:(i + 1) * tq, :] = acc.astype(o_ref.dtype)


def _sb_prompt(q, k, v, bias, slot):
    _, b, nh, t, _ = q.shape
    assert t % SB_QTILE == 0 and SB_QTILE % SB_KCHUNK == 0
    head_spec = pl.BlockSpec((None, 1, 1, t, HEAD_DIM), lambda bb, h: (slot, bb, h, 0, 0))
    return pl.pallas_call(
        _sb_prompt_kernel,
        grid=(b, nh),
        in_specs=[pl.BlockSpec(memory_space=pltpu.SMEM), head_spec, head_spec, head_spec],
        out_specs=pl.BlockSpec((1, t, HEAD_DIM), lambda bb, h: (bb, 0, h)),
        out_shape=jax.ShapeDtypeStruct((b, t, nh * HEAD_DIM), BF16),
        scratch_shapes=[pltpu.VMEM((t, HEAD_DIM), BF16), pltpu.VMEM((t, HEAD_DIM), BF16),
                        pltpu.VMEM((SB_KCHUNK, SB_KCHUNK), BF16)],
        compiler_params=_params(("parallel", "parallel"), 32 * MIB),
        name="sb_prompt",
    )(bias, q, k, v)


def _sb_decode_kernel(pt_ref, q_ref, knew_ref, vnew_ref, bias_ref, *refs, pages_per_step, n_heads):
    del pt_ref
    k_refs = refs[:pages_per_step]
    v_refs = refs[pages_per_step:2 * pages_per_step]
    o_ref, acc_ref, car_ref = refs[2 * pages_per_step:]
    p = pl.program_id(1)
    rows = n_heads * SAMPLE_ROWS
    scale = HEAD_DIM ** -0.5 * LOG2E
    chunk = min(SB_KCHUNK, pages_per_step * SB_BLOCK)
    umat = _suffix_matrix(chunk)
    bias = bias_ref[...] * LOG2E
    q = q_ref[0]
    qh = [q[h * SAMPLE_ROWS:(h + 1) * SAMPLE_ROWS].astype(BF16) for h in range(n_heads)]

    def sweep(keys, values, n_keys, vis):
        zs = [lax.dot_general(qh[h], keys(h), _NT, preferred_element_type=F32) for h in range(n_heads)]
        z = jnp.concatenate(zs, axis=0) * scale + bias
        step = min(chunk, n_keys)
        carry = car_ref[...]
        ws = []
        for c0 in range(n_keys - step, -1, -step):
            w, carry = _sb_weights(z[:, c0:c0 + step], vis, umat[:step, :step], carry)
            ws.insert(0, w)
        w = jnp.concatenate(ws, axis=1)
        outs = [jnp.dot(w[h * SAMPLE_ROWS:(h + 1) * SAMPLE_ROWS].astype(BF16), values(h),
                        preferred_element_type=F32) for h in range(n_heads)]
        acc_ref[...] += jnp.concatenate(outs, axis=0)
        car_ref[...] = carry

    @pl.when(p == 0)
    def _():
        acc_ref[...] = jnp.zeros_like(acc_ref)
        car_ref[...] = jnp.zeros_like(car_ref)
        t = lax.broadcasted_iota(jnp.int32, (rows, SB_BLOCK), 0) % SAMPLE_ROWS
        s = lax.broadcasted_iota(jnp.int32, (rows, SB_BLOCK), 1)
        zeros = jnp.zeros((SB_BLOCK - SAMPLE_ROWS, HEAD_DIM), F32)

        def new_block(ref, h):
            head = ref[0, :, h * HEAD_DIM:(h + 1) * HEAD_DIM]
            return jnp.concatenate([head, zeros], axis=0).astype(BF16)

        sweep(functools.partial(new_block, knew_ref), functools.partial(new_block, vnew_ref), SB_BLOCK, s < t)

    def pages(page_refs, h):
        return jnp.concatenate([r[h].astype(BF16) for r in reversed(page_refs)], axis=0)

    sweep(functools.partial(pages, k_refs), functools.partial(pages, v_refs), pages_per_step * SB_BLOCK, None)

    @pl.when(p == pl.num_programs(1) - 1)
    def _():
        o_ref[0] = acc_ref[...]


def _sb_decode(q_rows, k_new, v_new, bias_rows, cache_k, cache_v, page_flat, layer, *, pages_per_step):
    bsz, rows, _ = q_rows.shape
    n_heads = rows // SAMPLE_ROWS
    width = n_heads * HEAD_DIM
    page = cache_k.shape[3]
    n_pages = page_flat.shape[0] // bsz
    assert page == SB_BLOCK and n_pages % pages_per_step == 0
    steps = n_pages // pages_per_step

    def page_map(c):
        return lambda b, p, pt: (layer, pt[b * n_pages + n_pages - 1 - (p * pages_per_step + c)], 0, 0, 0)

    def page_specs():
        return [pl.BlockSpec((None, None, n_heads, page, HEAD_DIM), page_map(c)) for c in range(pages_per_step)]

    seq_map = lambda b, p, pt: (b, 0, 0)
    grid_spec = pltpu.PrefetchScalarGridSpec(
        num_scalar_prefetch=1,
        grid=(bsz, steps),
        in_specs=[pl.BlockSpec((1, rows, HEAD_DIM), seq_map),
                  pl.BlockSpec((1, SAMPLE_ROWS, width), seq_map),
                  pl.BlockSpec((1, SAMPLE_ROWS, width), seq_map),
                  pl.BlockSpec((rows, 1), lambda b, p, pt: (0, 0))] + page_specs() + page_specs(),
        out_specs=pl.BlockSpec((1, rows, HEAD_DIM), seq_map),
        scratch_shapes=[pltpu.VMEM((rows, HEAD_DIM), F32), pltpu.VMEM((rows, 1), F32)],
    )
    vmem = 2 * 2 * pages_per_step * page * width * 4 + 12 * MIB
    return pl.pallas_call(
        functools.partial(_sb_decode_kernel, pages_per_step=pages_per_step, n_heads=n_heads),
        grid_spec=grid_spec,
        out_shape=jax.ShapeDtypeStruct((bsz, rows, HEAD_DIM), F32),
        compiler_params=_params(("parallel", "arbitrary"), vmem),
        name="sb_decode",
    )(page_flat, q_rows, k_new, v_new, bias_rows, *([cache_k] * pages_per_step), *([cache_v] * pages_per_step))


def _pool_mix(ubuf_ref, gm_ref, sc_ref, cat_ref, first_pos, tm):
    tok_width = ubuf_ref.shape[1]
    group = tok_width // len(POOL_WINDOWS)
    pos = first_pos + lax.broadcasted_iota(jnp.int32, (tm, 1), 0)
    for gi, win in enumerate(POOL_WINDOWS):
        cols = slice(gi * group, (gi + 1) * group)
        own = ubuf_ref[HALO:HALO + tm, cols]
        tot = own
        for back in range(1, win):
            tot = tot + ubuf_ref[HALO - back:HALO - back + tm, cols]
        count = jnp.minimum(pos, win).astype(F32)
        diff = tot / count - own
        mixed = jnp.dot(diff.astype(BF16), gm_ref[gi], preferred_element_type=F32) * sc_ref[:, cols]
        cat_ref[:, cols] = mixed.astype(BF16)


def _memory_attend(queries, mk_ref, mv_ref, cat_ref, mem_by_head):
    tok_width = cat_ref.shape[1] - MEM_WIDTH
    scale = HEAD_DIM ** -0.5
    if mem_by_head:
        mk_heads = pltpu.einshape("mhd->hmd", mk_ref[0].astype(BF16))
        mv_heads = pltpu.einshape("mhd->hmd", mv_ref[0].astype(BF16))
    for h in range(MEM_HEADS):
        cols = slice(h * HEAD_DIM, (h + 1) * HEAD_DIM)
        qh = queries(cols).astype(BF16)
        if mem_by_head:
            kh, vh = mk_heads[h], mv_heads[h]
        else:
            kh, vh = mk_ref[0, :, cols].astype(BF16), mv_ref[0, :, cols].astype(BF16)
        s = lax.dot_general(qh, kh, _NT, preferred_element_type=F32) * scale
        e = jnp.exp(s - jnp.max(s, axis=-1, keepdims=True))
        prob = e / jnp.sum(e, axis=-1, keepdims=True)
        oh = jnp.dot(prob.astype(BF16), vh, preferred_element_type=F32)
        cat_ref[:, tok_width + h * HEAD_DIM:tok_width + (h + 1) * HEAD_DIM] = oh.astype(BF16)


def _mixer_kernel(*refs, pool, n_prefix, halo_valid_from, mem_by_head, project_queries):
    if pool:
        (u_ref, halo_ref, qm_ref, mk_ref, mv_ref, x_ref, g_ref, wo_ref, gm_ref, sc_ref,
         o_ref, cat_ref, ubuf_ref) = refs
    else:
        tok_ref, qm_ref, mk_ref, mv_ref, x_ref, g_ref, wo_ref, o_ref, cat_ref = refs
    t = pl.program_id(1)
    tm = x_ref.shape[1]
    tok_width = cat_ref.shape[1] - MEM_WIDTH

    if pool:
        ubuf_ref[0:HALO, :] = jnp.where(t >= halo_valid_from, halo_ref[0], 0.0)
        ubuf_ref[HALO:HALO + tm, :] = u_ref[0]
        _pool_mix(ubuf_ref, gm_ref, sc_ref, cat_ref, t * tm + n_prefix + 1, tm)
    else:
        cat_ref[:, 0:tok_width] = tok_ref[0].astype(BF16)
    if project_queries:
        h = _rms(x_ref[0], g_ref[0:1, :]).astype(BF16)
        qm = jnp.dot(h, qm_ref[...], preferred_element_type=F32)
        queries = lambda cols: qm[:, cols]
    else:
        queries = lambda cols: qm_ref[0, :, cols]
    _memory_attend(queries, mk_ref, mv_ref, cat_ref, mem_by_head)
    y = jnp.dot(cat_ref[...], wo_ref[...], preferred_element_type=F32)
    o_ref[0] = x_ref[0] + _rms(y, g_ref[1:2, :])


def _pool_layer_kernel(x_ref, g_ref, win_ref, mk_ref, mv_ref, wo_ref, gm_ref, sc_ref, o_ref, tail_ref,
                       cat_ref, ubuf_ref):
    t = pl.program_id(1)
    tm = x_ref.shape[1]
    tok_width = cat_ref.shape[1] - MEM_WIDTH

    @pl.when(t == 0)
    def _():
        ubuf_ref[0:HALO, :] = jnp.zeros((HALO, tok_width), F32)

    h = _rms(x_ref[0], g_ref[0:1, :]).astype(BF16)
    proj = jnp.dot(h, win_ref[...], preferred_element_type=F32)
    ubuf_ref[HALO:HALO + tm, :] = proj[:, 0:tok_width]
    _pool_mix(ubuf_ref, gm_ref, sc_ref, cat_ref, t * tm + 1, tm)
    _memory_attend(lambda cols: proj[:, tok_width + cols.start:tok_width + cols.stop], mk_ref, mv_ref, cat_ref, False)
    y = jnp.dot(cat_ref[...], wo_ref[...], preferred_element_type=F32)
    o_ref[0] = x_ref[0] + _rms(y, g_ref[1:2, :])
    tail = ubuf_ref[tm:tm + HALO, :]
    ubuf_ref[0:HALO, :] = tail
    tail_ref[0] = tail


def _pool_layer(x, g, w_in, w_out, layer, mk, mv, group_maps, pool_scale, *, tm):
    b, t, d = x.shape
    tok_width = d - MEM_WIDTH
    mem_len = mk.shape[1]
    _, ng, gw, _ = group_maps.shape
    x_spec = pl.BlockSpec((1, tm, d), lambda bb, i: (bb, i, 0))
    mem_spec = pl.BlockSpec((1, mem_len, MEM_WIDTH), lambda bb, i: (bb, 0, 0))
    weight_spec = pl.BlockSpec((None, d, d), lambda bb, i: (layer, 0, 0), pipeline_mode=pl.Buffered(1))
    vmem = (2 * d * d * 2 + 2 * (2 * tm * d * 4 + 4 * mem_len * MEM_WIDTH * 4 + ng * gw * gw * 2)
            + (HALO + tm) * tok_width * 4 + tm * d * 2 + 4 * tm * d * 4 + 6 * MIB)
    return pl.pallas_call(
        _pool_layer_kernel,
        grid=(b, t // tm),
        in_specs=[x_spec, pl.BlockSpec((2, d), lambda bb, i: (0, 0)), weight_spec, mem_spec, mem_spec, weight_spec,
                  pl.BlockSpec((None, ng, gw, gw), lambda bb, i: (layer, 0, 0, 0)),
                  pl.BlockSpec((1, tok_width), lambda bb, i: (0, 0))],
        out_specs=[x_spec, pl.BlockSpec((1, HALO, tok_width), lambda bb, i: (bb, 0, 0))],
        out_shape=[jax.ShapeDtypeStruct((b, t, d), F32), jax.ShapeDtypeStruct((b, HALO, tok_width), F32)],
        scratch_shapes=[pltpu.VMEM((tm, d), BF16), pltpu.VMEM((HALO + tm, tok_width), F32)],
        compiler_params=_params(("parallel", "arbitrary"), vmem),
        name="pool_layer",
    )(x, g, w_in, mk, mv, w_out, group_maps, pool_scale)


def _mixer_out(x, g, w_out, layer, qm_src, qm_block, mk, mv, *, tm, tok=None, pool_src=None, halo_src=None,
               halo_valid_from=0, n_prefix=0, group_maps=None, pool_scale=None, project_queries=False):
    b, t, d = x.shape
    tok_width = d - MEM_WIDTH
    mem_len = mk.shape[1]
    pool = tok is None
    mem_by_head = mk.ndim == 4
    if project_queries:
        qm_spec = pl.BlockSpec((None, d, MEM_WIDTH), lambda bb, i: (layer, 0, qm_block))
    else:
        qm_spec = pl.BlockSpec((1, tm, MEM_WIDTH), lambda bb, i: (bb, i, qm_block))
    if mem_by_head:
        mem_spec = pl.BlockSpec((1, mem_len, MEM_HEADS, HEAD_DIM), lambda bb, i: (bb, 0, 0, 0))
    else:
        mem_spec = pl.BlockSpec((1, mem_len, MEM_WIDTH), lambda bb, i: (bb, 0, 0))
    x_spec = pl.BlockSpec((1, tm, d), lambda bb, i: (bb, i, 0))
    g_spec = pl.BlockSpec((2, d), lambda bb, i: (0, 0))
    wo_spec = pl.BlockSpec((None, d, d), lambda bb, i: (layer, 0, 0))
    scratch = [pltpu.VMEM((tm, d), BF16)]
    if pool:
        halo_blocks = tm // HALO
        if halo_src is pool_src:
            halo_map = lambda bb, i: (bb, jnp.maximum(i * halo_blocks - 1, 0), 0)
        else:
            halo_map = lambda bb, i: (bb, 0, 0)
        _, ng, gw, _ = group_maps.shape
        in_specs = [pl.BlockSpec((1, tm, tok_width), lambda bb, i: (bb, i, 0)),
                    pl.BlockSpec((1, HALO, tok_width), halo_map),
                    qm_spec, mem_spec, mem_spec, x_spec, g_spec, wo_spec,
                    pl.BlockSpec((None, ng, gw, gw), lambda bb, i: (layer, 0, 0, 0)),
                    pl.BlockSpec((1, tok_width), lambda bb, i: (0, 0))]
        args = (pool_src, halo_src, qm_src, mk, mv, x, g, w_out, group_maps, pool_scale)
        scratch.append(pltpu.VMEM((HALO + tm, tok_width), F32))
    else:
        in_specs = [pl.BlockSpec((1, tm, tok_width), lambda bb, i: (bb, i, 0)),
                    qm_spec, mem_spec, mem_spec, x_spec, g_spec, wo_spec]
        args = (tok, qm_src, mk, mv, x, g, w_out)
    vmem = (2 * (tm * tok_width * 4 + tm * MEM_WIDTH * 4 + 2 * tm * d * 4 + 4 * mem_len * MEM_WIDTH * 4 + d * d * 2)
            + (HALO + tm) * tok_width * 4 + tm * d * 2 + 3 * tm * d * 4 + 6 * MIB)
    return pl.pallas_call(
        functools.partial(_mixer_kernel, pool=pool, n_prefix=n_prefix, halo_valid_from=halo_valid_from,
                          mem_by_head=mem_by_head, project_queries=project_queries),
        grid=(b, t // tm),
        in_specs=in_specs,
        out_specs=x_spec,
        out_shape=jax.ShapeDtypeStruct((b, t, d), F32),
        scratch_shapes=scratch,
        compiler_params=_params(("parallel", "arbitrary"), vmem),
        name="mixer_pool" if pool else "mixer_sb",
    )(*args)


def kernel(x_prompt, x_sample, state_pool, cache_sb_k, cache_sb_v, cache_mem_k, cache_mem_v, page_table, mem_prompt, norm_ffn1, ffn1_w_gate_up, ffn1_w_down, norm_mix, norm_mem, mem_w_kv, pool_w_in, pool_group_maps, pool_scale, pool_w_out, sb_w_in, sb_logit_bias, sb_w_out, norm_ffn2, ffn2_w_gate_up, ffn2_w_down):
    b, t, d = x_prompt.shape
    bs, ts, _ = x_sample.shape
    depth = norm_ffn1.shape[0]
    mem_len = mem_prompt.shape[1]
    tok_width = d - MEM_WIDTH
    n_heads = tok_width // HEAD_DIM
    assert ts <= SAMPLE_ROWS

    tm_ffn, tf_ffn = 1024, 512
    tm_proj, tn_proj = 1024, 512
    tm_mix = 512
    ms = bs * SAMPLE_ROWS

    w_kv = mem_w_kv.astype(BF16)
    wp_in, wp_out, gmaps = pool_w_in.astype(BF16), pool_w_out.astype(BF16), pool_group_maps.astype(BF16)
    ws_in, ws_out = sb_w_in.astype(BF16), sb_w_out.astype(BF16)

    xp = x_prompt.reshape(b * t, d)
    xs = jnp.pad(x_sample, ((0, 0), (0, SAMPLE_ROWS - ts), (0, 0))).reshape(ms, d)
    mem2 = mem_prompt.reshape(b * mem_len, d)
    page_flat = page_table.reshape(-1).astype(jnp.int32)
    cache_k = jnp.transpose(cache_sb_k, (0, 1, 3, 2, 4))
    cache_v = jnp.transpose(cache_sb_v, (0, 1, 3, 2, 4))

    pool_p, pool_s, ks, vs, mkp, mvp = [], [], [], [], [], []
    qkv_p = None
    for i in range(depth):
        li = i // N_MIXERS
        xs, *w_bf = _ffn_half_casting(xs, norm_ffn1[i], ffn1_w_gate_up, ffn1_w_down, i, tf=tf_ffn)
        xp = _ffn_half(xp, norm_ffn1[i], *w_bf, tm=tm_ffn, tf=tf_ffn)

        g_mem = norm_mem[i][None]
        flat = lambda n: (n, F32, False)
        mk_p, mv_p = _norm_proj(mem2, g_mem, w_kv, i, 0, (flat(MEM_WIDTH), flat(MEM_WIDTH)), tm=tm_proj, tn=tn_proj)
        mk_p = mk_p.reshape(b, mem_len, MEM_WIDTH)
        mv_p = mv_p.reshape(b, mem_len, MEM_WIDTH)
        mkp.append(mk_p.reshape(b, mem_len, MEM_HEADS, HEAD_DIM))
        mvp.append(mv_p.reshape(b, mem_len, MEM_HEADS, HEAD_DIM))
        mk_s, mv_s = cache_mem_k[i], cache_mem_v[i]

        g_mix = norm_mix[i]
        g_in = g_mix[0:1]
        xp3 = xp.reshape(b, t, d)
        xs3 = xs.reshape(bs, SAMPLE_ROWS, d)
        if i % N_MIXERS == 0:
            scale = pool_scale[li][None]
            xp3, tail_p = _pool_layer(xp3, g_mix, wp_in, wp_out, li, mk_p, mv_p, gmaps, scale, tm=tm_mix)
            (proj_s,) = _norm_proj(xs, g_in, wp_in, li, 0, (flat(d),), tm=ms, tn=tn_proj)
            proj_s = proj_s.reshape(bs, SAMPLE_ROWS, d)
            qm_block = tok_width // MEM_WIDTH
            prefix = state_pool[li]
            n_prefix = prefix.shape[1]
            halo_s = jnp.pad(prefix, ((0, 0), (HALO - n_prefix, 0), (0, 0)))
            xs3 = _mixer_out(xs3, g_mix, wp_out, li, proj_s, qm_block, mk_s, mv_s, tm=SAMPLE_ROWS, pool_src=proj_s,
                             halo_src=halo_s, halo_valid_from=0, n_prefix=n_prefix, group_maps=gmaps,
                             pool_scale=scale)
            pool_p.append(tail_p[:, HALO - POOL_BUF:])
            u_ext = jnp.concatenate([prefix, proj_s[:, :ts, :tok_width]], axis=1)
            pool_s.append(u_ext[:, u_ext.shape[1] - POOL_BUF:])
        else:
            bias = sb_logit_bias[li].astype(F32)
            carried = {} if qkv_p is None else dict(enumerate(qkv_p))
            qkv_p = _norm_proj(
                xp, g_in, ws_in, li, 0, ((tok_width, BF16, True), (tok_width, F32, True), (tok_width, F32, True)),
                tm=tm_proj, tn=tn_proj, seq_len=t, slots=(sb_w_in.shape[0], li, carried))
            o_p = _sb_prompt(*qkv_p, bias, li)
            xp3 = _mixer_out(xp3, g_mix, ws_out, li, ws_in, 3 * tok_width // MEM_WIDTH, mk_p, mv_p, tm=tm_mix,
                             tok=o_p, project_queries=True)

            q_s, k_s, v_s, qm_s = _norm_proj(
                xs, g_in, ws_in, li, 0, (flat(tok_width), flat(tok_width), flat(tok_width), flat(MEM_WIDTH)),
                tm=ms, tn=tn_proj)
            k_s3 = k_s.reshape(bs, SAMPLE_ROWS, tok_width)
            v_s3 = v_s.reshape(bs, SAMPLE_ROWS, tok_width)
            q_rows = q_s.reshape(bs, SAMPLE_ROWS, n_heads, HEAD_DIM).transpose(0, 2, 1, 3)
            q_rows = q_rows.reshape(bs, n_heads * SAMPLE_ROWS, HEAD_DIM)
            bias_rows = jnp.repeat(bias, SAMPLE_ROWS)[:, None]
            o_rows = _sb_decode(q_rows, k_s3, v_s3, bias_rows, cache_k, cache_v, page_flat, li,
                                pages_per_step=8)
            o_s = o_rows.reshape(bs, n_heads, SAMPLE_ROWS, HEAD_DIM).transpose(0, 2, 1, 3)
            o_s = o_s.reshape(bs, SAMPLE_ROWS, tok_width)
            xs3 = _mixer_out(xs3, g_mix, ws_out, li, qm_s.reshape(bs, SAMPLE_ROWS, MEM_WIDTH), 0, mk_s, mv_s,
                             tm=SAMPLE_ROWS, tok=o_s)
            ks.append(k_s3[:, :ts].reshape(bs, ts, n_heads, HEAD_DIM))
            vs.append(v_s3[:, :ts].reshape(bs, ts, n_heads, HEAD_DIM))
        xp = xp3.reshape(b * t, d)
        xs = xs3.reshape(ms, d)

        xs, *w_bf = _ffn_half_casting(xs, norm_ffn2[i], ffn2_w_gate_up, ffn2_w_down, i, tf=tf_ffn)
        xp = _ffn_half(xp, norm_ffn2[i], *w_bf, tm=tm_ffn, tf=tf_ffn)

    y_p = xp.reshape(b, t, d)
    y_s = xs.reshape(bs, SAMPLE_ROWS, d)[:, :ts]
    k_p, v_p = (jnp.transpose(a, (0, 1, 3, 2, 4)) for a in qkv_p[1:])
    return (y_p, y_s, jnp.stack(pool_p), k_p, v_p, jnp.stack(mkp), jnp.stack(mvp),
            jnp.stack(pool_s), jnp.stack(ks), jnp.stack(vs))
```

```python
import functools

import jax
import jax.numpy as jnp
from jax import lax
from jax.experimental import pallas as pl
from jax.experimental.pallas import tpu as pltpu

F32 = jnp.float32
BF16 = jnp.bfloat16

HEAD_DIM = 128
MEM_HEADS = 4
MEM_WIDTH = MEM_HEADS * HEAD_DIM
POOL_WINDOWS = (2, 4, 8, 16)
POOL_BUF = max(POOL_WINDOWS) - 1
HALO = 16
N_MIXERS = 2
SB_BLOCK = 128
SB_QTILE = 512
SB_KCHUNK = 256
RMS_EPS = 1e-6
SAMPLE_ROWS = 8
LOG2E = 1.4426950408889634
MIB = 1024 * 1024
VMEM_CAP = 62 * MIB

_NT = (((1,), (1,)), ((), ()))


def _params(semantics, vmem_bytes):
    return pltpu.CompilerParams(dimension_semantics=semantics,
                                vmem_limit_bytes=int(min(vmem_bytes, VMEM_CAP)))


def _rms(x, g):
    ms = jnp.mean(x * x, axis=-1, keepdims=True)
    return x * lax.rsqrt(ms + RMS_EPS) * g


def _rms_rows(src_ref, gain, dst_ref, residual_ref=None, zero_ref=None):
    chunk = 16
    group = min(8, src_ref.shape[0] // chunk)
    assert src_ref.shape[0] % (group * chunk) == 0

    def body(c, carry):
        for s in range(group):
            rows = pl.ds(pl.multiple_of((c * group + s) * chunk, chunk), chunk)
            y = _rms(src_ref[rows, :], gain)
            if residual_ref is not None:
                y = residual_ref[rows, :] + y
            dst_ref[rows, :] = y.astype(dst_ref.dtype)
            if zero_ref is not None:
                zero_ref[rows, :] = jnp.zeros((chunk, zero_ref.shape[1]), zero_ref.dtype)
        return carry

    lax.fori_loop(0, src_ref.shape[0] // (group * chunk), body, 0)


def _suffix_matrix(n):
    r = lax.broadcasted_iota(jnp.int32, (n, n), 0)
    c = lax.broadcasted_iota(jnp.int32, (n, n), 1)
    return jnp.where(r > c, 1.0, 0.0).astype(BF16)


def _sb_weights(z2, vis, umat, carry):
    neg_abs = lax.bitcast_convert_type(lax.bitcast_convert_type(z2, jnp.uint32) | jnp.uint32(0x80000000), F32)
    sp = jnp.maximum(z2, 0.0) + jnp.log2(1.0 + jnp.exp2(neg_abs))
    if vis is not None:
        sp = jnp.where(vis, sp, 0.0)
    later = jnp.dot(sp.astype(BF16), umat, preferred_element_type=F32)
    w = jnp.exp2(z2 - sp - later - carry)
    if vis is not None:
        w = jnp.where(vis, w, 0.0)
    return w, carry + jnp.sum(sp, axis=-1, keepdims=True)


def _ffn_kernel(x_ref, g_ref, wg_ref, wu_ref, wd_ref, o_ref, *refs, acc_cols, emit_weights):
    h_ref, acc_ref = refs[-2:]
    j = pl.program_id(1)

    @pl.when(j == 0)
    def _():
        _rms_rows(x_ref, g_ref[0:1, :], h_ref, zero_ref=acc_ref)

    if emit_weights:
        for src, dst in zip((wg_ref, wu_ref, wd_ref), refs[:3]):
            dst[...] = src[...].astype(BF16)
        wg_ref, wu_ref, wd_ref = refs[:3]
    h = h_ref[...]
    gate = jnp.dot(h, wg_ref[...], preferred_element_type=F32)
    up = jnp.dot(h, wu_ref[...], preferred_element_type=F32)
    a = (gate * jax.nn.sigmoid(gate) * up).astype(BF16)
    for c0 in range(0, acc_ref.shape[1], acc_cols):
        cols = slice(c0, c0 + acc_cols)
        acc_ref[:, cols] += jnp.dot(a, wd_ref[:, cols], preferred_element_type=F32)

    @pl.when(j == pl.num_programs(1) - 1)
    def _():
        _rms_rows(acc_ref, 0.5 * g_ref[1:2, :], o_ref, residual_ref=x_ref)


def _ffn_half(x, g, w_gate, w_up, w_down, *, tm, tf):
    m, d = x.shape
    f = w_down.shape[0]
    vmem = 2 * (2 * tm * d * 4) + tm * d * (2 + 4) + 2 * 3 * d * tf * 2 + 2 * tm * tf * 4 + 2 * MIB
    return pl.pallas_call(
        functools.partial(_ffn_kernel, acc_cols=min(d, 512), emit_weights=False),
        grid=(m // tm, f // tf),
        in_specs=[pl.BlockSpec((tm, d), lambda i, j: (i, 0)),
                  pl.BlockSpec((2, d), lambda i, j: (0, 0)),
                  pl.BlockSpec((d, tf), lambda i, j: (0, j)),
                  pl.BlockSpec((d, tf), lambda i, j: (0, j)),
                  pl.BlockSpec((tf, d), lambda i, j: (j, 0))],
        out_specs=pl.BlockSpec((tm, d), lambda i, j: (i, 0)),
        out_shape=jax.ShapeDtypeStruct((m, d), F32),
        scratch_shapes=[pltpu.VMEM((tm, d), BF16), pltpu.VMEM((tm, d), F32)],
        compiler_params=_params(("parallel", "arbitrary"), vmem),
        name="ffn_half",
    )(x, g, w_gate, w_up, w_down)


def _ffn_half_casting(x, g, w_gu, w_d, layer, *, tf):
    m, d = x.shape
    f = w_d.shape[1]
    nf = f // tf
    vmem = 4 * m * d * 4 + 2 * 3 * d * tf * (4 + 2) + 3 * d * tf * 2 + 6 * m * tf * 4 + 4 * MIB
    return pl.pallas_call(
        functools.partial(_ffn_kernel, acc_cols=min(d, 512), emit_weights=True),
        grid=(1, nf),
        in_specs=[pl.BlockSpec((m, d), lambda i, j: (0, 0)),
                  pl.BlockSpec((2, d), lambda i, j: (0, 0)),
                  pl.BlockSpec((None, d, tf), lambda i, j: (layer, 0, j)),
                  pl.BlockSpec((None, d, tf), lambda i, j: (layer, 0, j + nf)),
                  pl.BlockSpec((None, tf, d), lambda i, j: (layer, j, 0))],
        out_specs=[pl.BlockSpec((m, d), lambda i, j: (0, 0)),
                   pl.BlockSpec((d, tf), lambda i, j: (0, j)),
                   pl.BlockSpec((d, tf), lambda i, j: (0, j)),
                   pl.BlockSpec((tf, d), lambda i, j: (j, 0))],
        out_shape=[jax.ShapeDtypeStruct((m, d), F32), jax.ShapeDtypeStruct((d, f), BF16),
                   jax.ShapeDtypeStruct((d, f), BF16), jax.ShapeDtypeStruct((f, d), BF16)],
        scratch_shapes=[pltpu.VMEM((m, d), BF16), pltpu.VMEM((m, d), F32)],
        compiler_params=_params(("arbitrary", "arbitrary"), vmem),
        name="ffn_half_casting",
    )(x, g, w_gu, w_gu, w_d)


def _norm_proj_kernel(x_ref, g_ref, w_ref, *refs, bounds, by_head, n_carried):
    o_refs, h_ref = refs[n_carried:-1], refs[-1]
    j = pl.program_id(1)

    @pl.when(j == 0)
    def _():
        h_ref[...] = _rms(x_ref[...], g_ref[...]).astype(BF16)

    for o_ref, (lo, hi), heads in zip(o_refs, bounds, by_head):
        @pl.when(jnp.logical_and(j >= lo, j < hi))
        def _(o_ref=o_ref, heads=heads):
            out = jnp.dot(h_ref[...], w_ref[...], preferred_element_type=F32).astype(o_ref.dtype)
            if heads:
                for hh in range(o_ref.shape[1]):
                    o_ref[0, hh] = out[:, hh * HEAD_DIM:(hh + 1) * HEAD_DIM]
            else:
                o_ref[...] = out


def _norm_proj(x, g, w, layer, col0, segments, *, tm, tn, seq_len=None, slots=None):
    m, d = x.shape
    assert col0 % tn == 0 and m % tm == 0 and all(n % tn == 0 for n, _, _ in segments)
    cb = col0 // tn
    n_slots, slot, carried = slots if slots is not None else (1, 0, {})
    bounds, lo = [], 0
    for n, _, _ in segments:
        bounds.append((lo, lo + n // tn))
        lo += n // tn
    out_specs, out_shapes, aliases, carried_args = [], [], {}, []
    for si, ((n, dtype, heads), (a, b)) in enumerate(zip(segments, bounds)):
        local = lambda j, a=a, b=b: jnp.clip(j - a, 0, b - a - 1)
        if heads:
            assert seq_len % tm == 0 and tn % HEAD_DIM == 0
            tiles = seq_len // tm
            out_specs.append(pl.BlockSpec((None, 1, tn // HEAD_DIM, tm, HEAD_DIM),
                                          lambda i, j, local=local: (slot, i // tiles, local(j), i % tiles, 0)))
            out_shapes.append(jax.ShapeDtypeStruct((n_slots, m // seq_len, n // HEAD_DIM, seq_len, HEAD_DIM), dtype))
            if si in carried:
                aliases[3 + len(carried_args)] = si
                carried_args.append(carried[si])
        else:
            out_specs.append(pl.BlockSpec((tm, tn), lambda i, j, local=local: (i, local(j))))
            out_shapes.append(jax.ShapeDtypeStruct((m, n), dtype))
    vmem = 2 * tm * d * 4 + tm * d * 2 + 2 * d * tn * 2 + (2 * len(segments) + 2) * tm * tn * 4 + 4 * MIB
    return pl.pallas_call(
        functools.partial(_norm_proj_kernel, bounds=tuple(bounds), by_head=tuple(s[2] for s in segments),
                          n_carried=len(carried_args)),
        grid=(m // tm, lo),
        in_specs=[pl.BlockSpec((tm, d), lambda i, j: (i, 0)),
                  pl.BlockSpec((1, d), lambda i, j: (0, 0)),
                  pl.BlockSpec((None, d, tn), lambda i, j: (layer, 0, j + cb))]
                 + [pl.BlockSpec(memory_space=pl.ANY)] * len(carried_args),
        out_specs=out_specs,
        out_shape=out_shapes,
        input_output_aliases=aliases,
        scratch_shapes=[pltpu.VMEM((tm, d), BF16)],
        compiler_params=_params(("parallel", "arbitrary"), vmem),
        name="norm_proj",
    )(x, g, w, *carried_args)


def _sb_prompt_kernel(bias_ref, q_ref, k_ref, v_ref, o_ref, kb_ref, vb_ref, u_ref):
    tq, kc = SB_QTILE, SB_KCHUNK
    kb_ref[...] = k_ref[0, 0].astype(BF16)
    vb_ref[...] = v_ref[0, 0].astype(BF16)
    u_ref[...] = _suffix_matrix(kc)
    bias = bias_ref[pl.program_id(1)] * LOG2E
    scale = HEAD_DIM ** -0.5 * LOG2E
    r = lax.broadcasted_iota(jnp.int32, (tq, kc), 0)
    c = lax.broadcasted_iota(jnp.int32, (tq, kc), 1)
    for i in range(q_ref.shape[2] // tq):
        q = q_ref[0, 0, i * tq:(i + 1) * tq, :]
        acc = jnp.zeros((tq, HEAD_DIM), F32)
        carry = jnp.zeros((tq, 1), F32)
        for j in range((i + 1) * tq // kc - 1, -1, -1):
            rows = slice(j * kc, (j + 1) * kc)
            vis = (c + (j * kc - i * tq) < r) if (j + 1) * kc > i * tq else None
            z = lax.dot_general(q, kb_ref[rows, :], _NT, preferred_element_type=F32) * scale + bias
            w, carry = _sb_weights(z, vis, u_ref[...], carry)
            acc = acc + jnp.dot(w.astype(BF16), vb_ref[rows, :], preferred_element_type=F32)
        o_ref[0, i * tq:(i + 1) * tq, :] = acc.astype(o_ref.dtype)


def _sb_prompt(q, k, v, bias, slot):
    _, b, nh, t, _ = q.shape
    assert t % SB_QTILE == 0 and SB_QTILE % SB_KCHUNK == 0
    head_spec = pl.BlockSpec((None, 1, 1, t, HEAD_DIM), lambda bb, h: (slot, bb, h, 0, 0))
    return pl.pallas_call(
        _sb_prompt_kernel,
        grid=(b, nh),
        in_specs=[pl.BlockSpec(memory_space=pltpu.SMEM), head_spec, head_spec, head_spec],
        out_specs=pl.BlockSpec((1, t, HEAD_DIM), lambda bb, h: (bb, 0, h)),
        out_shape=jax.ShapeDtypeStruct((b, t, nh * HEAD_DIM), BF16),
        scratch_shapes=[pltpu.VMEM((t, HEAD_DIM), BF16), pltpu.VMEM((t, HEAD_DIM), BF16),
                        pltpu.VMEM((SB_KCHUNK, SB_KCHUNK), BF16)],
        compiler_params=_params(("parallel", "parallel"), 32 * MIB),
        name="sb_prompt",
    )(bias, q, k, v)


def _sb_decode_kernel(pt_ref, q_ref, knew_ref, vnew_ref, bias_ref, *refs, pages_per_step, n_heads):
    del pt_ref
    k_refs = refs[:pages_per_step]
    v_refs = refs[pages_per_step:2 * pages_per_step]
    o_ref, acc_ref, car_ref = refs[2 * pages_per_step:]
    p = pl.program_id(1)
    rows = n_heads * SAMPLE_ROWS
    scale = HEAD_DIM ** -0.5 * LOG2E
    chunk = min(SB_KCHUNK, pages_per_step * SB_BLOCK)
    umat = _suffix_matrix(chunk)
    bias = bias_ref[...] * LOG2E
    q = q_ref[0]
    qh = [q[h * SAMPLE_ROWS:(h + 1) * SAMPLE_ROWS].astype(BF16) for h in range(n_heads)]

    def sweep(keys, values, n_keys, vis):
        zs = [lax.dot_general(qh[h], keys(h), _NT, preferred_element_type=F32) for h in range(n_heads)]
        z = jnp.concatenate(zs, axis=0) * scale + bias
        step = min(chunk, n_keys)
        carry = car_ref[...]
        ws = []
        for c0 in range(n_keys - step, -1, -step):
            w, carry = _sb_weights(z[:, c0:c0 + step], vis, umat[:step, :step], carry)
            ws.insert(0, w)
        w = jnp.concatenate(ws, axis=1)
        outs = [jnp.dot(w[h * SAMPLE_ROWS:(h + 1) * SAMPLE_ROWS].astype(BF16), values(h),
                        preferred_element_type=F32) for h in range(n_heads)]
        acc_ref[...] += jnp.concatenate(outs, axis=0)
        car_ref[...] = carry

    @pl.when(p == 0)
    def _():
        acc_ref[...] = jnp.zeros_like(acc_ref)
        car_ref[...] = jnp.zeros_like(car_ref)
        t = lax.broadcasted_iota(jnp.int32, (rows, SB_BLOCK), 0) % SAMPLE_ROWS
        s = lax.broadcasted_iota(jnp.int32, (rows, SB_BLOCK), 1)
        zeros = jnp.zeros((SB_BLOCK - SAMPLE_ROWS, HEAD_DIM), F32)

        def new_block(ref, h):
            head = ref[0, :, h * HEAD_DIM:(h + 1) * HEAD_DIM]
            return jnp.concatenate([head, zeros], axis=0).astype(BF16)

        sweep(functools.partial(new_block, knew_ref), functools.partial(new_block, vnew_ref), SB_BLOCK, s < t)

    def pages(page_refs, h):
        return jnp.concatenate([r[h].astype(BF16) for r in reversed(page_refs)], axis=0)

    sweep(functools.partial(pages, k_refs), functools.partial(pages, v_refs), pages_per_step * SB_BLOCK, None)

    @pl.when(p == pl.num_programs(1) - 1)
    def _():
        o_ref[0] = acc_ref[...]


def _sb_decode(q_rows, k_new, v_new, bias_rows, cache_k, cache_v, page_flat, layer, *, pages_per_step):
    bsz, rows, _ = q_rows.shape
    n_heads = rows // SAMPLE_ROWS
    width = n_heads * HEAD_DIM
    page = cache_k.shape[3]
    n_pages = page_flat.shape[0] // bsz
    assert page == SB_BLOCK and n_pages % pages_per_step == 0
    steps = n_pages // pages_per_step

    def page_map(c):
        return lambda b, p, pt: (layer, pt[b * n_pages + n_pages - 1 - (p * pages_per_step + c)], 0, 0, 0)

    def page_specs():
        return [pl.BlockSpec((None, None, n_heads, page, HEAD_DIM), page_map(c)) for c in range(pages_per_step)]

    seq_map = lambda b, p, pt: (b, 0, 0)
    grid_spec = pltpu.PrefetchScalarGridSpec(
        num_scalar_prefetch=1,
        grid=(bsz, steps),
        in_specs=[pl.BlockSpec((1, rows, HEAD_DIM), seq_map),
                  pl.BlockSpec((1, SAMPLE_ROWS, width), seq_map),
                  pl.BlockSpec((1, SAMPLE_ROWS, width), seq_map),
                  pl.BlockSpec((rows, 1), lambda b, p, pt: (0, 0))] + page_specs() + page_specs(),
        out_specs=pl.BlockSpec((1, rows, HEAD_DIM), seq_map),
        scratch_shapes=[pltpu.VMEM((rows, HEAD_DIM), F32), pltpu.VMEM((rows, 1), F32)],
    )
    vmem = 2 * 2 * pages_per_step * page * width * 4 + 12 * MIB
    return pl.pallas_call(
        functools.partial(_sb_decode_kernel, pages_per_step=pages_per_step, n_heads=n_heads),
        grid_spec=grid_spec,
        out_shape=jax.ShapeDtypeStruct((bsz, rows, HEAD_DIM), F32),
        compiler_params=_params(("parallel", "arbitrary"), vmem),
        name="sb_decode",
    )(page_flat, q_rows, k_new, v_new, bias_rows, *([cache_k] * pages_per_step), *([cache_v] * pages_per_step))


def _pool_mix(ubuf_ref, gm_ref, sc_ref, cat_ref, first_pos, tm):
    tok_width = ubuf_ref.shape[1]
    group = tok_width // len(POOL_WINDOWS)
    pos = first_pos + lax.broadcasted_iota(jnp.int32, (tm, 1), 0)
    for gi, win in enumerate(POOL_WINDOWS):
        cols = slice(gi * group, (gi + 1) * group)
        own = ubuf_ref[HALO:HALO + tm, cols]
        tot = own
        for back in range(1, win):
            tot = tot + ubuf_ref[HALO - back:HALO - back + tm, cols]
        count = jnp.minimum(pos, win).astype(F32)
        diff = tot / count - own
        mixed = jnp.dot(diff.astype(BF16), gm_ref[gi], preferred_element_type=F32) * sc_ref[:, cols]
        cat_ref[:, cols] = mixed.astype(cat_ref.dtype)


def _memory_attend(queries, mk_ref, mv_ref, cat_ref, mem_by_head):
    tok_width = cat_ref.shape[1] - MEM_WIDTH
    scale = HEAD_DIM ** -0.5
    if mem_by_head:
        mk_heads = pltpu.einshape("mhd->hmd", mk_ref[...].astype(BF16))
        mv_heads = pltpu.einshape("mhd->hmd", mv_ref[...].astype(BF16))
    for h in range(MEM_HEADS):
        cols = slice(h * HEAD_DIM, (h + 1) * HEAD_DIM)
        qh = queries(cols).astype(BF16)
        if mem_by_head:
            kh, vh = mk_heads[h], mv_heads[h]
        else:
            kh, vh = mk_ref[:, cols].astype(BF16), mv_ref[:, cols].astype(BF16)
        s = lax.dot_general(qh, kh, _NT, preferred_element_type=F32) * scale
        e = jnp.exp(s - jnp.max(s, axis=-1, keepdims=True))
        prob = e / jnp.sum(e, axis=-1, keepdims=True)
        oh = jnp.dot(prob.astype(BF16), vh, preferred_element_type=F32)
        cat_ref[:, tok_width + h * HEAD_DIM:tok_width + (h + 1) * HEAD_DIM] = oh.astype(cat_ref.dtype)


def _mixer_kernel(*refs, pool, n_prefix, halo_valid_from, mem_by_head):
    if pool:
        (u_ref, halo_ref, qm_ref, mk_ref, mv_ref, x_ref, g_ref, wo_ref, gm_ref, sc_ref,
         o_ref, cat_ref, ubuf_ref) = refs
    else:
        tok_ref, qm_ref, mk_ref, mv_ref, x_ref, g_ref, wo_ref, o_ref, cat_ref = refs
    t = pl.program_id(1)
    nb, tm, d = x_ref.shape
    tok_width = cat_ref.shape[1] - MEM_WIDTH

    for s in range(nb):
        cat_s = cat_ref.at[s * tm:(s + 1) * tm]
        if pool:
            ubuf_ref[0:HALO, :] = jnp.where(t >= halo_valid_from, halo_ref[s], 0.0)
            ubuf_ref[HALO:HALO + tm, :] = u_ref[s]
            _pool_mix(ubuf_ref, gm_ref, sc_ref, cat_s, t * tm + n_prefix + 1, tm)
        else:
            cat_s[:, 0:tok_width] = tok_ref[s].astype(cat_ref.dtype)
        _memory_attend(lambda cols, s=s: qm_ref[s, :, cols], mk_ref.at[s], mv_ref.at[s], cat_s, mem_by_head)
    y = jnp.dot(cat_ref[...].astype(BF16), wo_ref[...], preferred_element_type=F32)
    out = x_ref[...].reshape(nb * tm, d) + _rms(y, g_ref[1:2, :])
    o_ref[...] = out.reshape(nb, tm, d)


def _pool_layer_kernel(x_ref, g_ref, win_ref, mk_ref, mv_ref, wo_ref, gm_ref, sc_ref, o_ref, tail_ref,
                       cat_ref, ubuf_ref):
    t = pl.program_id(1)
    tm = x_ref.shape[1]
    tok_width = cat_ref.shape[1] - MEM_WIDTH

    @pl.when(t == 0)
    def _():
        ubuf_ref[0:HALO, :] = jnp.zeros((HALO, tok_width), F32)

    h = _rms(x_ref[0], g_ref[0:1, :]).astype(BF16)
    proj = jnp.dot(h, win_ref[...], preferred_element_type=F32)
    ubuf_ref[HALO:HALO + tm, :] = proj[:, 0:tok_width]
    _pool_mix(ubuf_ref, gm_ref, sc_ref, cat_ref, t * tm + 1, tm)
    _memory_attend(lambda cols: proj[:, tok_width + cols.start:tok_width + cols.stop], mk_ref.at[0], mv_ref.at[0],
                   cat_ref, False)
    y = jnp.dot(cat_ref[...], wo_ref[...], preferred_element_type=F32)
    o_ref[0] = x_ref[0] + _rms(y, g_ref[1:2, :])
    tail = ubuf_ref[tm:tm + HALO, :]
    ubuf_ref[0:HALO, :] = tail
    tail_ref[0] = tail


def _pool_layer(x, g, w_in, w_out, layer, mk, mv, group_maps, pool_scale, *, tm):
    b, t, d = x.shape
    tok_width = d - MEM_WIDTH
    mem_len = mk.shape[1]
    _, ng, gw, _ = group_maps.shape
    x_spec = pl.BlockSpec((1, tm, d), lambda bb, i: (bb, i, 0))
    mem_spec = pl.BlockSpec((1, mem_len, MEM_WIDTH), lambda bb, i: (bb, 0, 0))
    weight_spec = pl.BlockSpec((None, d, d), lambda bb, i: (layer, 0, 0), pipeline_mode=pl.Buffered(1))
    vmem = (2 * d * d * 2 + 2 * (2 * tm * d * 4 + 4 * mem_len * MEM_WIDTH * 4 + ng * gw * gw * 2)
            + (HALO + tm) * tok_width * 4 + tm * d * 2 + 4 * tm * d * 4 + 6 * MIB)
    return pl.pallas_call(
        _pool_layer_kernel,
        grid=(b, t // tm),
        in_specs=[x_spec, pl.BlockSpec((2, d), lambda bb, i: (0, 0)), weight_spec, mem_spec, mem_spec, weight_spec,
                  pl.BlockSpec((None, ng, gw, gw), lambda bb, i: (layer, 0, 0, 0)),
                  pl.BlockSpec((1, tok_width), lambda bb, i: (0, 0))],
        out_specs=[x_spec, pl.BlockSpec((1, HALO, tok_width), lambda bb, i: (bb, 0, 0))],
        out_shape=[jax.ShapeDtypeStruct((b, t, d), F32), jax.ShapeDtypeStruct((b, HALO, tok_width), F32)],
        scratch_shapes=[pltpu.VMEM((tm, d), BF16), pltpu.VMEM((HALO + tm, tok_width), F32)],
        compiler_params=_params(("parallel", "arbitrary"), vmem),
        name="pool_layer",
    )(x, g, w_in, mk, mv, w_out, group_maps, pool_scale)


def _mixer_out(x, g, w_out, layer, qm_src, qm_block, mk, mv, *, tm, tok=None, pool_src=None, halo_src=None,
               halo_valid_from=0, n_prefix=0, group_maps=None, pool_scale=None, nb=1):
    b, t, d = x.shape
    tok_width = d - MEM_WIDTH
    mem_len = mk.shape[1]
    pool = tok is None
    mem_by_head = mk.ndim == 4
    rows = nb * tm
    qm_spec = pl.BlockSpec((nb, tm, MEM_WIDTH), lambda bb, i: (bb, i, qm_block))
    if mem_by_head:
        mem_spec = pl.BlockSpec((nb, mem_len, MEM_HEADS, HEAD_DIM), lambda bb, i: (bb, 0, 0, 0))
    else:
        mem_spec = pl.BlockSpec((nb, mem_len, MEM_WIDTH), lambda bb, i: (bb, 0, 0))
    x_spec = pl.BlockSpec((nb, tm, d), lambda bb, i: (bb, i, 0))
    g_spec = pl.BlockSpec((2, d), lambda bb, i: (0, 0))
    wo_spec = pl.BlockSpec((None, d, d), lambda bb, i: (layer, 0, 0))
    scratch = [pltpu.VMEM((rows, d), BF16 if nb == 1 or tm % 16 == 0 else F32)]
    if pool:
        halo_blocks = tm // HALO
        if halo_src is pool_src:
            halo_map = lambda bb, i: (bb, jnp.maximum(i * halo_blocks - 1, 0), 0)
        else:
            halo_map = lambda bb, i: (bb, 0, 0)
        _, ng, gw, _ = group_maps.shape
        in_specs = [pl.BlockSpec((nb, tm, tok_width), lambda bb, i: (bb, i, 0)),
                    pl.BlockSpec((nb, HALO, tok_width), halo_map),
                    qm_spec, mem_spec, mem_spec, x_spec, g_spec, wo_spec,
                    pl.BlockSpec((None, ng, gw, gw), lambda bb, i: (layer, 0, 0, 0)),
                    pl.BlockSpec((1, tok_width), lambda bb, i: (0, 0))]
        args = (pool_src, halo_src, qm_src, mk, mv, x, g, w_out, group_maps, pool_scale)
        scratch.append(pltpu.VMEM((HALO + tm, tok_width), F32))
    else:
        in_specs = [pl.BlockSpec((nb, tm, tok_width), lambda bb, i: (bb, i, 0)),
                    qm_spec, mem_spec, mem_spec, x_spec, g_spec, wo_spec]
        args = (tok, qm_src, mk, mv, x, g, w_out)
    vmem = (2 * (rows * tok_width * 4 + rows * MEM_WIDTH * 4 + 2 * rows * d * 4 + 4 * nb * mem_len * MEM_WIDTH * 4
                 + d * d * 2)
            + (HALO + tm) * tok_width * 4 + rows * d * 2 + 3 * rows * d * 4 + 6 * MIB)
    return pl.pallas_call(
        functools.partial(_mixer_kernel, pool=pool, n_prefix=n_prefix, halo_valid_from=halo_valid_from,
                          mem_by_head=mem_by_head),
        grid=(b // nb, t // tm),
        in_specs=in_specs,
        out_specs=x_spec,
        out_shape=jax.ShapeDtypeStruct((b, t, d), F32),
        scratch_shapes=scratch,
        compiler_params=_params(("parallel", "arbitrary"), vmem),
        name="mixer_pool" if pool else "mixer_sb",
    )(*args)


def kernel(x_prompt, x_sample, state_pool, cache_sb_k, cache_sb_v, cache_mem_k, cache_mem_v, page_table, mem_prompt, norm_ffn1, ffn1_w_gate_up, ffn1_w_down, norm_mix, norm_mem, mem_w_kv, pool_w_in, pool_group_maps, pool_scale, pool_w_out, sb_w_in, sb_logit_bias, sb_w_out, norm_ffn2, ffn2_w_gate_up, ffn2_w_down):
    b, t, d = x_prompt.shape
    bs, ts, _ = x_sample.shape
    depth = norm_ffn1.shape[0]
    mem_len = mem_prompt.shape[1]
    tok_width = d - MEM_WIDTH
    n_heads = tok_width // HEAD_DIM
    assert ts <= SAMPLE_ROWS

    tm_ffn, tf_ffn = 1024, 512
    tm_proj, tn_proj = 1024, 512
    tm_mix = 512
    ms = bs * SAMPLE_ROWS

    w_kv = mem_w_kv.astype(BF16)
    wp_in, wp_out, gmaps = pool_w_in.astype(BF16), pool_w_out.astype(BF16), pool_group_maps.astype(BF16)
    ws_in, ws_out = sb_w_in.astype(BF16), sb_w_out.astype(BF16)

    xp = x_prompt.reshape(b * t, d)
    xs = jnp.pad(x_sample, ((0, 0), (0, SAMPLE_ROWS - ts), (0, 0))).reshape(ms, d)
    mem2 = mem_prompt.reshape(b * mem_len, d)
    page_flat = page_table.reshape(-1).astype(jnp.int32)
    cache_k = jnp.transpose(cache_sb_k, (0, 1, 3, 2, 4))
    cache_v = jnp.transpose(cache_sb_v, (0, 1, 3, 2, 4))

    pool_p, pool_s, ks, vs, mkp, mvp = [], [], [], [], [], []
    qkv_p = None
    for i in range(depth):
        li = i // N_MIXERS
        xs, *w_bf = _ffn_half_casting(xs, norm_ffn1[i], ffn1_w_gate_up, ffn1_w_down, i, tf=tf_ffn)
        xp = _ffn_half(xp, norm_ffn1[i], *w_bf, tm=tm_ffn, tf=tf_ffn)

        g_mem = norm_mem[i][None]
        flat = lambda n: (n, F32, False)
        mk_p, mv_p = _norm_proj(mem2, g_mem, w_kv, i, 0, (flat(MEM_WIDTH), flat(MEM_WIDTH)), tm=tm_proj, tn=tn_proj)
        mk_p = mk_p.reshape(b, mem_len, MEM_WIDTH)
        mv_p = mv_p.reshape(b, mem_len, MEM_WIDTH)
        mkp.append(mk_p.reshape(b, mem_len, MEM_HEADS, HEAD_DIM))
        mvp.append(mv_p.reshape(b, mem_len, MEM_HEADS, HEAD_DIM))
        mk_s, mv_s = cache_mem_k[i], cache_mem_v[i]

        g_mix = norm_mix[i]
        g_in = g_mix[0:1]
        xp3 = xp.reshape(b, t, d)
        xs3 = xs.reshape(bs, SAMPLE_ROWS, d)
        if i % N_MIXERS == 0:
            scale = pool_scale[li][None]
            xp3, tail_p = _pool_layer(xp3, g_mix, wp_in, wp_out, li, mk_p, mv_p, gmaps, scale, tm=tm_mix)
            (proj_s,) = _norm_proj(xs, g_in, wp_in, li, 0, (flat(d),), tm=ms, tn=tn_proj)
            proj_s = proj_s.reshape(bs, SAMPLE_ROWS, d)
            qm_block = tok_width // MEM_WIDTH
            prefix = state_pool[li]
            n_prefix = prefix.shape[1]
            halo_s = jnp.pad(prefix, ((0, 0), (HALO - n_prefix, 0), (0, 0)))
            xs3 = _mixer_out(xs3, g_mix, wp_out, li, proj_s, qm_block, mk_s, mv_s, tm=SAMPLE_ROWS, nb=bs, pool_src=proj_s,
                             halo_src=halo_s, halo_valid_from=0, n_prefix=n_prefix, group_maps=gmaps,
                             pool_scale=scale)
            pool_p.append(tail_p[:, HALO - POOL_BUF:])
            u_ext = jnp.concatenate([prefix, proj_s[:, :ts, :tok_width]], axis=1)
            pool_s.append(u_ext[:, u_ext.shape[1] - POOL_BUF:])
        else:
            bias = sb_logit_bias[li].astype(F32)
            carried = {} if qkv_p is None else dict(enumerate(qkv_p))
            *qkv_p, qm_p = _norm_proj(
                xp, g_in, ws_in, li, 0,
                ((tok_width, BF16, True), (tok_width, F32, True), (tok_width, F32, True), flat(MEM_WIDTH)),
                tm=tm_proj, tn=tn_proj, seq_len=t, slots=(sb_w_in.shape[0], li, carried))
            o_p = _sb_prompt(*qkv_p, bias, li)
            xp3 = _mixer_out(xp3, g_mix, ws_out, li, qm_p.reshape(b, t, MEM_WIDTH), 0, mk_p, mv_p, tm=tm_mix,
                             tok=o_p)

            q_s, k_s, v_s, qm_s = _norm_proj(
                xs, g_in, ws_in, li, 0, (flat(tok_width), flat(tok_width), flat(tok_width), flat(MEM_WIDTH)),
                tm=ms, tn=tn_proj)
            k_s3 = k_s.reshape(bs, SAMPLE_ROWS, tok_width)
            v_s3 = v_s.reshape(bs, SAMPLE_ROWS, tok_width)
            q_rows = q_s.reshape(bs, SAMPLE_ROWS, n_heads, HEAD_DIM).transpose(0, 2, 1, 3)
            q_rows = q_rows.reshape(bs, n_heads * SAMPLE_ROWS, HEAD_DIM)
            bias_rows = jnp.repeat(bias, SAMPLE_ROWS)[:, None]
            o_rows = _sb_decode(q_rows, k_s3, v_s3, bias_rows, cache_k, cache_v, page_flat, li,
                                pages_per_step=8)
            o_s = o_rows.reshape(bs, n_heads, SAMPLE_ROWS, HEAD_DIM).transpose(0, 2, 1, 3)
            o_s = o_s.reshape(bs, SAMPLE_ROWS, tok_width)
            xs3 = _mixer_out(xs3, g_mix, ws_out, li, qm_s.reshape(bs, SAMPLE_ROWS, MEM_WIDTH), 0, mk_s, mv_s,
                             tm=SAMPLE_ROWS, nb=bs, tok=o_s)
            ks.append(k_s3[:, :ts].reshape(bs, ts, n_heads, HEAD_DIM))
            vs.append(v_s3[:, :ts].reshape(bs, ts, n_heads, HEAD_DIM))
        xp = xp3.reshape(b * t, d)
        xs = xs3.reshape(ms, d)

        xs, *w_bf = _ffn_half_casting(xs, norm_ffn2[i], ffn2_w_gate_up, ffn2_w_down, i, tf=tf_ffn)
        xp = _ffn_half(xp, norm_ffn2[i], *w_bf, tm=tm_ffn, tf=tf_ffn)

    y_p = xp.reshape(b, t, d)
    y_s = xs.reshape(bs, SAMPLE_ROWS, d)[:, :ts]
    k_p, v_p = (jnp.transpose(a, (0, 1, 3, 2, 4)) for a in qkv_p[1:])
    return (y_p, y_s, jnp.stack(pool_p), k_p, v_p, jnp.stack(mkp), jnp.stack(mvp),
            jnp.stack(pool_s), jnp.stack(ks), jnp.stack(vs))
```

```python
import functools

import jax
import jax.numpy as jnp
from jax import lax
from jax.experimental import pallas as pl
from jax.experimental.pallas import tpu as pltpu

F32 = jnp.float32
BF16 = jnp.bfloat16

HEAD_DIM = 128
MEM_HEADS = 4
MEM_WIDTH = MEM_HEADS * HEAD_DIM
POOL_WINDOWS = (2, 4, 8, 16)
POOL_BUF = max(POOL_WINDOWS) - 1
HALO = 16
N_MIXERS = 2
SB_BLOCK = 128
SB_QTILE = 512
SB_KCHUNK = 256
RMS_EPS = 1e-6
SAMPLE_ROWS = 8
LOG2E = 1.4426950408889634
MIB = 1024 * 1024
VMEM_CAP = 62 * MIB

_NT = (((1,), (1,)), ((), ()))


def _params(semantics, vmem_bytes):
    return pltpu.CompilerParams(dimension_semantics=semantics,
                                vmem_limit_bytes=int(min(vmem_bytes, VMEM_CAP)))


def _rms(x, g):
    ms = jnp.mean(x * x, axis=-1, keepdims=True)
    return x * lax.rsqrt(ms + RMS_EPS) * g


def _rms_rows(src_ref, gain, dst_ref, residual_ref=None, zero_ref=None):
    chunk = 16
    group = min(8, src_ref.shape[0] // chunk)
    assert src_ref.shape[0] % (group * chunk) == 0

    def body(c, carry):
        for s in range(group):
            rows = pl.ds(pl.multiple_of((c * group + s) * chunk, chunk), chunk)
            y = _rms(src_ref[rows, :], gain)
            if residual_ref is not None:
                y = residual_ref[rows, :] + y
            dst_ref[rows, :] = y.astype(dst_ref.dtype)
            if zero_ref is not None:
                zero_ref[rows, :] = jnp.zeros((chunk, zero_ref.shape[1]), zero_ref.dtype)
        return carry

    lax.fori_loop(0, src_ref.shape[0] // (group * chunk), body, 0)


def _suffix_matrix(n):
    r = lax.broadcasted_iota(jnp.int32, (n, n), 0)
    c = lax.broadcasted_iota(jnp.int32, (n, n), 1)
    return jnp.where(r > c, 1.0, 0.0).astype(BF16)


def _sb_weights(z2, vis, umat, carry):
    neg_abs = lax.bitcast_convert_type(lax.bitcast_convert_type(z2, jnp.uint32) | jnp.uint32(0x80000000), F32)
    sp = jnp.maximum(z2, 0.0) + jnp.log2(1.0 + jnp.exp2(neg_abs))
    if vis is not None:
        sp = jnp.where(vis, sp, 0.0)
    later = jnp.dot(sp.astype(BF16), umat, preferred_element_type=F32)
    w = jnp.exp2(z2 - sp - later - carry)
    if vis is not None:
        w = jnp.where(vis, w, 0.0)
    return w, carry + jnp.sum(sp, axis=-1, keepdims=True)


def _ffn_kernel(x_ref, g_ref, wg_ref, wu_ref, wd_ref, o_ref, *refs, acc_cols, emit_weights):
    h_ref, acc_ref = refs[-2:]
    j = pl.program_id(1)

    @pl.when(j == 0)
    def _():
        _rms_rows(x_ref, g_ref[0:1, :], h_ref, zero_ref=acc_ref)

    if emit_weights:
        for src, dst in zip((wg_ref, wu_ref, wd_ref), refs[:3]):
            dst[...] = src[...].astype(BF16)
        wg_ref, wu_ref, wd_ref = refs[:3]
    h = h_ref[...]
    gate = jnp.dot(h, wg_ref[...], preferred_element_type=F32)
    up = jnp.dot(h, wu_ref[...], preferred_element_type=F32)
    a = (gate * jax.nn.sigmoid(gate) * up).astype(BF16)
    for c0 in range(0, acc_ref.shape[1], acc_cols):
        cols = slice(c0, c0 + acc_cols)
        acc_ref[:, cols] += jnp.dot(a, wd_ref[:, cols], preferred_element_type=F32)

    @pl.when(j == pl.num_programs(1) - 1)
    def _():
        _rms_rows(acc_ref, 0.5 * g_ref[1:2, :], o_ref, residual_ref=x_ref)


def _ffn_half(x, g, w_gate, w_up, w_down, *, tm):
    m, d = x.shape
    nf, _, tf = w_gate.shape
    vmem = 2 * (2 * tm * d * 4) + tm * d * (2 + 4) + 2 * 3 * d * tf * 2 + 2 * tm * tf * 4 + 2 * MIB
    return pl.pallas_call(
        functools.partial(_ffn_kernel, acc_cols=min(d, 512), emit_weights=False),
        grid=(m // tm, nf),
        in_specs=[pl.BlockSpec((tm, d), lambda i, j: (i, 0)),
                  pl.BlockSpec((2, d), lambda i, j: (0, 0)),
                  pl.BlockSpec((None, d, tf), lambda i, j: (j, 0, 0)),
                  pl.BlockSpec((None, d, tf), lambda i, j: (j, 0, 0)),
                  pl.BlockSpec((tf, d), lambda i, j: (j, 0))],
        out_specs=pl.BlockSpec((tm, d), lambda i, j: (i, 0)),
        out_shape=jax.ShapeDtypeStruct((m, d), F32),
        scratch_shapes=[pltpu.VMEM((tm, d), BF16), pltpu.VMEM((tm, d), F32)],
        compiler_params=_params(("parallel", "arbitrary"), vmem),
        name="ffn_half",
    )(x, g, w_gate, w_up, w_down)


def _ffn_half_casting(x, g, w_gu, w_d, layer, *, tf):
    m, d = x.shape
    f = w_d.shape[1]
    nf = f // tf
    vmem = 4 * m * d * 4 + 2 * 3 * d * tf * (4 + 2) + 3 * d * tf * 2 + 6 * m * tf * 4 + 4 * MIB
    return pl.pallas_call(
        functools.partial(_ffn_kernel, acc_cols=min(d, 512), emit_weights=True),
        grid=(1, nf),
        in_specs=[pl.BlockSpec((m, d), lambda i, j: (0, 0)),
                  pl.BlockSpec((2, d), lambda i, j: (0, 0)),
                  pl.BlockSpec((None, d, tf), lambda i, j: (layer, 0, j)),
                  pl.BlockSpec((None, d, tf), lambda i, j: (layer, 0, j + nf)),
                  pl.BlockSpec((None, tf, d), lambda i, j: (layer, j, 0))],
        out_specs=[pl.BlockSpec((m, d), lambda i, j: (0, 0)),
                   pl.BlockSpec((None, d, tf), lambda i, j: (j, 0, 0)),
                   pl.BlockSpec((None, d, tf), lambda i, j: (j, 0, 0)),
                   pl.BlockSpec((tf, d), lambda i, j: (j, 0))],
        out_shape=[jax.ShapeDtypeStruct((m, d), F32), jax.ShapeDtypeStruct((nf, d, tf), BF16),
                   jax.ShapeDtypeStruct((nf, d, tf), BF16), jax.ShapeDtypeStruct((f, d), BF16)],
        scratch_shapes=[pltpu.VMEM((m, d), BF16), pltpu.VMEM((m, d), F32)],
        compiler_params=_params(("arbitrary", "arbitrary"), vmem),
        name="ffn_half_casting",
    )(x, g, w_gu, w_gu, w_d)


def _norm_proj_kernel(x_ref, g_ref, w_ref, *refs, bounds, by_head, n_carried):
    o_refs, h_ref = refs[n_carried:-1], refs[-1]
    j = pl.program_id(1)

    @pl.when(j == 0)
    def _():
        h_ref[...] = _rms(x_ref[...], g_ref[...]).astype(BF16)

    for o_ref, (lo, hi), heads in zip(o_refs, bounds, by_head):
        @pl.when(jnp.logical_and(j >= lo, j < hi))
        def _(o_ref=o_ref, heads=heads):
            out = jnp.dot(h_ref[...], w_ref[...], preferred_element_type=F32).astype(o_ref.dtype)
            if heads:
                for hh in range(o_ref.shape[1]):
                    o_ref[0, hh] = out[:, hh * HEAD_DIM:(hh + 1) * HEAD_DIM]
            else:
                o_ref[...] = out


def _norm_proj(x, g, w, layer, col0, segments, *, tm, tn, seq_len=None, slots=None):
    m, d = x.shape
    assert col0 % tn == 0 and m % tm == 0 and all(n % tn == 0 for n, _, _ in segments)
    cb = col0 // tn
    n_slots, slot, carried = slots if slots is not None else (1, 0, {})
    bounds, lo = [], 0
    for n, _, _ in segments:
        bounds.append((lo, lo + n // tn))
        lo += n // tn
    out_specs, out_shapes, aliases, carried_args = [], [], {}, []
    for si, ((n, dtype, heads), (a, b)) in enumerate(zip(segments, bounds)):
        local = lambda j, a=a, b=b: jnp.clip(j - a, 0, b - a - 1)
        if heads:
            assert seq_len % tm == 0 and tn % HEAD_DIM == 0
            tiles = seq_len // tm
            out_specs.append(pl.BlockSpec((None, 1, tn // HEAD_DIM, tm, HEAD_DIM),
                                          lambda i, j, local=local: (slot, i // tiles, local(j), i % tiles, 0)))
            out_shapes.append(jax.ShapeDtypeStruct((n_slots, m // seq_len, n // HEAD_DIM, seq_len, HEAD_DIM), dtype))
            if si in carried:
                aliases[3 + len(carried_args)] = si
                carried_args.append(carried[si])
        else:
            out_specs.append(pl.BlockSpec((tm, tn), lambda i, j, local=local: (i, local(j))))
            out_shapes.append(jax.ShapeDtypeStruct((m, n), dtype))
    vmem = 2 * tm * d * 4 + tm * d * 2 + 2 * d * tn * 2 + (2 * len(segments) + 2) * tm * tn * 4 + 4 * MIB
    return pl.pallas_call(
        functools.partial(_norm_proj_kernel, bounds=tuple(bounds), by_head=tuple(s[2] for s in segments),
                          n_carried=len(carried_args)),
        grid=(m // tm, lo),
        in_specs=[pl.BlockSpec((tm, d), lambda i, j: (i, 0)),
                  pl.BlockSpec((1, d), lambda i, j: (0, 0)),
                  pl.BlockSpec((None, d, tn), lambda i, j: (layer, 0, j + cb))]
                 + [pl.BlockSpec(memory_space=pl.ANY)] * len(carried_args),
        out_specs=out_specs,
        out_shape=out_shapes,
        input_output_aliases=aliases,
        scratch_shapes=[pltpu.VMEM((tm, d), BF16)],
        compiler_params=_params(("parallel", "arbitrary"), vmem),
        name="norm_proj",
    )(x, g, w, *carried_args)


def _sb_prompt_kernel(bias_ref, q_ref, k_ref, v_ref, o_ref, kb_ref, vb_ref, u_ref):
    tq, kc = SB_QTILE, SB_KCHUNK
    kb_ref[...] = k_ref[0, 0].astype(BF16)
    vb_ref[...] = v_ref[0, 0].astype(BF16)
    u_ref[...] = _suffix_matrix(kc)
    bias = bias_ref[pl.program_id(1)] * LOG2E
    scale = HEAD_DIM ** -0.5 * LOG2E
    r = lax.broadcasted_iota(jnp.int32, (tq, kc), 0)
    c = lax.broadcasted_iota(jnp.int32, (tq, kc), 1)
    for i in range(q_ref.shape[2] // tq):
        q = q_ref[0, 0, i * tq:(i + 1) * tq, :]
        acc = jnp.zeros((tq, HEAD_DIM), F32)
        carry = jnp.zeros((tq, 1), F32)
        for j in range((i + 1) * tq // kc - 1, -1, -1):
            rows = slice(j * kc, (j + 1) * kc)
            vis = (c + (j * kc - i * tq) < r) if (j + 1) * kc > i * tq else None
            z = lax.dot_general(q, kb_ref[rows, :], _NT, preferred_element_type=F32) * scale + bias
            w, carry = _sb_weights(z, vis, u_ref[...], carry)
            acc = acc + jnp.dot(w.astype(BF16), vb_ref[rows, :], preferred_element_type=F32)
        o_ref[0, i * tq:(i + 1) * tq, :] = acc.astype(o_ref.dtype)


def _sb_prompt(q, k, v, bias, slot):
    _, b, nh, t, _ = q.shape
    assert t % SB_QTILE == 0 and SB_QTILE % SB_KCHUNK == 0
    head_spec = pl.BlockSpec((None, 1, 1, t, HEAD_DIM), lambda bb, h: (slot, bb, h, 0, 0))
    return pl.pallas_call(
        _sb_prompt_kernel,
        grid=(b, nh),
        in_specs=[pl.BlockSpec(memory_space=pltpu.SMEM), head_spec, head_spec, head_spec],
        out_specs=pl.BlockSpec((1, t, HEAD_DIM), lambda bb, h: (bb, 0, h)),
        out_shape=jax.ShapeDtypeStruct((b, t, nh * HEAD_DIM), BF16),
        scratch_shapes=[pltpu.VMEM((t, HEAD_DIM), BF16), pltpu.VMEM((t, HEAD_DIM), BF16),
                        pltpu.VMEM((SB_KCHUNK, SB_KCHUNK), BF16)],
        compiler_params=_params(("parallel", "parallel"), 32 * MIB),
        name="sb_prompt",
    )(bias, q, k, v)


def _sb_decode_kernel(pt_ref, q_ref, knew_ref, vnew_ref, bias_ref, *refs, pages_per_step, n_heads):
    del pt_ref
    k_refs = refs[:pages_per_step]
    v_refs = refs[pages_per_step:2 * pages_per_step]
    o_ref, acc_ref, car_ref = refs[2 * pages_per_step:]
    p = pl.program_id(1)
    rows = n_heads * SAMPLE_ROWS
    scale = HEAD_DIM ** -0.5 * LOG2E
    chunk = min(SB_KCHUNK, pages_per_step * SB_BLOCK)
    umat = _suffix_matrix(chunk)
    bias = bias_ref[...] * LOG2E
    q = q_ref[0]
    qh = [q[h * SAMPLE_ROWS:(h + 1) * SAMPLE_ROWS].astype(BF16) for h in range(n_heads)]

    def sweep(keys, values, n_keys, vis):
        zs = [lax.dot_general(qh[h], keys(h), _NT, preferred_element_type=F32) for h in range(n_heads)]
        z = jnp.concatenate(zs, axis=0) * scale + bias
        step = min(chunk, n_keys)
        carry = car_ref[...]
        ws = []
        for c0 in range(n_keys - step, -1, -step):
            w, carry = _sb_weights(z[:, c0:c0 + step], vis, umat[:step, :step], carry)
            ws.insert(0, w)
        w = jnp.concatenate(ws, axis=1)
        outs = [jnp.dot(w[h * SAMPLE_ROWS:(h + 1) * SAMPLE_ROWS].astype(BF16), values(h),
                        preferred_element_type=F32) for h in range(n_heads)]
        acc_ref[...] += jnp.concatenate(outs, axis=0)
        car_ref[...] = carry

    @pl.when(p == 0)
    def _():
        acc_ref[...] = jnp.zeros_like(acc_ref)
        car_ref[...] = jnp.zeros_like(car_ref)
        t = lax.broadcasted_iota(jnp.int32, (rows, SB_BLOCK), 0) % SAMPLE_ROWS
        s = lax.broadcasted_iota(jnp.int32, (rows, SB_BLOCK), 1)
        zeros = jnp.zeros((SB_BLOCK - SAMPLE_ROWS, HEAD_DIM), F32)

        def new_block(ref, h):
            head = ref[0, :, h * HEAD_DIM:(h + 1) * HEAD_DIM]
            return jnp.concatenate([head, zeros], axis=0).astype(BF16)

        sweep(functools.partial(new_block, knew_ref), functools.partial(new_block, vnew_ref), SB_BLOCK, s < t)

    def pages(page_refs, h):
        return jnp.concatenate([r[h].astype(BF16) for r in reversed(page_refs)], axis=0)

    sweep(functools.partial(pages, k_refs), functools.partial(pages, v_refs), pages_per_step * SB_BLOCK, None)

    @pl.when(p == pl.num_programs(1) - 1)
    def _():
        o_ref[0] = acc_ref[...]


def _sb_decode(q_rows, k_new, v_new, bias_rows, cache_k, cache_v, page_flat, layer, *, pages_per_step):
    bsz, rows, _ = q_rows.shape
    n_heads = rows // SAMPLE_ROWS
    width = n_heads * HEAD_DIM
    page = cache_k.shape[3]
    n_pages = page_flat.shape[0] // bsz
    assert page == SB_BLOCK and n_pages % pages_per_step == 0
    steps = n_pages // pages_per_step

    def page_map(c):
        return lambda b, p, pt: (layer, pt[b * n_pages + n_pages - 1 - (p * pages_per_step + c)], 0, 0, 0)

    def page_specs():
        return [pl.BlockSpec((None, None, n_heads, page, HEAD_DIM), page_map(c)) for c in range(pages_per_step)]

    seq_map = lambda b, p, pt: (b, 0, 0)
    grid_spec = pltpu.PrefetchScalarGridSpec(
        num_scalar_prefetch=1,
        grid=(bsz, steps),
        in_specs=[pl.BlockSpec((1, rows, HEAD_DIM), seq_map),
                  pl.BlockSpec((1, SAMPLE_ROWS, width), seq_map),
                  pl.BlockSpec((1, SAMPLE_ROWS, width), seq_map),
                  pl.BlockSpec((rows, 1), lambda b, p, pt: (0, 0))] + page_specs() + page_specs(),
        out_specs=pl.BlockSpec((1, rows, HEAD_DIM), seq_map),
        scratch_shapes=[pltpu.VMEM((rows, HEAD_DIM), F32), pltpu.VMEM((rows, 1), F32)],
    )
    vmem = 2 * 2 * pages_per_step * page * width * 4 + 12 * MIB
    return pl.pallas_call(
        functools.partial(_sb_decode_kernel, pages_per_step=pages_per_step, n_heads=n_heads),
        grid_spec=grid_spec,
        out_shape=jax.ShapeDtypeStruct((bsz, rows, HEAD_DIM), F32),
        compiler_params=_params(("parallel", "arbitrary"), vmem),
        name="sb_decode",
    )(page_flat, q_rows, k_new, v_new, bias_rows, *([cache_k] * pages_per_step), *([cache_v] * pages_per_step))


def _pool_mix(ubuf_ref, gm_ref, sc_ref, cat_ref, first_pos, tm):
    tok_width = ubuf_ref.shape[1]
    group = tok_width // len(POOL_WINDOWS)
    pos = first_pos + lax.broadcasted_iota(jnp.int32, (tm, 1), 0)
    for gi, win in enumerate(POOL_WINDOWS):
        cols = slice(gi * group, (gi + 1) * group)
        own = ubuf_ref[HALO:HALO + tm, cols]
        tot = own
        for back in range(1, win):
            tot = tot + ubuf_ref[HALO - back:HALO - back + tm, cols]
        count = jnp.minimum(pos, win).astype(F32)
        diff = tot / count - own
        mixed = jnp.dot(diff.astype(BF16), gm_ref[gi], preferred_element_type=F32) * sc_ref[:, cols]
        cat_ref[:, cols] = mixed.astype(cat_ref.dtype)


def _memory_attend(queries, mk_ref, mv_ref, cat_ref, mem_by_head):
    tok_width = cat_ref.shape[1] - MEM_WIDTH
    scale = HEAD_DIM ** -0.5
    if mem_by_head:
        mk_heads = pltpu.einshape("mhd->hmd", mk_ref[...].astype(BF16))
        mv_heads = pltpu.einshape("mhd->hmd", mv_ref[...].astype(BF16))
    for h in range(MEM_HEADS):
        cols = slice(h * HEAD_DIM, (h + 1) * HEAD_DIM)
        qh = queries(cols).astype(BF16)
        if mem_by_head:
            kh, vh = mk_heads[h], mv_heads[h]
        else:
            kh, vh = mk_ref[:, cols].astype(BF16), mv_ref[:, cols].astype(BF16)
        s = lax.dot_general(qh, kh, _NT, preferred_element_type=F32) * scale
        e = jnp.exp(s - jnp.max(s, axis=-1, keepdims=True))
        prob = e / jnp.sum(e, axis=-1, keepdims=True)
        oh = jnp.dot(prob.astype(BF16), vh, preferred_element_type=F32)
        cat_ref[:, tok_width + h * HEAD_DIM:tok_width + (h + 1) * HEAD_DIM] = oh.astype(cat_ref.dtype)


def _mixer_kernel(*refs, pool, n_prefix, halo_valid_from, mem_by_head):
    if pool:
        (u_ref, halo_ref, qm_ref, mk_ref, mv_ref, x_ref, g_ref, wo_ref, gm_ref, sc_ref,
         o_ref, cat_ref, ubuf_ref) = refs
    else:
        tok_ref, qm_ref, mk_ref, mv_ref, x_ref, g_ref, wo_ref, o_ref, cat_ref = refs
    t = pl.program_id(1)
    nb, tm, d = x_ref.shape
    tok_width = cat_ref.shape[1] - MEM_WIDTH

    for s in range(nb):
        cat_s = cat_ref.at[s * tm:(s + 1) * tm]
        if pool:
            ubuf_ref[0:HALO, :] = jnp.where(t >= halo_valid_from, halo_ref[s], 0.0)
            ubuf_ref[HALO:HALO + tm, :] = u_ref[s]
            _pool_mix(ubuf_ref, gm_ref, sc_ref, cat_s, t * tm + n_prefix + 1, tm)
        else:
            cat_s[:, 0:tok_width] = tok_ref[s].astype(cat_ref.dtype)
        _memory_attend(lambda cols, s=s: qm_ref[s, :, cols], mk_ref.at[s], mv_ref.at[s], cat_s, mem_by_head)
    y = jnp.dot(cat_ref[...].astype(BF16), wo_ref[...], preferred_element_type=F32)
    out = x_ref[...].reshape(nb * tm, d) + _rms(y, g_ref[1:2, :])
    o_ref[...] = out.reshape(nb, tm, d)


def _pool_layer_kernel(x_ref, g_ref, win_ref, mk_ref, mv_ref, wo_ref, gm_ref, sc_ref, o_ref, tail_ref,
                       cat_ref, ubuf_ref):
    t = pl.program_id(1)
    tm = x_ref.shape[1]
    tok_width = cat_ref.shape[1] - MEM_WIDTH

    @pl.when(t == 0)
    def _():
        ubuf_ref[0:HALO, :] = jnp.zeros((HALO, tok_width), F32)

    h = _rms(x_ref[0], g_ref[0:1, :]).astype(BF16)
    proj = jnp.dot(h, win_ref[...], preferred_element_type=F32)
    ubuf_ref[HALO:HALO + tm, :] = proj[:, 0:tok_width]
    _pool_mix(ubuf_ref, gm_ref, sc_ref, cat_ref, t * tm + 1, tm)
    _memory_attend(lambda cols: proj[:, tok_width + cols.start:tok_width + cols.stop], mk_ref.at[0], mv_ref.at[0],
                   cat_ref, False)
    y = jnp.dot(cat_ref[...], wo_ref[...], preferred_element_type=F32)
    o_ref[0] = x_ref[0] + _rms(y, g_ref[1:2, :])
    tail = ubuf_ref[tm:tm + HALO, :]
    ubuf_ref[0:HALO, :] = tail
    tail_ref[0] = tail


def _pool_layer(x, g, w_in, w_out, layer, mk, mv, group_maps, pool_scale, *, tm):
    b, t, d = x.shape
    tok_width = d - MEM_WIDTH
    mem_len = mk.shape[1]
    _, ng, gw, _ = group_maps.shape
    x_spec = pl.BlockSpec((1, tm, d), lambda bb, i: (bb, i, 0))
    mem_spec = pl.BlockSpec((1, mem_len, MEM_WIDTH), lambda bb, i: (bb, 0, 0))
    weight_spec = pl.BlockSpec((None, d, d), lambda bb, i: (layer, 0, 0), pipeline_mode=pl.Buffered(1))
    vmem = (2 * d * d * 2 + 2 * (2 * tm * d * 4 + 4 * mem_len * MEM_WIDTH * 4 + ng * gw * gw * 2)
            + (HALO + tm) * tok_width * 4 + tm * d * 2 + 4 * tm * d * 4 + 6 * MIB)
    return pl.pallas_call(
        _pool_layer_kernel,
        grid=(b, t // tm),
        in_specs=[x_spec, pl.BlockSpec((2, d), lambda bb, i: (0, 0)), weight_spec, mem_spec, mem_spec, weight_spec,
                  pl.BlockSpec((None, ng, gw, gw), lambda bb, i: (layer, 0, 0, 0)),
                  pl.BlockSpec((1, tok_width), lambda bb, i: (0, 0))],
        out_specs=[x_spec, pl.BlockSpec((1, HALO, tok_width), lambda bb, i: (bb, 0, 0))],
        out_shape=[jax.ShapeDtypeStruct((b, t, d), F32), jax.ShapeDtypeStruct((b, HALO, tok_width), F32)],
        scratch_shapes=[pltpu.VMEM((tm, d), BF16), pltpu.VMEM((HALO + tm, tok_width), F32)],
        compiler_params=_params(("parallel", "arbitrary"), vmem),
        name="pool_layer",
    )(x, g, w_in, mk, mv, w_out, group_maps, pool_scale)


def _mixer_out(x, g, w_out, layer, qm_src, qm_block, mk, mv, *, tm, tok=None, pool_src=None, halo_src=None,
               halo_valid_from=0, n_prefix=0, group_maps=None, pool_scale=None, nb=1):
    b, t, d = x.shape
    tok_width = d - MEM_WIDTH
    mem_len = mk.shape[1]
    pool = tok is None
    mem_by_head = mk.ndim == 4
    rows = nb * tm
    qm_spec = pl.BlockSpec((nb, tm, MEM_WIDTH), lambda bb, i: (bb, i, qm_block))
    if mem_by_head:
        mem_spec = pl.BlockSpec((nb, mem_len, MEM_HEADS, HEAD_DIM), lambda bb, i: (bb, 0, 0, 0))
    else:
        mem_spec = pl.BlockSpec((nb, mem_len, MEM_WIDTH), lambda bb, i: (bb, 0, 0))
    x_spec = pl.BlockSpec((nb, tm, d), lambda bb, i: (bb, i, 0))
    g_spec = pl.BlockSpec((2, d), lambda bb, i: (0, 0))
    wo_spec = pl.BlockSpec((None, d, d), lambda bb, i: (layer, 0, 0))
    scratch = [pltpu.VMEM((rows, d), BF16 if nb == 1 or tm % 16 == 0 else F32)]
    if pool:
        halo_blocks = tm // HALO
        if halo_src is pool_src:
            halo_map = lambda bb, i: (bb, jnp.maximum(i * halo_blocks - 1, 0), 0)
        else:
            halo_map = lambda bb, i: (bb, 0, 0)
        _, ng, gw, _ = group_maps.shape
        in_specs = [pl.BlockSpec((nb, tm, tok_width), lambda bb, i: (bb, i, 0)),
                    pl.BlockSpec((nb, HALO, tok_width), halo_map),
                    qm_spec, mem_spec, mem_spec, x_spec, g_spec, wo_spec,
                    pl.BlockSpec((None, ng, gw, gw), lambda bb, i: (layer, 0, 0, 0)),
                    pl.BlockSpec((1, tok_width), lambda bb, i: (0, 0))]
        args = (pool_src, halo_src, qm_src, mk, mv, x, g, w_out, group_maps, pool_scale)
        scratch.append(pltpu.VMEM((HALO + tm, tok_width), F32))
    else:
        in_specs = [pl.BlockSpec((nb, tm, tok_width), lambda bb, i: (bb, i, 0)),
                    qm_spec, mem_spec, mem_spec, x_spec, g_spec, wo_spec]
        args = (tok, qm_src, mk, mv, x, g, w_out)
    vmem = (2 * (rows * tok_width * 4 + rows * MEM_WIDTH * 4 + 2 * rows * d * 4 + 4 * nb * mem_len * MEM_WIDTH * 4
                 + d * d * 2)
            + (HALO + tm) * tok_width * 4 + rows * d * 2 + 3 * rows * d * 4 + 6 * MIB)
    return pl.pallas_call(
        functools.partial(_mixer_kernel, pool=pool, n_prefix=n_prefix, halo_valid_from=halo_valid_from,
                          mem_by_head=mem_by_head),
        grid=(b // nb, t // tm),
        in_specs=in_specs,
        out_specs=x_spec,
        out_shape=jax.ShapeDtypeStruct((b, t, d), F32),
        scratch_shapes=scratch,
        compiler_params=_params(("parallel", "arbitrary"), vmem),
        name="mixer_pool" if pool else "mixer_sb",
    )(*args)


def kernel(x_prompt, x_sample, state_pool, cache_sb_k, cache_sb_v, cache_mem_k, cache_mem_v, page_table, mem_prompt, norm_ffn1, ffn1_w_gate_up, ffn1_w_down, norm_mix, norm_mem, mem_w_kv, pool_w_in, pool_group_maps, pool_scale, pool_w_out, sb_w_in, sb_logit_bias, sb_w_out, norm_ffn2, ffn2_w_gate_up, ffn2_w_down):
    b, t, d = x_prompt.shape
    bs, ts, _ = x_sample.shape
    depth = norm_ffn1.shape[0]
    mem_len = mem_prompt.shape[1]
    tok_width = d - MEM_WIDTH
    n_heads = tok_width // HEAD_DIM
    assert ts <= SAMPLE_ROWS

    tm_ffn, tf_ffn = 1024, 512
    tm_proj, tn_proj = 1024, 512
    tm_mix = 512
    ms = bs * SAMPLE_ROWS

    w_kv = mem_w_kv.astype(BF16)
    wp_in, wp_out, gmaps = pool_w_in.astype(BF16), pool_w_out.astype(BF16), pool_group_maps.astype(BF16)
    ws_in, ws_out = sb_w_in.astype(BF16), sb_w_out.astype(BF16)

    xp = x_prompt.reshape(b * t, d)
    xs = jnp.pad(x_sample, ((0, 0), (0, SAMPLE_ROWS - ts), (0, 0))).reshape(ms, d)
    mem2 = mem_prompt.reshape(b * mem_len, d)
    page_flat = page_table.reshape(-1).astype(jnp.int32)
    cache_k = jnp.transpose(cache_sb_k, (0, 1, 3, 2, 4))
    cache_v = jnp.transpose(cache_sb_v, (0, 1, 3, 2, 4))

    pool_p, pool_s, ks, vs, mkp, mvp = [], [], [], [], [], []
    qkv_p = None
    for i in range(depth):
        li = i // N_MIXERS
        xs, *w_bf = _ffn_half_casting(xs, norm_ffn1[i], ffn1_w_gate_up, ffn1_w_down, i, tf=tf_ffn)
        xp = _ffn_half(xp, norm_ffn1[i], *w_bf, tm=tm_ffn)

        g_mem = norm_mem[i][None]
        flat = lambda n: (n, F32, False)
        mk_p, mv_p = _norm_proj(mem2, g_mem, w_kv, i, 0, (flat(MEM_WIDTH), flat(MEM_WIDTH)), tm=tm_proj, tn=tn_proj)
        mk_p = mk_p.reshape(b, mem_len, MEM_WIDTH)
        mv_p = mv_p.reshape(b, mem_len, MEM_WIDTH)
        mkp.append(mk_p.reshape(b, mem_len, MEM_HEADS, HEAD_DIM))
        mvp.append(mv_p.reshape(b, mem_len, MEM_HEADS, HEAD_DIM))
        mk_s, mv_s = cache_mem_k[i], cache_mem_v[i]

        g_mix = norm_mix[i]
        g_in = g_mix[0:1]
        xp3 = xp.reshape(b, t, d)
        xs3 = xs.reshape(bs, SAMPLE_ROWS, d)
        if i % N_MIXERS == 0:
            scale = pool_scale[li][None]
            xp3, tail_p = _pool_layer(xp3, g_mix, wp_in, wp_out, li, mk_p, mv_p, gmaps, scale, tm=tm_mix)
            (proj_s,) = _norm_proj(xs, g_in, wp_in, li, 0, (flat(d),), tm=ms, tn=tn_proj)
            proj_s = proj_s.reshape(bs, SAMPLE_ROWS, d)
            qm_block = tok_width // MEM_WIDTH
            prefix = state_pool[li]
            n_prefix = prefix.shape[1]
            halo_s = jnp.pad(prefix, ((0, 0), (HALO - n_prefix, 0), (0, 0)))
            xs3 = _mixer_out(xs3, g_mix, wp_out, li, proj_s, qm_block, mk_s, mv_s, tm=SAMPLE_ROWS, nb=bs, pool_src=proj_s,
                             halo_src=halo_s, halo_valid_from=0, n_prefix=n_prefix, group_maps=gmaps,
                             pool_scale=scale)
            pool_p.append(tail_p[:, HALO - POOL_BUF:])
            u_ext = jnp.concatenate([prefix, proj_s[:, :ts, :tok_width]], axis=1)
            pool_s.append(u_ext[:, u_ext.shape[1] - POOL_BUF:])
        else:
            bias = sb_logit_bias[li].astype(F32)
            carried = {} if qkv_p is None else dict(enumerate(qkv_p))
            *qkv_p, qm_p = _norm_proj(
                xp, g_in, ws_in, li, 0,
                ((tok_width, BF16, True), (tok_width, F32, True), (tok_width, F32, True), flat(MEM_WIDTH)),
                tm=tm_proj, tn=tn_proj, seq_len=t, slots=(sb_w_in.shape[0], li, carried))
            o_p = _sb_prompt(*qkv_p, bias, li)
            xp3 = _mixer_out(xp3, g_mix, ws_out, li, qm_p.reshape(b, t, MEM_WIDTH), 0, mk_p, mv_p, tm=tm_mix,
                             tok=o_p)

            q_s, k_s, v_s, qm_s = _norm_proj(
                xs, g_in, ws_in, li, 0, (flat(tok_width), flat(tok_width), flat(tok_width), flat(MEM_WIDTH)),
                tm=ms, tn=tn_proj)
            k_s3 = k_s.reshape(bs, SAMPLE_ROWS, tok_width)
            v_s3 = v_s.reshape(bs, SAMPLE_ROWS, tok_width)
            q_rows = q_s.reshape(bs, SAMPLE_ROWS, n_heads, HEAD_DIM).transpose(0, 2, 1, 3)
            q_rows = q_rows.reshape(bs, n_heads * SAMPLE_ROWS, HEAD_DIM)
            bias_rows = jnp.repeat(bias, SAMPLE_ROWS)[:, None]
            o_rows = _sb_decode(q_rows, k_s3, v_s3, bias_rows, cache_k, cache_v, page_flat, li,
                                pages_per_step=8)
            o_s = o_rows.reshape(bs, n_heads, SAMPLE_ROWS, HEAD_DIM).transpose(0, 2, 1, 3)
            o_s = o_s.reshape(bs, SAMPLE_ROWS, tok_width)
            xs3 = _mixer_out(xs3, g_mix, ws_out, li, qm_s.reshape(bs, SAMPLE_ROWS, MEM_WIDTH), 0, mk_s, mv_s,
                             tm=SAMPLE_ROWS, nb=bs, tok=o_s)
            ks.append(k_s3[:, :ts].reshape(bs, ts, n_heads, HEAD_DIM))
            vs.append(v_s3[:, :ts].reshape(bs, ts, n_heads, HEAD_DIM))
        xp = xp3.reshape(b * t, d)
        xs = xs3.reshape(ms, d)

        xs, *w_bf = _ffn_half_casting(xs, norm_ffn2[i], ffn2_w_gate_up, ffn2_w_down, i, tf=tf_ffn)
        xp = _ffn_half(xp, norm_ffn2[i], *w_bf, tm=tm_ffn)

    y_p = xp.reshape(b, t, d)
    y_s = xs.reshape(bs, SAMPLE_ROWS, d)[:, :ts]
    k_p, v_p = (jnp.transpose(a, (0, 1, 3, 2, 4)) for a in qkv_p[1:])
    return (y_p, y_s, jnp.stack(pool_p), k_p, v_p, jnp.stack(mkp), jnp.stack(mvp),
            jnp.stack(pool_s), jnp.stack(ks), jnp.stack(vs))
```

```python
import functools

import jax
import jax.numpy as jnp
from jax import lax
from jax.experimental import pallas as pl
from jax.experimental.pallas import tpu as pltpu

F32 = jnp.float32
BF16 = jnp.bfloat16

HEAD_DIM = 128
MEM_HEADS = 4
MEM_WIDTH = MEM_HEADS * HEAD_DIM
POOL_WINDOWS = (2, 4, 8, 16)
POOL_BUF = max(POOL_WINDOWS) - 1
HALO = 16
N_MIXERS = 2
SB_BLOCK = 128
SB_QTILE = 512
SB_KCHUNK = 256
RMS_EPS = 1e-6
SAMPLE_ROWS = 8
LOG2E = 1.4426950408889634
SB_Q_GAIN = HEAD_DIM ** -0.5 * LOG2E
MIB = 1024 * 1024
VMEM_CAP = 62 * MIB

_NT = (((1,), (1,)), ((), ()))


def _params(semantics, vmem_bytes):
    return pltpu.CompilerParams(dimension_semantics=semantics,
                                vmem_limit_bytes=int(min(vmem_bytes, VMEM_CAP)))


def _rms(x, g):
    ms = jnp.mean(x * x, axis=-1, keepdims=True)
    return x * lax.rsqrt(ms + RMS_EPS) * g


def _rms_rows(src_ref, gain, dst_ref, residual_ref=None, zero_ref=None):
    chunk = 16
    group = min(8, src_ref.shape[0] // chunk)
    assert src_ref.shape[0] % (group * chunk) == 0

    def body(c, carry):
        for s in range(group):
            rows = pl.ds(pl.multiple_of((c * group + s) * chunk, chunk), chunk)
            y = _rms(src_ref[rows, :], gain)
            if residual_ref is not None:
                y = residual_ref[rows, :] + y
            dst_ref[rows, :] = y.astype(dst_ref.dtype)
            if zero_ref is not None:
                zero_ref[rows, :] = jnp.zeros((chunk, zero_ref.shape[1]), zero_ref.dtype)
        return carry

    lax.fori_loop(0, src_ref.shape[0] // (group * chunk), body, 0)


def _suffix_matrix(n):
    r = lax.broadcasted_iota(jnp.int32, (n, n), 0)
    c = lax.broadcasted_iota(jnp.int32, (n, n), 1)
    return jnp.where(r > c, 1.0, 0.0).astype(BF16)


def _sb_weights(z2, vis, umat, carry):
    neg_abs = lax.bitcast_convert_type(lax.bitcast_convert_type(z2, jnp.uint32) | jnp.uint32(0x80000000), F32)
    sp = jnp.maximum(z2, 0.0) + jnp.log2(1.0 + jnp.exp2(neg_abs))
    if vis is not None:
        sp = jnp.where(vis, sp, 0.0)
    later = jnp.dot(sp.astype(BF16), umat, preferred_element_type=F32)
    w = jnp.exp2(z2 - sp - later - carry)
    if vis is not None:
        w = jnp.where(vis, w, 0.0)
    return w, carry + jnp.sum(sp, axis=-1, keepdims=True)


def _ffn_kernel(x_ref, g_ref, wg_ref, wu_ref, wd_ref, o_ref, *refs, acc_cols, emit_weights):
    h_ref, acc_ref = refs[-2:]
    j = pl.program_id(1)

    @pl.when(j == 0)
    def _():
        _rms_rows(x_ref, g_ref[0:1, :], h_ref, zero_ref=acc_ref)

    if emit_weights:
        for src, dst in zip((wg_ref, wu_ref, wd_ref), refs[:3]):
            dst[...] = src[...].astype(BF16)
        wg_ref, wu_ref, wd_ref = refs[:3]
    h = h_ref[...]
    gate = jnp.dot(h, wg_ref[...], preferred_element_type=F32)
    up = jnp.dot(h, wu_ref[...], preferred_element_type=F32)
    a = (gate * jax.nn.sigmoid(gate) * up).astype(BF16)
    for c0 in range(0, acc_ref.shape[1], acc_cols):
        cols = slice(c0, c0 + acc_cols)
        acc_ref[:, cols] += jnp.dot(a, wd_ref[:, cols], preferred_element_type=F32)

    @pl.when(j == pl.num_programs(1) - 1)
    def _():
        _rms_rows(acc_ref, 0.5 * g_ref[1:2, :], o_ref, residual_ref=x_ref)


def _ffn_half(x, g, w_gate, w_up, w_down, *, tm, tf):
    m, d = x.shape
    f = w_down.shape[0]
    vmem = 2 * (2 * tm * d * 4) + tm * d * (2 + 4) + 2 * 3 * d * tf * 2 + 2 * tm * tf * 4 + 2 * MIB
    return pl.pallas_call(
        functools.partial(_ffn_kernel, acc_cols=min(d, 512), emit_weights=False),
        grid=(m // tm, f // tf),
        in_specs=[pl.BlockSpec((tm, d), lambda i, j: (i, 0)),
                  pl.BlockSpec((2, d), lambda i, j: (0, 0)),
                  pl.BlockSpec((d, tf), lambda i, j: (0, j)),
                  pl.BlockSpec((d, tf), lambda i, j: (0, j)),
                  pl.BlockSpec((tf, d), lambda i, j: (j, 0))],
        out_specs=pl.BlockSpec((tm, d), lambda i, j: (i, 0)),
        out_shape=jax.ShapeDtypeStruct((m, d), F32),
        scratch_shapes=[pltpu.VMEM((tm, d), BF16), pltpu.VMEM((tm, d), F32)],
        compiler_params=_params(("parallel", "arbitrary"), vmem),
        name="ffn_half",
    )(x, g, w_gate, w_up, w_down)


def _ffn_half_casting(x, g, w_gu, w_d, layer, *, tf):
    m, d = x.shape
    f = w_d.shape[1]
    nf = f // tf
    vmem = 4 * m * d * 4 + 2 * 3 * d * tf * (4 + 2) + 3 * d * tf * 2 + 6 * m * tf * 4 + 4 * MIB
    return pl.pallas_call(
        functools.partial(_ffn_kernel, acc_cols=min(d, 512), emit_weights=True),
        grid=(1, nf),
        in_specs=[pl.BlockSpec((m, d), lambda i, j: (0, 0)),
                  pl.BlockSpec((2, d), lambda i, j: (0, 0)),
                  pl.BlockSpec((None, d, tf), lambda i, j: (layer, 0, j)),
                  pl.BlockSpec((None, d, tf), lambda i, j: (layer, 0, j + nf)),
                  pl.BlockSpec((None, tf, d), lambda i, j: (layer, j, 0))],
        out_specs=[pl.BlockSpec((m, d), lambda i, j: (0, 0)),
                   pl.BlockSpec((d, tf), lambda i, j: (0, j)),
                   pl.BlockSpec((d, tf), lambda i, j: (0, j)),
                   pl.BlockSpec((tf, d), lambda i, j: (j, 0))],
        out_shape=[jax.ShapeDtypeStruct((m, d), F32), jax.ShapeDtypeStruct((d, f), BF16),
                   jax.ShapeDtypeStruct((d, f), BF16), jax.ShapeDtypeStruct((f, d), BF16)],
        scratch_shapes=[pltpu.VMEM((m, d), BF16), pltpu.VMEM((m, d), F32)],
        compiler_params=_params(("arbitrary", "arbitrary"), vmem),
        name="ffn_half_casting",
    )(x, g, w_gu, w_gu, w_d)


def _norm_proj_kernel(x_ref, g_ref, w_ref, *refs, bounds, by_head, n_carried, first_gain):
    o_refs, h_ref = refs[n_carried:-1], refs[-1]
    j = pl.program_id(1)

    @pl.when(j == 0)
    def _():
        h_ref[...] = _rms(x_ref[...], g_ref[...]).astype(BF16)

    for si, (o_ref, (lo, hi), heads) in enumerate(zip(o_refs, bounds, by_head)):
        @pl.when(jnp.logical_and(j >= lo, j < hi))
        def _(o_ref=o_ref, heads=heads, gain=first_gain if si == 0 else 1.0):
            out = jnp.dot(h_ref[...], w_ref[...], preferred_element_type=F32)
            if gain != 1.0:
                out = out * gain
            out = out.astype(o_ref.dtype)
            if heads:
                for hh in range(o_ref.shape[1]):
                    o_ref[0, hh] = out[:, hh * HEAD_DIM:(hh + 1) * HEAD_DIM]
            else:
                o_ref[...] = out


def _norm_proj(x, g, w, layer, col0, segments, *, tm, tn, seq_len=None, slots=None, first_gain=1.0):
    m, d = x.shape
    assert col0 % tn == 0 and m % tm == 0 and all(n % tn == 0 for n, _, _ in segments)
    cb = col0 // tn
    n_slots, slot, carried = slots if slots is not None else (1, 0, {})
    bounds, lo = [], 0
    for n, _, _ in segments:
        bounds.append((lo, lo + n // tn))
        lo += n // tn
    out_specs, out_shapes, aliases, carried_args = [], [], {}, []
    for si, ((n, dtype, heads), (a, b)) in enumerate(zip(segments, bounds)):
        local = lambda j, a=a, b=b: jnp.clip(j - a, 0, b - a - 1)
        if heads:
            assert seq_len % tm == 0 and tn % HEAD_DIM == 0
            tiles = seq_len // tm
            out_specs.append(pl.BlockSpec((None, 1, tn // HEAD_DIM, tm, HEAD_DIM),
                                          lambda i, j, local=local: (slot, i // tiles, local(j), i % tiles, 0)))
            out_shapes.append(jax.ShapeDtypeStruct((n_slots, m // seq_len, n // HEAD_DIM, seq_len, HEAD_DIM), dtype))
            if si in carried:
                aliases[3 + len(carried_args)] = si
                carried_args.append(carried[si])
        else:
            out_specs.append(pl.BlockSpec((tm, tn), lambda i, j, local=local: (i, local(j))))
            out_shapes.append(jax.ShapeDtypeStruct((m, n), dtype))
    vmem = 2 * tm * d * 4 + tm * d * 2 + 2 * d * tn * 2 + (2 * len(segments) + 2) * tm * tn * 4 + 4 * MIB
    return pl.pallas_call(
        functools.partial(_norm_proj_kernel, bounds=tuple(bounds), by_head=tuple(s[2] for s in segments),
                          first_gain=first_gain,
                          n_carried=len(carried_args)),
        grid=(m // tm, lo),
        in_specs=[pl.BlockSpec((tm, d), lambda i, j: (i, 0)),
                  pl.BlockSpec((1, d), lambda i, j: (0, 0)),
                  pl.BlockSpec((None, d, tn), lambda i, j: (layer, 0, j + cb))]
                 + [pl.BlockSpec(memory_space=pl.ANY)] * len(carried_args),
        out_specs=out_specs,
        out_shape=out_shapes,
        input_output_aliases=aliases,
        scratch_shapes=[pltpu.VMEM((tm, d), BF16)],
        compiler_params=_params(("parallel", "arbitrary"), vmem),
        name="norm_proj",
    )(x, g, w, *carried_args)


def _sb_prompt_kernel(bias_ref, q_ref, k_ref, v_ref, o_ref, kb_ref, vb_ref, u_ref):
    tq, kc = SB_QTILE, SB_KCHUNK
    kb_ref[...] = k_ref[0, 0].astype(BF16)
    vb_ref[...] = v_ref[0, 0].astype(BF16)
    u_ref[...] = _suffix_matrix(kc)
    bias = bias_ref[pl.program_id(1)] * LOG2E
    r = lax.broadcasted_iota(jnp.int32, (tq, kc), 0)
    c = lax.broadcasted_iota(jnp.int32, (tq, kc), 1)
    for i in range(q_ref.shape[2] // tq):
        q = q_ref[0, 0, i * tq:(i + 1) * tq, :]
        acc = jnp.zeros((tq, HEAD_DIM), F32)
        carry = jnp.zeros((tq, 1), F32)
        for j in range((i + 1) * tq // kc - 1, -1, -1):
            rows = slice(j * kc, (j + 1) * kc)
            vis = (c + (j * kc - i * tq) < r) if (j + 1) * kc > i * tq else None
            z = lax.dot_general(q, kb_ref[rows, :], _NT, preferred_element_type=F32) + bias
            w, carry = _sb_weights(z, vis, u_ref[...], carry)
            acc = acc + jnp.dot(w.astype(BF16), vb_ref[rows, :], preferred_element_type=F32)
        o_ref[0, i * tq:(i + 1) * tq, :] = acc.astype(o_ref.dtype)


def _sb_prompt(q, k, v, bias, slot):
    _, b, nh, t, _ = q.shape
    assert t % SB_QTILE == 0 and SB_QTILE % SB_KCHUNK == 0
    head_spec = pl.BlockSpec((None, 1, 1, t, HEAD_DIM), lambda bb, h: (slot, bb, h, 0, 0))
    return pl.pallas_call(
        _sb_prompt_kernel,
        grid=(b, nh),
        in_specs=[pl.BlockSpec(memory_space=pltpu.SMEM), head_spec, head_spec, head_spec],
        out_specs=pl.BlockSpec((1, t, HEAD_DIM), lambda bb, h: (bb, 0, h)),
        out_shape=jax.ShapeDtypeStruct((b, t, nh * HEAD_DIM), BF16),
        scratch_shapes=[pltpu.VMEM((t, HEAD_DIM), BF16), pltpu.VMEM((t, HEAD_DIM), BF16),
                        pltpu.VMEM((SB_KCHUNK, SB_KCHUNK), BF16)],
        compiler_params=_params(("parallel", "parallel"), 32 * MIB),
        name="sb_prompt",
    )(bias, q, k, v)


def _sb_decode_kernel(pt_ref, q_ref, knew_ref, vnew_ref, bias_ref, *refs, pages_per_step, n_heads):
    del pt_ref
    k_refs = refs[:pages_per_step]
    v_refs = refs[pages_per_step:2 * pages_per_step]
    o_ref, acc_ref, car_ref = refs[2 * pages_per_step:]
    p = pl.program_id(1)
    rows = n_heads * SAMPLE_ROWS
    scale = HEAD_DIM ** -0.5 * LOG2E
    chunk = min(SB_KCHUNK, pages_per_step * SB_BLOCK)
    umat = _suffix_matrix(chunk)
    bias = bias_ref[...] * LOG2E
    q = q_ref[0]
    qh = [q[h * SAMPLE_ROWS:(h + 1) * SAMPLE_ROWS].astype(BF16) for h in range(n_heads)]

    def sweep(keys, values, n_keys, vis):
        zs = [lax.dot_general(qh[h], keys(h), _NT, preferred_element_type=F32) for h in range(n_heads)]
        z = jnp.concatenate(zs, axis=0) * scale + bias
        step = min(chunk, n_keys)
        carry = car_ref[...]
        ws = []
        for c0 in range(n_keys - step, -1, -step):
            w, carry = _sb_weights(z[:, c0:c0 + step], vis, umat[:step, :step], carry)
            ws.insert(0, w)
        w = jnp.concatenate(ws, axis=1)
        outs = [jnp.dot(w[h * SAMPLE_ROWS:(h + 1) * SAMPLE_ROWS].astype(BF16), values(h),
                        preferred_element_type=F32) for h in range(n_heads)]
        acc_ref[...] += jnp.concatenate(outs, axis=0)
        car_ref[...] = carry

    @pl.when(p == 0)
    def _():
        acc_ref[...] = jnp.zeros_like(acc_ref)
        car_ref[...] = jnp.zeros_like(car_ref)
        t = lax.broadcasted_iota(jnp.int32, (rows, SB_BLOCK), 0) % SAMPLE_ROWS
        s = lax.broadcasted_iota(jnp.int32, (rows, SB_BLOCK), 1)
        zeros = jnp.zeros((SB_BLOCK - SAMPLE_ROWS, HEAD_DIM), F32)

        def new_block(ref, h):
            head = ref[0, :, h * HEAD_DIM:(h + 1) * HEAD_DIM]
            return jnp.concatenate([head, zeros], axis=0).astype(BF16)

        sweep(functools.partial(new_block, knew_ref), functools.partial(new_block, vnew_ref), SB_BLOCK, s < t)

    def pages(page_refs, h):
        return jnp.concatenate([r[h].astype(BF16) for r in reversed(page_refs)], axis=0)

    sweep(functools.partial(pages, k_refs), functools.partial(pages, v_refs), pages_per_step * SB_BLOCK, None)

    @pl.when(p == pl.num_programs(1) - 1)
    def _():
        o_ref[0] = acc_ref[...]


def _sb_decode(q_rows, k_new, v_new, bias_rows, cache_k, cache_v, page_flat, layer, *, pages_per_step):
    bsz, rows, _ = q_rows.shape
    n_heads = rows // SAMPLE_ROWS
    width = n_heads * HEAD_DIM
    page = cache_k.shape[3]
    n_pages = page_flat.shape[0] // bsz
    assert page == SB_BLOCK and n_pages % pages_per_step == 0
    steps = n_pages // pages_per_step

    def page_map(c):
        return lambda b, p, pt: (layer, pt[b * n_pages + n_pages - 1 - (p * pages_per_step + c)], 0, 0, 0)

    def page_specs():
        return [pl.BlockSpec((None, None, n_heads, page, HEAD_DIM), page_map(c)) for c in range(pages_per_step)]

    seq_map = lambda b, p, pt: (b, 0, 0)
    grid_spec = pltpu.PrefetchScalarGridSpec(
        num_scalar_prefetch=1,
        grid=(bsz, steps),
        in_specs=[pl.BlockSpec((1, rows, HEAD_DIM), seq_map),
                  pl.BlockSpec((1, SAMPLE_ROWS, width), seq_map),
                  pl.BlockSpec((1, SAMPLE_ROWS, width), seq_map),
                  pl.BlockSpec((rows, 1), lambda b, p, pt: (0, 0))] + page_specs() + page_specs(),
        out_specs=pl.BlockSpec((1, rows, HEAD_DIM), seq_map),
        scratch_shapes=[pltpu.VMEM((rows, HEAD_DIM), F32), pltpu.VMEM((rows, 1), F32)],
    )
    vmem = 2 * 2 * pages_per_step * page * width * 4 + 12 * MIB
    return pl.pallas_call(
        functools.partial(_sb_decode_kernel, pages_per_step=pages_per_step, n_heads=n_heads),
        grid_spec=grid_spec,
        out_shape=jax.ShapeDtypeStruct((bsz, rows, HEAD_DIM), F32),
        compiler_params=_params(("parallel", "arbitrary"), vmem),
        name="sb_decode",
    )(page_flat, q_rows, k_new, v_new, bias_rows, *([cache_k] * pages_per_step), *([cache_v] * pages_per_step))


def _pool_mix(ubuf_ref, gm_ref, sc_ref, cat_ref, first_pos, tm):
    tok_width = ubuf_ref.shape[1]
    group = tok_width // len(POOL_WINDOWS)
    pos = first_pos + lax.broadcasted_iota(jnp.int32, (tm, 1), 0)
    for gi, win in enumerate(POOL_WINDOWS):
        cols = slice(gi * group, (gi + 1) * group)
        own = ubuf_ref[HALO:HALO + tm, cols]
        tot = own
        for back in range(1, win):
            tot = tot + ubuf_ref[HALO - back:HALO - back + tm, cols]
        count = jnp.minimum(pos, win).astype(F32)
        diff = tot / count - own
        mixed = jnp.dot(diff.astype(BF16), gm_ref[gi], preferred_element_type=F32) * sc_ref[:, cols]
        cat_ref[:, cols] = mixed.astype(cat_ref.dtype)


def _memory_attend(queries, mk_ref, mv_ref, cat_ref, mem_by_head):
    tok_width = cat_ref.shape[1] - MEM_WIDTH
    scale = HEAD_DIM ** -0.5
    if mem_by_head:
        mk_heads = pltpu.einshape("mhd->hmd", mk_ref[...].astype(BF16))
        mv_heads = pltpu.einshape("mhd->hmd", mv_ref[...].astype(BF16))
    for h in range(MEM_HEADS):
        cols = slice(h * HEAD_DIM, (h + 1) * HEAD_DIM)
        qh = queries(cols).astype(BF16)
        if mem_by_head:
            kh, vh = mk_heads[h], mv_heads[h]
        else:
            kh, vh = mk_ref[:, cols].astype(BF16), mv_ref[:, cols].astype(BF16)
        s = lax.dot_general(qh, kh, _NT, preferred_element_type=F32) * scale
        e = jnp.exp(s - jnp.max(s, axis=-1, keepdims=True))
        prob = e / jnp.sum(e, axis=-1, keepdims=True)
        oh = jnp.dot(prob.astype(BF16), vh, preferred_element_type=F32)
        cat_ref[:, tok_width + h * HEAD_DIM:tok_width + (h + 1) * HEAD_DIM] = oh.astype(cat_ref.dtype)


def _mixer_kernel(*refs, pool, n_prefix, halo_valid_from, mem_by_head):
    if pool:
        (u_ref, halo_ref, qm_ref, mk_ref, mv_ref, x_ref, g_ref, wo_ref, gm_ref, sc_ref,
         o_ref, cat_ref, ubuf_ref) = refs
    else:
        tok_ref, qm_ref, mk_ref, mv_ref, x_ref, g_ref, wo_ref, o_ref, cat_ref = refs
    t = pl.program_id(1)
    nb, tm, d = x_ref.shape
    tok_width = cat_ref.shape[1] - MEM_WIDTH

    for s in range(nb):
        cat_s = cat_ref.at[s * tm:(s + 1) * tm]
        if pool:
            ubuf_ref[0:HALO, :] = jnp.where(t >= halo_valid_from, halo_ref[s], 0.0)
            ubuf_ref[HALO:HALO + tm, :] = u_ref[s]
            _pool_mix(ubuf_ref, gm_ref, sc_ref, cat_s, t * tm + n_prefix + 1, tm)
        else:
            cat_s[:, 0:tok_width] = tok_ref[s].astype(cat_ref.dtype)
        _memory_attend(lambda cols, s=s: qm_ref[s, :, cols], mk_ref.at[s], mv_ref.at[s], cat_s, mem_by_head)
    y = jnp.dot(cat_ref[...].astype(BF16), wo_ref[...], preferred_element_type=F32)
    out = x_ref[...].reshape(nb * tm, d) + _rms(y, g_ref[1:2, :])
    o_ref[...] = out.reshape(nb, tm, d)


def _pool_layer_kernel(x_ref, g_ref, win_ref, mk_ref, mv_ref, wo_ref, gm_ref, sc_ref, o_ref, tail_ref,
                       cat_ref, ubuf_ref):
    t = pl.program_id(1)
    tm = x_ref.shape[1]
    tok_width = cat_ref.shape[1] - MEM_WIDTH

    @pl.when(t == 0)
    def _():
        ubuf_ref[0:HALO, :] = jnp.zeros((HALO, tok_width), F32)

    h = _rms(x_ref[0], g_ref[0:1, :]).astype(BF16)
    proj = jnp.dot(h, win_ref[...], preferred_element_type=F32)
    ubuf_ref[HALO:HALO + tm, :] = proj[:, 0:tok_width]
    _pool_mix(ubuf_ref, gm_ref, sc_ref, cat_ref, t * tm + 1, tm)
    _memory_attend(lambda cols: proj[:, tok_width + cols.start:tok_width + cols.stop], mk_ref.at[0], mv_ref.at[0],
                   cat_ref, False)
    y = jnp.dot(cat_ref[...], wo_ref[...], preferred_element_type=F32)
    o_ref[0] = x_ref[0] + _rms(y, g_ref[1:2, :])
    tail = ubuf_ref[tm:tm + HALO, :]
    ubuf_ref[0:HALO, :] = tail
    tail_ref[0] = tail


def _pool_layer(x, g, w_in, w_out, layer, mk, mv, group_maps, pool_scale, *, tm):
    b, t, d = x.shape
    tok_width = d - MEM_WIDTH
    mem_len = mk.shape[1]
    _, ng, gw, _ = group_maps.shape
    x_spec = pl.BlockSpec((1, tm, d), lambda bb, i: (bb, i, 0))
    mem_spec = pl.BlockSpec((1, mem_len, MEM_WIDTH), lambda bb, i: (bb, 0, 0))
    weight_spec = pl.BlockSpec((None, d, d), lambda bb, i: (layer, 0, 0), pipeline_mode=pl.Buffered(1))
    vmem = (2 * d * d * 2 + 2 * (2 * tm * d * 4 + 4 * mem_len * MEM_WIDTH * 4 + ng * gw * gw * 2)
            + (HALO + tm) * tok_width * 4 + tm * d * 2 + 4 * tm * d * 4 + 6 * MIB)
    return pl.pallas_call(
        _pool_layer_kernel,
        grid=(b, t // tm),
        in_specs=[x_spec, pl.BlockSpec((2, d), lambda bb, i: (0, 0)), weight_spec, mem_spec, mem_spec, weight_spec,
                  pl.BlockSpec((None, ng, gw, gw), lambda bb, i: (layer, 0, 0, 0)),
                  pl.BlockSpec((1, tok_width), lambda bb, i: (0, 0))],
        out_specs=[x_spec, pl.BlockSpec((1, HALO, tok_width), lambda bb, i: (bb, 0, 0))],
        out_shape=[jax.ShapeDtypeStruct((b, t, d), F32), jax.ShapeDtypeStruct((b, HALO, tok_width), F32)],
        scratch_shapes=[pltpu.VMEM((tm, d), BF16), pltpu.VMEM((HALO + tm, tok_width), F32)],
        compiler_params=_params(("parallel", "arbitrary"), vmem),
        name="pool_layer",
    )(x, g, w_in, mk, mv, w_out, group_maps, pool_scale)


def _mixer_out(x, g, w_out, layer, qm_src, qm_block, mk, mv, *, tm, tok=None, pool_src=None, halo_src=None,
               halo_valid_from=0, n_prefix=0, group_maps=None, pool_scale=None, nb=1):
    b, t, d = x.shape
    tok_width = d - MEM_WIDTH
    mem_len = mk.shape[1]
    pool = tok is None
    mem_by_head = mk.ndim == 4
    rows = nb * tm
    qm_spec = pl.BlockSpec((nb, tm, MEM_WIDTH), lambda bb, i: (bb, i, qm_block))
    if mem_by_head:
        mem_spec = pl.BlockSpec((nb, mem_len, MEM_HEADS, HEAD_DIM), lambda bb, i: (bb, 0, 0, 0))
    else:
        mem_spec = pl.BlockSpec((nb, mem_len, MEM_WIDTH), lambda bb, i: (bb, 0, 0))
    x_spec = pl.BlockSpec((nb, tm, d), lambda bb, i: (bb, i, 0))
    g_spec = pl.BlockSpec((2, d), lambda bb, i: (0, 0))
    wo_spec = pl.BlockSpec((None, d, d), lambda bb, i: (layer, 0, 0))
    scratch = [pltpu.VMEM((rows, d), BF16 if nb == 1 or tm % 16 == 0 else F32)]
    if pool:
        halo_blocks = tm // HALO
        if halo_src is pool_src:
            halo_map = lambda bb, i: (bb, jnp.maximum(i * halo_blocks - 1, 0), 0)
        else:
            halo_map = lambda bb, i: (bb, 0, 0)
        _, ng, gw, _ = group_maps.shape
        in_specs = [pl.BlockSpec((nb, tm, tok_width), lambda bb, i: (bb, i, 0)),
                    pl.BlockSpec((nb, HALO, tok_width), halo_map),
                    qm_spec, mem_spec, mem_spec, x_spec, g_spec, wo_spec,
                    pl.BlockSpec((None, ng, gw, gw), lambda bb, i: (layer, 0, 0, 0)),
                    pl.BlockSpec((1, tok_width), lambda bb, i: (0, 0))]
        args = (pool_src, halo_src, qm_src, mk, mv, x, g, w_out, group_maps, pool_scale)
        scratch.append(pltpu.VMEM((HALO + tm, tok_width), F32))
    else:
        in_specs = [pl.BlockSpec((nb, tm, tok_width), lambda bb, i: (bb, i, 0)),
                    qm_spec, mem_spec, mem_spec, x_spec, g_spec, wo_spec]
        args = (tok, qm_src, mk, mv, x, g, w_out)
    vmem = (2 * (rows * tok_width * 4 + rows * MEM_WIDTH * 4 + 2 * rows * d * 4 + 4 * nb * mem_len * MEM_WIDTH * 4
                 + d * d * 2)
            + (HALO + tm) * tok_width * 4 + rows * d * 2 + 3 * rows * d * 4 + 6 * MIB)
    return pl.pallas_call(
        functools.partial(_mixer_kernel, pool=pool, n_prefix=n_prefix, halo_valid_from=halo_valid_from,
                          mem_by_head=mem_by_head),
        grid=(b // nb, t // tm),
        in_specs=in_specs,
        out_specs=x_spec,
        out_shape=jax.ShapeDtypeStruct((b, t, d), F32),
        scratch_shapes=scratch,
        compiler_params=_params(("parallel", "arbitrary"), vmem),
        name="mixer_pool" if pool else "mixer_sb",
    )(*args)


def kernel(x_prompt, x_sample, state_pool, cache_sb_k, cache_sb_v, cache_mem_k, cache_mem_v, page_table, mem_prompt, norm_ffn1, ffn1_w_gate_up, ffn1_w_down, norm_mix, norm_mem, mem_w_kv, pool_w_in, pool_group_maps, pool_scale, pool_w_out, sb_w_in, sb_logit_bias, sb_w_out, norm_ffn2, ffn2_w_gate_up, ffn2_w_down):
    b, t, d = x_prompt.shape
    bs, ts, _ = x_sample.shape
    depth = norm_ffn1.shape[0]
    mem_len = mem_prompt.shape[1]
    tok_width = d - MEM_WIDTH
    n_heads = tok_width // HEAD_DIM
    assert ts <= SAMPLE_ROWS

    tm_ffn, tf_ffn = 1024, 512
    tm_proj, tn_proj = 1024, 512
    tm_mix = 512
    ms = bs * SAMPLE_ROWS

    w_kv = mem_w_kv.astype(BF16)
    wp_in, wp_out, gmaps = pool_w_in.astype(BF16), pool_w_out.astype(BF16), pool_group_maps.astype(BF16)
    ws_in, ws_out = sb_w_in.astype(BF16), sb_w_out.astype(BF16)

    xp = x_prompt.reshape(b * t, d)
    xs = jnp.pad(x_sample, ((0, 0), (0, SAMPLE_ROWS - ts), (0, 0))).reshape(ms, d)
    mem2 = mem_prompt.reshape(b * mem_len, d)
    page_flat = page_table.reshape(-1).astype(jnp.int32)
    cache_k = jnp.transpose(cache_sb_k, (0, 1, 3, 2, 4))
    cache_v = jnp.transpose(cache_sb_v, (0, 1, 3, 2, 4))

    pool_p, pool_s, ks, vs, mkp, mvp = [], [], [], [], [], []
    qkv_p = None
    for i in range(depth):
        li = i // N_MIXERS
        xs, *w_bf = _ffn_half_casting(xs, norm_ffn1[i], ffn1_w_gate_up, ffn1_w_down, i, tf=tf_ffn)
        xp = _ffn_half(xp, norm_ffn1[i], *w_bf, tm=tm_ffn, tf=tf_ffn)

        g_mem = norm_mem[i][None]
        flat = lambda n: (n, F32, False)
        mk_p, mv_p = _norm_proj(mem2, g_mem, w_kv, i, 0, (flat(MEM_WIDTH), flat(MEM_WIDTH)), tm=tm_proj, tn=tn_proj)
        mk_p = mk_p.reshape(b, mem_len, MEM_WIDTH)
        mv_p = mv_p.reshape(b, mem_len, MEM_WIDTH)
        mkp.append(mk_p.reshape(b, mem_len, MEM_HEADS, HEAD_DIM))
        mvp.append(mv_p.reshape(b, mem_len, MEM_HEADS, HEAD_DIM))
        mk_s, mv_s = cache_mem_k[i], cache_mem_v[i]

        g_mix = norm_mix[i]
        g_in = g_mix[0:1]
        xp3 = xp.reshape(b, t, d)
        xs3 = xs.reshape(bs, SAMPLE_ROWS, d)
        if i % N_MIXERS == 0:
            scale = pool_scale[li][None]
            xp3, tail_p = _pool_layer(xp3, g_mix, wp_in, wp_out, li, mk_p, mv_p, gmaps, scale, tm=tm_mix)
            (proj_s,) = _norm_proj(xs, g_in, wp_in, li, 0, (flat(d),), tm=ms, tn=tn_proj)
            proj_s = proj_s.reshape(bs, SAMPLE_ROWS, d)
            qm_block = tok_width // MEM_WIDTH
            prefix = state_pool[li]
            n_prefix = prefix.shape[1]
            halo_s = jnp.pad(prefix, ((0, 0), (HALO - n_prefix, 0), (0, 0)))
            xs3 = _mixer_out(xs3, g_mix, wp_out, li, proj_s, qm_block, mk_s, mv_s, tm=SAMPLE_ROWS, nb=bs, pool_src=proj_s,
                             halo_src=halo_s, halo_valid_from=0, n_prefix=n_prefix, group_maps=gmaps,
                             pool_scale=scale)
            pool_p.append(tail_p[:, HALO - POOL_BUF:])
            u_ext = jnp.concatenate([prefix, proj_s[:, :ts, :tok_width]], axis=1)
            pool_s.append(u_ext[:, u_ext.shape[1] - POOL_BUF:])
        else:
            bias = sb_logit_bias[li].astype(F32)
            carried = {} if qkv_p is None else dict(enumerate(qkv_p))
            *qkv_p, qm_p = _norm_proj(
                xp, g_in, ws_in, li, 0,
                ((tok_width, BF16, True), (tok_width, F32, True), (tok_width, F32, True), flat(MEM_WIDTH)),
                tm=tm_proj, tn=tn_proj, seq_len=t, slots=(sb_w_in.shape[0], li, carried), first_gain=SB_Q_GAIN)
            o_p = _sb_prompt(*qkv_p, bias, li)
            xp3 = _mixer_out(xp3, g_mix, ws_out, li, qm_p.reshape(b, t, MEM_WIDTH), 0, mk_p, mv_p, tm=tm_mix,
                             tok=o_p)

            q_s, k_s, v_s, qm_s = _norm_proj(
                xs, g_in, ws_in, li, 0, (flat(tok_width), flat(tok_width), flat(tok_width), flat(MEM_WIDTH)),
                tm=ms, tn=tn_proj)
            k_s3 = k_s.reshape(bs, SAMPLE_ROWS, tok_width)
            v_s3 = v_s.reshape(bs, SAMPLE_ROWS, tok_width)
            q_rows = q_s.reshape(bs, SAMPLE_ROWS, n_heads, HEAD_DIM).transpose(0, 2, 1, 3)
            q_rows = q_rows.reshape(bs, n_heads * SAMPLE_ROWS, HEAD_DIM)
            bias_rows = jnp.repeat(bias, SAMPLE_ROWS)[:, None]
            o_rows = _sb_decode(q_rows, k_s3, v_s3, bias_rows, cache_k, cache_v, page_flat, li,
                                pages_per_step=8)
            o_s = o_rows.reshape(bs, n_heads, SAMPLE_ROWS, HEAD_DIM).transpose(0, 2, 1, 3)
            o_s = o_s.reshape(bs, SAMPLE_ROWS, tok_width)
            xs3 = _mixer_out(xs3, g_mix, ws_out, li, qm_s.reshape(bs, SAMPLE_ROWS, MEM_WIDTH), 0, mk_s, mv_s,
                             tm=SAMPLE_ROWS, nb=bs, tok=o_s)
            ks.append(k_s3[:, :ts].reshape(bs, ts, n_heads, HEAD_DIM))
            vs.append(v_s3[:, :ts].reshape(bs, ts, n_heads, HEAD_DIM))
        xp = xp3.reshape(b * t, d)
        xs = xs3.reshape(ms, d)

        xs, *w_bf = _ffn_half_casting(xs, norm_ffn2[i], ffn2_w_gate_up, ffn2_w_down, i, tf=tf_ffn)
        xp = _ffn_half(xp, norm_ffn2[i], *w_bf, tm=tm_ffn, tf=tf_ffn)

    y_p = xp.reshape(b, t, d)
    y_s = xs.reshape(bs, SAMPLE_ROWS, d)[:, :ts]
    k_p, v_p = (jnp.transpose(a, (0, 1, 3, 2, 4)) for a in qkv_p[1:])
    return (y_p, y_s, jnp.stack(pool_p), k_p, v_p, jnp.stack(mkp), jnp.stack(mvp),
            jnp.stack(pool_s), jnp.stack(ks), jnp.stack(vs))
```
